```python
import jax, jax.numpy as jnp
from jax import lax
import numpy as np

D_MODEL = 1024
BATCH = 2
SEQ = 8192
DEPTH = 1
DEC_BATCH = 8
DEC_SEQ = 64
PAST_LEN = 4096

CHUNK = 64
WINDOW = 128
WINDOW_CHUNKS = WINDOW // CHUNK
BAND = (WINDOW_CHUNKS + 1) * CHUNK
HEAD_DIM = 64
N_Q_HEADS = 8
N_KV_HEADS = 2
Q_PER_KV = N_Q_HEADS // N_KV_HEADS
ATTN_WIDTH = N_Q_HEADS * HEAD_DIM
KV_WIDTH = N_KV_HEADS * HEAD_DIM
QKV_COLS = ATTN_WIDTH + 2 * KV_WIDTH
SGU_CHUNK = 128
SGU_HEADS = 4
SGU_HEAD_DIM = 128
SGU_WIDTH = SGU_HEADS * SGU_HEAD_DIM
MIX_WIDTH = ATTN_WIDTH + SGU_WIDTH
IN_COLS = QKV_COLS + 2 * SGU_WIDTH
N_EXPERTS = 32
TOP_K = 4
D_FF = 1024
SWIGLU_LIMIT = 7.0
SWIGLU_ALPHA = 1.702
MOE_BLOCK = 128
RMS_EPS = 1e-5
NEG_INF = -1e30

kernel_name = "hymba_swa_sink_gmlp_moe_stream_step"


def _rmsnorm(x, g):
    xf = x.astype(jnp.float32)
    y = xf * lax.rsqrt(jnp.mean(xf * xf, axis=-1, keepdims=True) + RMS_EPS)
    return (y * g.astype(jnp.float32)).astype(x.dtype)


def _alibi_slopes():
    s = jnp.exp2(-8.0 * jnp.arange(1, N_Q_HEADS + 1, dtype=jnp.float32) / N_Q_HEADS)
    return s.reshape(N_KV_HEADS, Q_PER_KV)


def _sgu_mask():
    sub = jnp.arange(SGU_CHUNK) // CHUNK
    return (sub[None, :] <= sub[:, None])


def _sink_attend(q, k, v, bias, sinks):
    scores = jnp.einsum('...qkgd,...skd->...kgqs', q, k).astype(jnp.float32) * (HEAD_DIM ** -0.5)
    scores = scores + bias
    sink_col = jnp.broadcast_to(sinks.astype(jnp.float32).reshape(N_KV_HEADS, Q_PER_KV, 1, 1),
                                scores.shape[:-1] + (1,))
    probs = jax.nn.softmax(jnp.concatenate([scores, sink_col], axis=-1), axis=-1)[..., :-1]
    return jnp.einsum('...kgqs,...skd->...qkgd', probs.astype(v.dtype), v)


def _split_in(h, w_in, sgu_norm_g):
    cols = h @ w_in
    lead = cols.shape[:-1]
    q = cols[..., :ATTN_WIDTH].reshape(lead + (N_KV_HEADS, Q_PER_KV, HEAD_DIM))
    k = cols[..., ATTN_WIDTH:ATTN_WIDTH + KV_WIDTH].reshape(lead + (N_KV_HEADS, HEAD_DIM))
    v = cols[..., ATTN_WIDTH + KV_WIDTH:QKV_COLS].reshape(lead + (N_KV_HEADS, HEAD_DIM))
    gu = jax.nn.gelu(cols[..., QKV_COLS:QKV_COLS + SGU_WIDTH], approximate=False)
    gv = jax.nn.gelu(cols[..., QKV_COLS + SGU_WIDTH:], approximate=False)
    gv = _rmsnorm(gv.reshape(lead + (SGU_HEADS, SGU_HEAD_DIM)), sgu_norm_g)
    return q, k, v, gu, gv


def _mixer_prompt(h, w_in, sgu_norm_g, sinks, sgu_w, sgu_b):
    b, s, _ = h.shape
    q, k, v, gu, gv = _split_in(h, w_in, sgu_norm_g)
    nc = s // CHUNK
    qc = q.reshape(b, nc, CHUNK, N_KV_HEADS, Q_PER_KV, HEAD_DIM)

    def band_of(t):
        tp = jnp.pad(t, ((0, 0), (WINDOW, 0), (0, 0), (0, 0)))
        tp = tp.reshape(b, nc + WINDOW_CHUNKS, CHUNK, N_KV_HEADS, HEAD_DIM)
        return jnp.concatenate([tp[:, w:w + nc] for w in range(WINDOW_CHUNKS + 1)], axis=2)

    dist = jnp.abs(jnp.arange(CHUNK)[:, None] + WINDOW - jnp.arange(BAND)[None, :]).astype(jnp.float32)
    alibi = -_alibi_slopes()[:, :, None, None] * dist
    key_pos = jnp.arange(nc)[:, None] * CHUNK - WINDOW + jnp.arange(BAND)[None, :]
    mask = jnp.where(key_pos >= 0, 0.0, NEG_INF).astype(jnp.float32).reshape(nc, 1, 1, 1, BAND)
    attn = _sink_attend(qc, band_of(k), band_of(v), alibi + mask, sinks).reshape(b, s, ATTN_WIDTH)

    ns = s // SGU_CHUNK
    vc = gv.reshape(b, ns, SGU_CHUNK, SGU_HEADS, SGU_HEAD_DIM)
    w_s = sgu_w * _sgu_mask().astype(sgu_w.dtype)
    spatial = jnp.einsum('hij,bnjhc->bnihc', w_s, vc) + sgu_b.T[:, :, None]
    sgu = gu * spatial.reshape(b, s, SGU_WIDTH)
    return attn, sgu, k[:, s - WINDOW:], v[:, s - WINDOW:]


def _mixer_sample(h, cache_k, cache_v, w_in, sgu_norm_g, sinks, sgu_w, sgu_b):
    db, l, _ = h.shape
    q, k, v, gu, gv = _split_in(h, w_in, sgu_norm_g)
    k_all = jnp.concatenate([cache_k.astype(k.dtype), k], axis=1)
    v_all = jnp.concatenate([cache_v.astype(v.dtype), v], axis=1)
    dist = jnp.abs(jnp.arange(l)[:, None] + WINDOW - jnp.arange(WINDOW + l)[None, :]).astype(jnp.float32)
    alibi = -_alibi_slopes()[:, :, None, None] * dist
    attn = _sink_attend(q, k_all, v_all, alibi, sinks).reshape(db, l, ATTN_WIDTH)

    w_s = (sgu_w * _sgu_mask().astype(sgu_w.dtype))[:, :l, :l]
    spatial = jnp.einsum('hij,bjhc->bihc', w_s, gv) + sgu_b[:, :l].T[:, :, None]
    sgu = gu * spatial.reshape(db, l, SGU_WIDTH)
    return attn, sgu, k_all[:, l:], v_all[:, l:], gv


def _merge(attn, sgu, attn_out_g, sgu_out_g, w_out):
    return jnp.concatenate([_rmsnorm(attn, attn_out_g), _rmsnorm(sgu, sgu_out_g)], axis=-1) @ w_out


def _moe(h, w_router, b_router, w_gate_up, b_gate_up, w_down, b_down):
    shape = h.shape
    x = h.reshape(-1, D_MODEL)
    t = x.shape[0]
    logits = (x @ w_router + b_router).astype(jnp.float32)
    top_vals, top_idx = lax.top_k(logits, TOP_K)
    gates = jax.nn.softmax(top_vals, axis=-1).astype(x.dtype)
    n = t * TOP_K
    flat_e = top_idx.reshape(n).astype(jnp.int32)
    flat_tok = jnp.repeat(jnp.arange(t, dtype=jnp.int32), TOP_K)
    flat_g = gates.reshape(n)
    order = jnp.argsort(flat_e)
    e_s, tok_s, g_s = flat_e[order], flat_tok[order], flat_g[order]
    counts = jnp.zeros((N_EXPERTS,), jnp.int32).at[flat_e].add(1)
    padded = (counts + MOE_BLOCK - 1) // MOE_BLOCK * MOE_BLOCK
    start = jnp.cumsum(counts) - counts
    pend = jnp.cumsum(padded)
    pstart = pend - padded
    dest = pstart[e_s] + (jnp.arange(n, dtype=jnp.int32) - start[e_s])
    nb = -(-n // MOE_BLOCK) + N_EXPERTS
    slot_tok = jnp.full((nb * MOE_BLOCK,), t, jnp.int32).at[dest].set(tok_s)
    slot_gate = jnp.zeros((nb * MOE_BLOCK,), x.dtype).at[dest].set(g_s)
    block_e = jnp.minimum(jnp.searchsorted(pend, jnp.arange(nb, dtype=jnp.int32) * MOE_BLOCK, side='right'),
                          N_EXPERTS - 1)
    x_pad = jnp.concatenate([x, jnp.zeros((1, D_MODEL), x.dtype)], axis=0)

    def expert_block(args):
        tok, e = args
        xb = x_pad[tok]
        gu = xb @ w_gate_up[e] + b_gate_up[e]
        x_glu = jnp.minimum(gu[:, :D_FF], SWIGLU_LIMIT)
        x_lin = jnp.clip(gu[:, D_FF:], -SWIGLU_LIMIT, SWIGLU_LIMIT)
        act = x_glu * jax.nn.sigmoid(SWIGLU_ALPHA * x_glu) * (x_lin + 1.0)
        return act @ w_down[e] + b_down[e]

    out = lax.map(expert_block, (slot_tok.reshape(nb, MOE_BLOCK), block_e))
    out = out.reshape(nb * MOE_BLOCK, D_MODEL) * slot_gate[:, None]
    y = jax.ops.segment_sum(out, slot_tok, num_segments=t + 1)[:t]
    return y.reshape(shape)


def setup_inputs(seed: int = 0) -> dict:
    key = jax.random.key(seed)
    ks = jax.random.split(key, 21)

    def nrm(k, shape, scale):
        return jax.random.normal(k, shape, jnp.float32) * scale

    L = DEPTH
    return {
        "x_prompt": nrm(ks[0], (BATCH, SEQ, D_MODEL), 1.0),
        "x_sample": nrm(ks[1], (DEC_BATCH, DEC_SEQ, D_MODEL), 1.0),
        "cache_k": nrm(ks[2], (L, DEC_BATCH, WINDOW, N_KV_HEADS, HEAD_DIM), 1.0),
        "cache_v": nrm(ks[3], (L, DEC_BATCH, WINDOW, N_KV_HEADS, HEAD_DIM), 1.0),
        "norm1_g": 1.0 + nrm(ks[4], (L, D_MODEL), 0.05),
        "w_in": nrm(ks[5], (L, D_MODEL, IN_COLS), D_MODEL ** -0.5),
        "sgu_norm_g": 1.0 + nrm(ks[6], (L, SGU_HEADS, SGU_HEAD_DIM), 0.05),
        "attn_sinks": nrm(ks[7], (L, N_Q_HEADS), 1.0),
        "sgu_w": nrm(ks[8], (L, SGU_HEADS, SGU_CHUNK, SGU_CHUNK), SGU_CHUNK ** -0.5),
        "sgu_b": 1.0 + nrm(ks[9], (L, SGU_HEADS, SGU_CHUNK), 0.1),
        "attn_out_g": 1.0 + nrm(ks[10], (L, ATTN_WIDTH), 0.05),
        "sgu_out_g": 1.0 + nrm(ks[11], (L, SGU_WIDTH), 0.05),
        "w_out": nrm(ks[12], (L, MIX_WIDTH, D_MODEL), MIX_WIDTH ** -0.5),
        "norm2_g": 1.0 + nrm(ks[13], (L, D_MODEL), 0.05),
        "w_router": nrm(ks[14], (L, D_MODEL, N_EXPERTS), D_MODEL ** -0.5),
        "b_router": nrm(ks[15], (L, N_EXPERTS), 0.01),
        "w_gate_up": nrm(ks[16], (L, N_EXPERTS, D_MODEL, 2 * D_FF), D_MODEL ** -0.5),
        "b_gate_up": nrm(ks[17], (L, N_EXPERTS, 2 * D_FF), 0.01),
        "w_down": nrm(ks[18], (L, N_EXPERTS, D_FF, D_MODEL), D_FF ** -0.5),
        "b_down": nrm(ks[19], (L, N_EXPERTS, D_MODEL), 0.01),
        "final_g": 1.0 + nrm(ks[20], (D_MODEL,), 0.05),
    }


def reference(x_prompt, x_sample, cache_k, cache_v, norm1_g, w_in, sgu_norm_g, attn_sinks, sgu_w, sgu_b,
              attn_out_g, sgu_out_g, w_out, norm2_g, w_router, b_router, w_gate_up, b_gate_up, w_down,
              b_down, final_g):
    xp, xs = x_prompt, x_sample
    kp_l, vp_l, ks_l, vs_l, gv_l = [], [], [], [], []
    for l in range(DEPTH):
        hp = _rmsnorm(xp, norm1_g[l])
        hs = _rmsnorm(xs, norm1_g[l])
        ap, sp, kpn, vpn = _mixer_prompt(hp, w_in[l], sgu_norm_g[l], attn_sinks[l], sgu_w[l], sgu_b[l])
        a_s, s_s, ksn, vsn, gvs = _mixer_sample(hs, cache_k[l], cache_v[l], w_in[l], sgu_norm_g[l],
                                                attn_sinks[l], sgu_w[l], sgu_b[l])
        xp = xp + _merge(ap, sp, attn_out_g[l], sgu_out_g[l], w_out[l])
        xs = xs + _merge(a_s, s_s, attn_out_g[l], sgu_out_g[l], w_out[l])
        xp = xp + _moe(_rmsnorm(xp, norm2_g[l]), w_router[l], b_router[l], w_gate_up[l], b_gate_up[l],
                       w_down[l], b_down[l])
        xs = xs + _moe(_rmsnorm(xs, norm2_g[l]), w_router[l], b_router[l], w_gate_up[l], b_gate_up[l],
                       w_down[l], b_down[l])
        kp_l.append(kpn)
        vp_l.append(vpn)
        ks_l.append(ksn)
        vs_l.append(vsn)
        gv_l.append(gvs)
    y_prompt = _rmsnorm(xp, final_g)
    y_sample = _rmsnorm(xs, final_g)
    return (y_prompt, y_sample, jnp.stack(kp_l), jnp.stack(vp_l), jnp.stack(ks_l), jnp.stack(vs_l),
            jnp.stack(gv_l))
```

```python
import functools

import jax
import jax.numpy as jnp
from jax import lax
from jax.experimental import pallas as pl
from jax.experimental.pallas import tpu as pltpu

D_MODEL = 1024
BATCH = 2
SEQ = 8192
DEC_BATCH = 8
DEC_SEQ = 64
CHUNK = 64
WINDOW = 128
BAND = WINDOW + CHUNK
HEAD_DIM = 64
N_Q_HEADS = 8
N_KV_HEADS = 2
Q_PER_KV = N_Q_HEADS // N_KV_HEADS
ATTN_WIDTH = N_Q_HEADS * HEAD_DIM
KV_WIDTH = N_KV_HEADS * HEAD_DIM
QKV_COLS = ATTN_WIDTH + 2 * KV_WIDTH
SGU_CHUNK = 128
SGU_HEADS = 4
SGU_HEAD_DIM = 128
SGU_WIDTH = SGU_HEADS * SGU_HEAD_DIM
IN_COLS = QKV_COLS + 2 * SGU_WIDTH
N_EXPERTS = 32
TOP_K = 4
D_FF = 1024
SWIGLU_LIMIT = 7.0
SWIGLU_ALPHA = 1.702
RMS_EPS = 1e-5
NEG_INF = -1e30

T_PROMPT = BATCH * SEQ
T_SAMPLE = DEC_BATCH * DEC_SEQ
T_ALL = T_PROMPT + T_SAMPLE
TOKEN_BLOCK = 512
SEQ_BLOCKS = SEQ // TOKEN_BLOCK
N_TOKEN_BLOCKS = T_ALL // TOKEN_BLOCK
PROMPT_BLOCKS = T_PROMPT // TOKEN_BLOCK
EXPERT_ROWS = 256
N_SLOTS = T_ALL * TOP_K
MIN_ROW_BLOCKS = N_SLOTS // EXPERT_ROWS
N_ROW_BLOCKS = MIN_ROW_BLOCKS + N_EXPERTS
ROW_TILE = 8
FILL_ROWS = EXPERT_ROWS + ROW_TILE
SORTED_BLOCKS = N_ROW_BLOCKS + 2
SORTED_ROWS = SORTED_BLOCKS * EXPERT_ROWS
VMEM_LIMIT_BYTES = 56 * 1024 * 1024

assert T_SAMPLE == TOKEN_BLOCK and SEQ % TOKEN_BLOCK == 0 and N_SLOTS % EXPERT_ROWS == 0

F32 = jnp.float32
BF16 = jnp.bfloat16


def _rms(x, g):
    return x * lax.rsqrt(jnp.mean(x * x, axis=-1, keepdims=True) + RMS_EPS) * g


def _gelu(x):
    return 0.5 * x * (1.0 + lax.erf(x * 0.7071067811865476))


def _in_proj(x, n1g_ref, w_in_ref, sgng_ref):
    h = _rms(x, n1g_ref[...])
    cols = jnp.dot(h.astype(BF16), w_in_ref[...], preferred_element_type=F32)
    q = cols[:, :ATTN_WIDTH] * (HEAD_DIM ** -0.5)
    k = cols[:, ATTN_WIDTH:ATTN_WIDTH + KV_WIDTH]
    v = cols[:, ATTN_WIDTH + KV_WIDTH:QKV_COLS]
    gu = _gelu(cols[:, QKV_COLS:QKV_COLS + SGU_WIDTH])
    gv = _gelu(cols[:, QKV_COLS + SGU_WIDTH:])
    sg = sgng_ref[...]
    gvn = jnp.concatenate(
        [_rms(gv[:, i * SGU_HEAD_DIM:(i + 1) * SGU_HEAD_DIM], sg[:, i * SGU_HEAD_DIM:(i + 1) * SGU_HEAD_DIM])
         for i in range(SGU_HEADS)], axis=1)
    return q, k, v, gu, gvn


def _dup_heads(t):
    lane = lax.broadcasted_iota(jnp.int32, t.shape, 1)
    r = pltpu.roll(t, HEAD_DIM, axis=1)
    return jnp.where(lane < HEAD_DIM, t, r), jnp.where(lane < HEAD_DIM, r, t)


def _attend_chunk(q64, kk, vv, alibi_ref, sink_ref, key_bias):
    lane = lax.broadcasted_iota(jnp.int32, (CHUNK, 2 * HEAD_DIM), 1)
    lo = lane < HEAD_DIM
    pairs = []
    for g in range(N_KV_HEADS):
        rows = []
        for p in range(2):
            qp = q64[:, (2 * g + p) * 128:(2 * g + p + 1) * 128]
            rows.append(jnp.where(lo, qp, 0.0))
            rows.append(jnp.where(lo, 0.0, qp))
        q4 = jnp.concatenate(rows, axis=0).astype(BF16)
        sc = lax.dot_general(q4, kk[g], (((1,), (1,)), ((), ())), preferred_element_type=F32)
        sc = sc + alibi_ref[g]
        if key_bias is not None:
            sc = sc + key_bias
        sink = sink_ref[g]
        m = jnp.maximum(jnp.max(sc, axis=1, keepdims=True), sink)
        p_un = jnp.exp(sc - m)
        den = jnp.sum(p_un, axis=1, keepdims=True) + jnp.exp(sink - m)
        o = jnp.dot(p_un.astype(BF16), vv[g], preferred_element_type=F32) / den
        for p in range(2):
            pairs.append(jnp.where(lo, o[(2 * p) * CHUNK:(2 * p + 1) * CHUNK],
                                   o[(2 * p + 1) * CHUNK:(2 * p + 2) * CHUNK]))
    return jnp.concatenate(pairs, axis=1)


def _sgu_weights(sguw_ref):
    sub_i = lax.broadcasted_iota(jnp.int32, (SGU_CHUNK, SGU_CHUNK), 0) // CHUNK
    sub_j = lax.broadcasted_iota(jnp.int32, (SGU_CHUNK, SGU_CHUNK), 1) // CHUNK
    keep = sub_j <= sub_i
    return [jnp.where(keep, sguw_ref[i], 0.0).astype(BF16) for i in range(SGU_HEADS)]


def _sgu_chunk(ws, gu_c, gvn_c, sgubt_ref):
    outs = []
    for i in range(SGU_HEADS):
        sl = slice(i * SGU_HEAD_DIM, (i + 1) * SGU_HEAD_DIM)
        sp = jnp.dot(ws[i], gvn_c[:, sl].astype(BF16), preferred_element_type=F32) + sgubt_ref[:, i:i + 1]
        outs.append(gu_c[:, sl] * sp)
    return jnp.concatenate(outs, axis=1)


def _merge_route(x, attn, sgu, aog_ref, sog_ref, w_out_ref, n2g_ref, wr_ref, br_ref, cnt_scr,
                 x1_ref, h2_ref, idx_ref, rank_ref, gate_ref):
    n = x.shape[0]
    a_n = _rms(attn, aog_ref[...]).astype(BF16)
    s_n = _rms(sgu, sog_ref[...]).astype(BF16)
    x1 = (x + jnp.dot(a_n, w_out_ref[:ATTN_WIDTH, :], preferred_element_type=F32)
          + jnp.dot(s_n, w_out_ref[ATTN_WIDTH:, :], preferred_element_type=F32))
    x1_ref[...] = x1
    h2 = _rms(x1, n2g_ref[...])
    h2_ref[...] = h2
    logits = lax.dot_general(wr_ref[...], h2, (((1,), (1,)), ((), ())),
                             precision=lax.Precision.HIGHEST, preferred_element_type=F32) + br_ref[...]
    e_iota = lax.broadcasted_iota(jnp.int32, (N_EXPERTS, n), 0).astype(F32)
    cur = logits
    vals, idxs, sels = [], [], []
    for _ in range(TOP_K):
        m = jnp.max(cur, axis=0, keepdims=True)
        idx = jnp.min(jnp.where(cur == m, e_iota, float(N_EXPERTS)), axis=0, keepdims=True)
        sel = e_iota == idx
        vals.append(m)
        idxs.append(idx)
        sels.append(sel)
        cur = jnp.where(sel, -jnp.inf, cur)
    exps = [jnp.exp(v - vals[0]) for v in vals]
    den = exps[0] + exps[1] + exps[2] + exps[3]
    gate_ref[...] = jnp.concatenate([e / den for e in exps], axis=0)
    onehot = jnp.where(sels[0] | sels[1] | sels[2] | sels[3], 1.0, 0.0)
    before = lax.broadcasted_iota(jnp.int32, (n, n), 0) < lax.broadcasted_iota(jnp.int32, (n, n), 1)
    upper = jnp.where(before, 1.0, 0.0).astype(BF16)
    prior = jnp.dot(onehot.astype(BF16), upper, preferred_element_type=F32) + cnt_scr[...]
    ranks = [jnp.sum(jnp.where(s, prior, 0.0), axis=0, keepdims=True) for s in sels]
    idx_ref[...] = jnp.concatenate(idxs, axis=0).astype(jnp.int32)
    rank_ref[...] = jnp.concatenate(ranks, axis=0).astype(jnp.int32)
    cnt_scr[...] = cnt_scr[...] + jnp.sum(onehot, axis=1, keepdims=True)


def _mixer_kernel(xp_ref, xs_in_ref, ck_ref, cv_ref, n1g_ref, w_in_ref, sgng_ref, sink_ref, alibi_ref, sguw_ref,
                  sgubt_ref, aog_ref, sog_ref, w_out_ref, n2g_ref, wr_ref, br_ref,
                  x1_ref, h2_ref, idx_ref, rank_ref, gate_ref, cnt_ref, kp_ref, vp_ref, ks_ref, vs_ref, gv_ref,
                  kk0, kk1, vv0, vv1, cnt_scr):
    n = pl.program_id(0)
    s = n % SEQ_BLOCKS
    weights = (aog_ref, sog_ref, w_out_ref, n2g_ref, wr_ref, br_ref)
    outs = (x1_ref, h2_ref, idx_ref, rank_ref, gate_ref)

    @pl.when(n == 0)
    def _():
        cnt_scr[...] = jnp.zeros_like(cnt_scr)

    @pl.when(n < PROMPT_BLOCKS)
    def _():
        @pl.when(s == 0)
        def _():
            zeros = jnp.zeros((WINDOW, 2 * HEAD_DIM), BF16)
            for buf in (kk0, kk1, vv0, vv1):
                buf[0:WINDOW, :] = zeros

        x = xp_ref[...]
        q, k, v, gu, gvn = _in_proj(x, n1g_ref, w_in_ref, sgng_ref)

        @pl.when(s == SEQ_BLOCKS - 1)
        def _():
            kp_ref[...] = k[TOKEN_BLOCK - WINDOW:, :]
            vp_ref[...] = v[TOKEN_BLOCK - WINDOW:, :]

        k0, k1 = _dup_heads(k)
        v0, v1 = _dup_heads(v)
        for buf, val in ((kk0, k0), (kk1, k1), (vv0, v0), (vv1, v1)):
            buf[WINDOW:, :] = val.astype(BF16)

        attn_chunks = []
        for j in range(TOKEN_BLOCK // CHUNK):
            band = slice(j * CHUNK, j * CHUNK + BAND)
            key_bias = None
            if j * CHUNK < WINDOW:
                key_pos = lax.broadcasted_iota(jnp.int32, (1, BAND), 1) + (s * TOKEN_BLOCK + j * CHUNK - WINDOW)
                key_bias = jnp.where(key_pos < 0, NEG_INF, 0.0)
            attn_chunks.append(_attend_chunk(q[j * CHUNK:(j + 1) * CHUNK], (kk0[band, :], kk1[band, :]),
                                             (vv0[band, :], vv1[band, :]), alibi_ref, sink_ref, key_bias))
        attn = jnp.concatenate(attn_chunks, axis=0)

        for buf in (kk0, kk1, vv0, vv1):
            buf[0:WINDOW, :] = buf[TOKEN_BLOCK:TOKEN_BLOCK + WINDOW, :]

        ws = _sgu_weights(sguw_ref)
        sgu = jnp.concatenate(
            [_sgu_chunk(ws, gu[c * SGU_CHUNK:(c + 1) * SGU_CHUNK], gvn[c * SGU_CHUNK:(c + 1) * SGU_CHUNK],
                        sgubt_ref) for c in range(TOKEN_BLOCK // SGU_CHUNK)], axis=0)
        _merge_route(x, attn, sgu, *weights, cnt_scr, *outs)

    @pl.when(n == PROMPT_BLOCKS)
    def _():
        x = xs_in_ref[...]
        q, k, v, gu, gvn = _in_proj(x, n1g_ref, w_in_ref, sgng_ref)
        gv_ref[...] = gvn
        ws = _sgu_weights(sguw_ref)
        attn_chunks, sgu_chunks = [], []
        pad = jnp.zeros((SGU_CHUNK - DEC_SEQ, SGU_WIDTH), F32)
        for b in range(DEC_BATCH):
            rows = slice(b * DEC_SEQ, (b + 1) * DEC_SEQ)
            k_all = jnp.concatenate([ck_ref[b], k[rows]], axis=0)
            v_all = jnp.concatenate([cv_ref[b], v[rows]], axis=0)
            ks_ref[b] = k_all[DEC_SEQ:]
            vs_ref[b] = v_all[DEC_SEQ:]
            k0, k1 = _dup_heads(k_all)
            v0, v1 = _dup_heads(v_all)
            attn_chunks.append(_attend_chunk(q[rows], (k0.astype(BF16), k1.astype(BF16)),
                                             (v0.astype(BF16), v1.astype(BF16)), alibi_ref, sink_ref, None))
            gated = _sgu_chunk(ws, jnp.concatenate([gu[rows], pad], axis=0),
                               jnp.concatenate([gvn[rows], pad], axis=0), sgubt_ref)
            sgu_chunks.append(gated[:DEC_SEQ])
        attn = jnp.concatenate(attn_chunks, axis=0)
        sgu = jnp.concatenate(sgu_chunks, axis=0)
        _merge_route(x, attn, sgu, *weights, cnt_scr, *outs)

    cnt_ref[...] = cnt_scr[...].astype(jnp.int32)


def _dispatch_kernel(meta_ref, pstart_ref, idx_ref, rank_ref, h2_ref, dest_ref, xs_ref,
                     zero_buf, dest_vmem, dest_smem, sem_rows, sem_idx, sem_fill):
    i = pl.program_id(0)

    def fill_copy(e):
        start = pl.multiple_of(meta_ref[e] // ROW_TILE * ROW_TILE, ROW_TILE)
        return pltpu.make_async_copy(zero_buf, xs_ref.at[pl.ds(start, FILL_ROWS), :], sem_fill)

    def tail_copy(j):
        start = pl.multiple_of((meta_ref[N_EXPERTS] + j) * EXPERT_ROWS, EXPERT_ROWS)
        return pltpu.make_async_copy(zero_buf.at[pl.ds(0, EXPERT_ROWS), :],
                                     xs_ref.at[pl.ds(start, EXPERT_ROWS), :], sem_fill)

    @pl.when(i == 0)
    def _():
        zero_buf[...] = jnp.zeros_like(zero_buf)
        for e in range(N_EXPERTS):
            fill_copy(e).start()
        for e in range(N_EXPERTS):
            fill_copy(e).wait()
        for j in range(SORTED_BLOCKS - MIN_ROW_BLOCKS):
            @pl.when(meta_ref[N_EXPERTS] + j < SORTED_BLOCKS)
            def _():
                tail_copy(j).start()
        for j in range(SORTED_BLOCKS - MIN_ROW_BLOCKS):
            @pl.when(meta_ref[N_EXPERTS] + j < SORTED_BLOCKS)
            def _():
                tail_copy(j).wait()

    e_iota = lax.broadcasted_iota(jnp.int32, (N_EXPERTS, TOKEN_BLOCK), 0)
    pstart = pstart_ref[...]
    idx = idx_ref[...]
    dest = jnp.concatenate(
        [jnp.sum(jnp.where(e_iota == idx[kk:kk + 1, :], pstart, 0), axis=0, keepdims=True)
         for kk in range(TOP_K)], axis=0) + rank_ref[...]
    dest_ref[...] = dest
    dest_vmem[...] = dest
    to_smem = pltpu.make_async_copy(dest_vmem, dest_smem, sem_idx)
    to_smem.start()
    to_smem.wait()

    def issue(t, carry):
        for kk in range(TOP_K):
            pltpu.make_async_copy(h2_ref.at[pl.ds(t, 1), :], xs_ref.at[pl.ds(dest_smem[kk, t], 1), :],
                                  sem_rows).start()
        return carry

    lax.fori_loop(0, TOKEN_BLOCK, issue, 0)
    for _ in range(TOP_K):
        pltpu.make_async_copy(h2_ref, xs_ref.at[pl.ds(0, TOKEN_BLOCK), :], sem_rows).wait()


def _expert_kernel(be_ref, nv_ref, xs_ref, wgu_ref, bgu_ref, wd_ref, bd_ref, os_ref, wgu_bf, wd_bf):
    i = pl.program_id(0)

    @pl.when(i < nv_ref[0])
    def _():
        prev = be_ref[jnp.maximum(i - 1, 0)]

        @pl.when((i == 0) | (be_ref[i] != prev))
        def _():
            rows = 128

            def cast_rows(c, carry):
                r = pl.multiple_of(c * rows, rows)
                wgu_bf[pl.ds(r, rows), :] = wgu_ref[pl.ds(r, rows), :].astype(BF16)
                wd_bf[pl.ds(r, rows), :] = wd_ref[pl.ds(r, rows), :].astype(BF16)
                return carry

            lax.fori_loop(0, D_MODEL // rows, cast_rows, 0)

        x = xs_ref[...].astype(BF16)
        gu = jnp.dot(x, wgu_bf[...], preferred_element_type=F32) + bgu_ref[...]
        x_glu = jnp.minimum(gu[:, :D_FF], SWIGLU_LIMIT)
        x_lin = jnp.clip(gu[:, D_FF:], -SWIGLU_LIMIT, SWIGLU_LIMIT)
        act = x_glu * (1.0 / (1.0 + jnp.exp(-SWIGLU_ALPHA * x_glu))) * (x_lin + 1.0)
        os_ref[...] = jnp.dot(act.astype(BF16), wd_bf[...], preferred_element_type=F32) + bd_ref[...]

    @pl.when(i >= nv_ref[0])
    def _():
        os_ref[...] = jnp.zeros_like(os_ref)


def _combine_kernel(dest_hbm, gate_ref, x1_ref, fg_ref, os_ref, yp_ref, ys_ref,
                    rows_buf, dest_smem, sem_rows, sem_idx):
    i = pl.program_id(0)
    slot = i % 2

    def fetch(step, into):
        idx_copy = pltpu.make_async_copy(dest_hbm.at[step], dest_smem.at[into], sem_idx)
        idx_copy.start()
        idx_copy.wait()

        def issue(t, carry):
            for kk in range(TOP_K):
                pltpu.make_async_copy(os_ref.at[pl.ds(dest_smem[into, kk, t], 1), :],
                                      rows_buf.at[into, kk, pl.ds(t, 1), :], sem_rows.at[into]).start()
            return carry

        lax.fori_loop(0, TOKEN_BLOCK, issue, 0)

    @pl.when(i == 0)
    def _():
        fetch(0, 0)

    @pl.when(i + 1 < N_TOKEN_BLOCKS)
    def _():
        fetch(i + 1, 1 - slot)

    for kk in range(TOP_K):
        pltpu.make_async_copy(os_ref.at[pl.ds(0, TOKEN_BLOCK), :], rows_buf.at[slot, kk], sem_rows.at[slot]).wait()

    gates = gate_ref[...]
    y = x1_ref[...]
    for kk in range(TOP_K):
        y = y + gates[:, kk:kk + 1] * rows_buf[slot, kk]
    out = _rms(y, fg_ref[...])

    @pl.when(i < PROMPT_BLOCKS)
    def _():
        yp_ref[...] = out

    @pl.when(i >= PROMPT_BLOCKS)
    def _():
        ys_ref[...] = out


def _full(shape):
    return pl.BlockSpec(shape, lambda *_: (0,) * len(shape))


def kernel(x_prompt, x_sample, cache_k, cache_v, norm1_g, w_in, sgu_norm_g, attn_sinks, sgu_w, sgu_b, attn_out_g,
           sgu_out_g, w_out, norm2_g, w_router, b_router, w_gate_up, b_gate_up, w_down, b_down, final_g):
    i32 = jnp.int32
    n1g = norm1_g[0].reshape(1, D_MODEL)
    w_in_b = w_in[0].astype(BF16)
    sgng = sgu_norm_g[0].reshape(1, SGU_WIDTH)
    sink_cols = jnp.repeat(attn_sinks[0].reshape(N_KV_HEADS, Q_PER_KV, 1), CHUNK, axis=1).reshape(
        N_KV_HEADS, Q_PER_KV * CHUNK, 1)
    slopes = jnp.exp2(-8.0 * jnp.arange(1, N_Q_HEADS + 1, dtype=F32) / N_Q_HEADS).reshape(N_KV_HEADS, Q_PER_KV)
    dist = jnp.abs(jnp.arange(CHUNK)[:, None] + WINDOW - jnp.arange(BAND)[None, :]).astype(F32)
    alibi = (-slopes[:, :, None, None] * dist).reshape(N_KV_HEADS, Q_PER_KV * CHUNK, BAND)
    sguw = sgu_w[0]
    sgubt = sgu_b[0].T
    aog = attn_out_g[0].reshape(1, ATTN_WIDTH)
    sog = sgu_out_g[0].reshape(1, SGU_WIDTH)
    w_out_b = w_out[0].astype(BF16)
    n2g = norm2_g[0].reshape(1, D_MODEL)
    wr_t = w_router[0].T
    br_c = b_router[0].reshape(N_EXPERTS, 1)
    fg = final_g.reshape(1, D_MODEL)

    weight_specs = [
        _full((1, D_MODEL)), _full((D_MODEL, IN_COLS)), _full((1, SGU_WIDTH)),
        _full((N_KV_HEADS, Q_PER_KV * CHUNK, 1)), _full((N_KV_HEADS, Q_PER_KV * CHUNK, BAND)),
        _full((SGU_HEADS, SGU_CHUNK, SGU_CHUNK)), _full((SGU_CHUNK, SGU_HEADS)),
        _full((1, ATTN_WIDTH)), _full((1, SGU_WIDTH)), _full((D_MODEL, D_MODEL)), _full((1, D_MODEL)),
        _full((N_EXPERTS, D_MODEL)), _full((N_EXPERTS, 1)),
    ]
    weights = (n1g, w_in_b, sgng, sink_cols, alibi, sguw, sgubt, aog, sog, w_out_b, n2g, wr_t, br_c)
    cparams = functools.partial(pltpu.CompilerParams, vmem_limit_bytes=VMEM_LIMIT_BYTES)

    def tok_map(n):
        return (n, 0)

    def tok_map_t(n):
        return (0, n)

    def stream_map(n):
        return (jnp.minimum(n // SEQ_BLOCKS, BATCH - 1), 0, 0)

    kv_cache = (DEC_BATCH, WINDOW, KV_WIDTH)
    bf_band = pltpu.VMEM((WINDOW + TOKEN_BLOCK, 2 * HEAD_DIM), BF16)
    (x1_all, h2_all, idx_all, rank_all, gate_all, cnt_all, kp_new, vp_new, ks_new, vs_new, gv_s) = pl.pallas_call(
        _mixer_kernel,
        grid=(N_TOKEN_BLOCKS,),
        in_specs=[pl.BlockSpec((TOKEN_BLOCK, D_MODEL), lambda n: (jnp.minimum(n, PROMPT_BLOCKS - 1), 0)),
                  _full((T_SAMPLE, D_MODEL)), _full(kv_cache), _full(kv_cache)] + weight_specs,
        out_specs=[
            pl.BlockSpec((TOKEN_BLOCK, D_MODEL), tok_map), pl.BlockSpec((TOKEN_BLOCK, D_MODEL), tok_map),
            pl.BlockSpec((TOP_K, TOKEN_BLOCK), tok_map_t), pl.BlockSpec((TOP_K, TOKEN_BLOCK), tok_map_t),
            pl.BlockSpec((TOP_K, TOKEN_BLOCK), tok_map_t),
            _full((N_EXPERTS, 1)),
            pl.BlockSpec((None, WINDOW, KV_WIDTH), stream_map), pl.BlockSpec((None, WINDOW, KV_WIDTH), stream_map),
            _full(kv_cache), _full(kv_cache), _full((T_SAMPLE, SGU_WIDTH)),
        ],
        out_shape=(
            jax.ShapeDtypeStruct((T_ALL, D_MODEL), F32),
            jax.ShapeDtypeStruct((T_ALL, D_MODEL), F32),
            jax.ShapeDtypeStruct((TOP_K, T_ALL), i32),
            jax.ShapeDtypeStruct((TOP_K, T_ALL), i32),
            jax.ShapeDtypeStruct((TOP_K, T_ALL), F32),
            jax.ShapeDtypeStruct((N_EXPERTS, 1), i32),
            jax.ShapeDtypeStruct((BATCH, WINDOW, KV_WIDTH), F32),
            jax.ShapeDtypeStruct((BATCH, WINDOW, KV_WIDTH), F32),
            jax.ShapeDtypeStruct(kv_cache, F32),
            jax.ShapeDtypeStruct(kv_cache, F32),
            jax.ShapeDtypeStruct((T_SAMPLE, SGU_WIDTH), F32),
        ),
        scratch_shapes=[bf_band, bf_band, bf_band, bf_band, pltpu.VMEM((N_EXPERTS, 1), F32)],
        compiler_params=cparams(dimension_semantics=("arbitrary",)),
        name="mixer",
    )(x_prompt.reshape(T_PROMPT, D_MODEL), x_sample.reshape(T_SAMPLE, D_MODEL),
      cache_k[0].reshape(kv_cache), cache_v[0].reshape(kv_cache), *weights)

    counts = cnt_all[:, 0]
    blocks_per = (counts + EXPERT_ROWS - 1) // EXPERT_ROWS
    block_end = jnp.cumsum(blocks_per)
    pstart = (block_end - blocks_per) * EXPERT_ROWS
    n_valid = block_end[-1:].astype(i32)
    step = jnp.minimum(jnp.arange(N_ROW_BLOCKS, dtype=i32), n_valid - 1)
    blk_e = jnp.minimum(jnp.searchsorted(block_end, step, side="right"), N_EXPERTS - 1).astype(i32)
    dispatch_meta = jnp.concatenate([(pstart + counts).astype(i32), n_valid])

    dest_blocks, xs = pl.pallas_call(
        _dispatch_kernel,
        grid_spec=pltpu.PrefetchScalarGridSpec(
            num_scalar_prefetch=1,
            grid=(N_TOKEN_BLOCKS,),
            in_specs=[
                pl.BlockSpec((N_EXPERTS, 1), lambda i, m: (0, 0)),
                pl.BlockSpec((TOP_K, TOKEN_BLOCK), lambda i, m: (0, i)),
                pl.BlockSpec((TOP_K, TOKEN_BLOCK), lambda i, m: (0, i)),
                pl.BlockSpec((TOKEN_BLOCK, D_MODEL), lambda i, m: (i, 0)),
            ],
            out_specs=[
                pl.BlockSpec((None, TOP_K, TOKEN_BLOCK), lambda i, m: (i, 0, 0)),
                pl.BlockSpec(memory_space=pl.ANY),
            ],
            scratch_shapes=[
                pltpu.VMEM((FILL_ROWS, D_MODEL), F32),
                pltpu.VMEM((TOP_K, TOKEN_BLOCK), i32),
                pltpu.SMEM((TOP_K, TOKEN_BLOCK), i32),
                pltpu.SemaphoreType.DMA, pltpu.SemaphoreType.DMA, pltpu.SemaphoreType.DMA,
            ],
        ),
        out_shape=(jax.ShapeDtypeStruct((N_TOKEN_BLOCKS, TOP_K, TOKEN_BLOCK), i32),
                   jax.ShapeDtypeStruct((SORTED_ROWS, D_MODEL), F32)),
        compiler_params=cparams(dimension_semantics=("arbitrary",)),
        name="dispatch",
    )(dispatch_meta, pstart.reshape(N_EXPERTS, 1).astype(i32), idx_all, rank_all, h2_all)

    def row_map(i, be, nv):
        return (i, 0)

    def expert_map(i, be, nv):
        return (be[i], 0, 0)

    os_rows = pl.pallas_call(
        _expert_kernel,
        grid_spec=pltpu.PrefetchScalarGridSpec(
            num_scalar_prefetch=2,
            grid=(N_ROW_BLOCKS,),
            in_specs=[
                pl.BlockSpec((EXPERT_ROWS, D_MODEL), row_map),
                pl.BlockSpec((None, D_MODEL, 2 * D_FF), expert_map),
                pl.BlockSpec((None, 1, 2 * D_FF), expert_map),
                pl.BlockSpec((None, D_FF, D_MODEL), expert_map),
                pl.BlockSpec((None, 1, D_MODEL), expert_map),
            ],
            out_specs=pl.BlockSpec((EXPERT_ROWS, D_MODEL), row_map),
            scratch_shapes=[pltpu.VMEM((D_MODEL, 2 * D_FF), BF16), pltpu.VMEM((D_FF, D_MODEL), BF16)],
        ),
        out_shape=jax.ShapeDtypeStruct((N_ROW_BLOCKS * EXPERT_ROWS, D_MODEL), F32),
        compiler_params=cparams(dimension_semantics=("arbitrary",)),
        name="experts",
    )(blk_e, n_valid, xs, w_gate_up[0], b_gate_up[0].reshape(N_EXPERTS, 1, 2 * D_FF), w_down[0],
      b_down[0].reshape(N_EXPERTS, 1, D_MODEL))

    any_spec = pl.BlockSpec(memory_space=pl.ANY)
    y_p, y_s = pl.pallas_call(
        _combine_kernel,
        grid=(N_TOKEN_BLOCKS,),
        in_specs=[
            any_spec,
            pl.BlockSpec((TOKEN_BLOCK, TOP_K), lambda i: (i, 0)),
            pl.BlockSpec((TOKEN_BLOCK, D_MODEL), lambda i: (i, 0)),
            _full((1, D_MODEL)),
            any_spec,
        ],
        out_specs=[
            pl.BlockSpec((TOKEN_BLOCK, D_MODEL), lambda i: (jnp.minimum(i, PROMPT_BLOCKS - 1), 0)),
            pl.BlockSpec((TOKEN_BLOCK, D_MODEL), lambda i: (jnp.maximum(i - PROMPT_BLOCKS, 0), 0)),
        ],
        out_shape=(jax.ShapeDtypeStruct((T_PROMPT, D_MODEL), F32), jax.ShapeDtypeStruct((T_SAMPLE, D_MODEL), F32)),
        scratch_shapes=[
            pltpu.VMEM((2, TOP_K, TOKEN_BLOCK, D_MODEL), F32),
            pltpu.SMEM((2, TOP_K, TOKEN_BLOCK), i32),
            pltpu.SemaphoreType.DMA((2,)), pltpu.SemaphoreType.DMA,
        ],
        compiler_params=cparams(dimension_semantics=("arbitrary",)),
        name="combine",
    )(dest_blocks, gate_all.T, x1_all, fg, os_rows)

    kv5 = (1, -1, WINDOW, N_KV_HEADS, HEAD_DIM)
    return (y_p.reshape(BATCH, SEQ, D_MODEL), y_s.reshape(DEC_BATCH, DEC_SEQ, D_MODEL),
            kp_new.reshape(kv5), vp_new.reshape(kv5), ks_new.reshape(kv5), vs_new.reshape(kv5),
            gv_s.reshape(1, DEC_BATCH, DEC_SEQ, SGU_HEADS, SGU_HEAD_DIM))
```

```python
import functools

import jax
import jax.numpy as jnp
from jax import lax
from jax.experimental import pallas as pl
from jax.experimental.pallas import tpu as pltpu

D_MODEL = 1024
BATCH = 2
SEQ = 8192
DEC_BATCH = 8
DEC_SEQ = 64
CHUNK = 64
WINDOW = 128
BAND = WINDOW + CHUNK
HEAD_DIM = 64
N_Q_HEADS = 8
N_KV_HEADS = 2
Q_PER_KV = N_Q_HEADS // N_KV_HEADS
ATTN_WIDTH = N_Q_HEADS * HEAD_DIM
KV_WIDTH = N_KV_HEADS * HEAD_DIM
QKV_COLS = ATTN_WIDTH + 2 * KV_WIDTH
SGU_CHUNK = 128
SGU_HEADS = 4
SGU_HEAD_DIM = 128
SGU_WIDTH = SGU_HEADS * SGU_HEAD_DIM
IN_COLS = QKV_COLS + 2 * SGU_WIDTH
N_EXPERTS = 32
TOP_K = 4
D_FF = 1024
SWIGLU_LIMIT = 7.0
SWIGLU_ALPHA = 1.702
RMS_EPS = 1e-5
NEG_INF = -1e30

T_PROMPT = BATCH * SEQ
T_SAMPLE = DEC_BATCH * DEC_SEQ
T_ALL = T_PROMPT + T_SAMPLE
TOKEN_BLOCK = 512
SEQ_BLOCKS = SEQ // TOKEN_BLOCK
N_TOKEN_BLOCKS = T_ALL // TOKEN_BLOCK
PROMPT_BLOCKS = T_PROMPT // TOKEN_BLOCK
EXPERT_ROWS = 256
N_SLOTS = T_ALL * TOP_K
MIN_ROW_BLOCKS = N_SLOTS // EXPERT_ROWS
N_ROW_BLOCKS = MIN_ROW_BLOCKS + N_EXPERTS
ROW_TILE = 8
FILL_ROWS = EXPERT_ROWS + ROW_TILE
SORTED_BLOCKS = N_ROW_BLOCKS + 2
SORTED_ROWS = SORTED_BLOCKS * EXPERT_ROWS
VMEM_LIMIT_BYTES = 56 * 1024 * 1024
ISSUE_UNROLL = 8
TAG_LANES = 128
ROW_WIDTH = D_MODEL + TAG_LANES
DUMP_ROW = TOP_K * T_ALL
OUT_ROWS = DUMP_ROW + EXPERT_ROWS

assert T_SAMPLE == TOKEN_BLOCK and SEQ % TOKEN_BLOCK == 0 and N_SLOTS % EXPERT_ROWS == 0

F32 = jnp.float32
BF16 = jnp.bfloat16


def _rms(x, g):
    return x * lax.rsqrt(jnp.mean(x * x, axis=-1, keepdims=True) + RMS_EPS) * g


def _gelu(x):
    return 0.5 * x * (1.0 + lax.erf(x * 0.7071067811865476))


def _in_proj(x, n1g_ref, w_in_ref, sgng_ref):
    h = _rms(x, n1g_ref[...])
    cols = jnp.dot(h.astype(BF16), w_in_ref[...], preferred_element_type=F32)
    q = cols[:, :ATTN_WIDTH] * (HEAD_DIM ** -0.5)
    k = cols[:, ATTN_WIDTH:ATTN_WIDTH + KV_WIDTH]
    v = cols[:, ATTN_WIDTH + KV_WIDTH:QKV_COLS]
    gu = _gelu(cols[:, QKV_COLS:QKV_COLS + SGU_WIDTH])
    gv = _gelu(cols[:, QKV_COLS + SGU_WIDTH:])
    sg = sgng_ref[...]
    gvn = jnp.concatenate(
        [_rms(gv[:, i * SGU_HEAD_DIM:(i + 1) * SGU_HEAD_DIM], sg[:, i * SGU_HEAD_DIM:(i + 1) * SGU_HEAD_DIM])
         for i in range(SGU_HEADS)], axis=1)
    return q, k, v, gu, gvn


def _dup_heads(t):
    lane = lax.broadcasted_iota(jnp.int32, t.shape, 1)
    r = pltpu.roll(t, HEAD_DIM, axis=1)
    return jnp.where(lane < HEAD_DIM, t, r), jnp.where(lane < HEAD_DIM, r, t)


def _attend_chunk(q64, kk, vv, alibi_ref, sink_ref, key_bias):
    lane = lax.broadcasted_iota(jnp.int32, (CHUNK, 2 * HEAD_DIM), 1)
    lo = lane < HEAD_DIM
    pairs = []
    for g in range(N_KV_HEADS):
        rows = []
        for p in range(2):
            qp = q64[:, (2 * g + p) * 128:(2 * g + p + 1) * 128]
            rows.append(jnp.where(lo, qp, 0.0))
            rows.append(jnp.where(lo, 0.0, qp))
        q4 = jnp.concatenate(rows, axis=0).astype(BF16)
        sc = lax.dot_general(q4, kk[g], (((1,), (1,)), ((), ())), preferred_element_type=F32)
        sc = sc + alibi_ref[g]
        if key_bias is not None:
            sc = sc + key_bias
        sink = sink_ref[g]
        m = jnp.maximum(jnp.max(sc, axis=1, keepdims=True), sink)
        p_un = jnp.exp(sc - m)
        den = jnp.sum(p_un, axis=1, keepdims=True) + jnp.exp(sink - m)
        o = jnp.dot(p_un.astype(BF16), vv[g], preferred_element_type=F32) / den
        for p in range(2):
            pairs.append(jnp.where(lo, o[(2 * p) * CHUNK:(2 * p + 1) * CHUNK],
                                   o[(2 * p + 1) * CHUNK:(2 * p + 2) * CHUNK]))
    return jnp.concatenate(pairs, axis=1)


def _sgu_weights(sguw_ref):
    sub_i = lax.broadcasted_iota(jnp.int32, (SGU_CHUNK, SGU_CHUNK), 0) // CHUNK
    sub_j = lax.broadcasted_iota(jnp.int32, (SGU_CHUNK, SGU_CHUNK), 1) // CHUNK
    keep = sub_j <= sub_i
    return [jnp.where(keep, sguw_ref[i], 0.0).astype(BF16) for i in range(SGU_HEADS)]


def _sgu_chunk(ws, gu_c, gvn_c, sgubt_ref):
    outs = []
    for i in range(SGU_HEADS):
        sl = slice(i * SGU_HEAD_DIM, (i + 1) * SGU_HEAD_DIM)
        sp = jnp.dot(ws[i], gvn_c[:, sl].astype(BF16), preferred_element_type=F32) + sgubt_ref[:, i:i + 1]
        outs.append(gu_c[:, sl] * sp)
    return jnp.concatenate(outs, axis=1)


def _merge_route(x, attn, sgu, tok_base, aog_ref, sog_ref, w_out_ref, n2g_ref, wr_ref, br_ref, cnt_scr,
                 x1_ref, h2_ref, idx_ref, rank_ref, gate_ref):
    n = x.shape[0]
    a_n = _rms(attn, aog_ref[...]).astype(BF16)
    s_n = _rms(sgu, sog_ref[...]).astype(BF16)
    x1 = (x + jnp.dot(a_n, w_out_ref[:ATTN_WIDTH, :], preferred_element_type=F32)
          + jnp.dot(s_n, w_out_ref[ATTN_WIDTH:, :], preferred_element_type=F32))
    x1_ref[...] = x1
    h2 = _rms(x1, n2g_ref[...])
    h2_ref[:, :D_MODEL] = h2
    logits = lax.dot_general(wr_ref[...], h2, (((1,), (1,)), ((), ())),
                             precision=lax.Precision.HIGHEST, preferred_element_type=F32) + br_ref[...]
    e_iota = lax.broadcasted_iota(jnp.int32, (N_EXPERTS, n), 0).astype(F32)
    cur = logits
    vals, idxs, sels = [], [], []
    for _ in range(TOP_K):
        m = jnp.max(cur, axis=0, keepdims=True)
        idx = jnp.min(jnp.where(cur == m, e_iota, float(N_EXPERTS)), axis=0, keepdims=True)
        sel = e_iota == idx
        vals.append(m)
        idxs.append(idx)
        sels.append(sel)
        cur = jnp.where(sel, -jnp.inf, cur)
    exps = [jnp.exp(v - vals[0]) for v in vals]
    den = exps[0] + exps[1] + exps[2] + exps[3]
    gate_ref[...] = jnp.concatenate([e / den for e in exps], axis=0)
    onehot = jnp.where(sels[0] | sels[1] | sels[2] | sels[3], 1.0, 0.0)
    before = lax.broadcasted_iota(jnp.int32, (n, n), 0) < lax.broadcasted_iota(jnp.int32, (n, n), 1)
    upper = jnp.where(before, 1.0, 0.0).astype(BF16)
    prior = jnp.dot(onehot.astype(BF16), upper, preferred_element_type=F32) + cnt_scr[...]
    ranks = [jnp.sum(jnp.where(s, prior, 0.0), axis=0, keepdims=True) for s in sels]
    idx_ref[...] = jnp.concatenate(idxs, axis=0).astype(jnp.int32)
    rank_ref[...] = jnp.concatenate(ranks, axis=0).astype(jnp.int32)
    tok = (lax.broadcasted_iota(jnp.int32, (1, n), 1) + tok_base).astype(F32)
    tag_t = jnp.concatenate([tok] + idxs + [jnp.zeros((TAG_LANES - 1 - TOP_K, n), F32)], axis=0)
    h2_ref[:, D_MODEL:] = tag_t.T
    cnt_scr[...] = cnt_scr[...] + jnp.sum(onehot, axis=1, keepdims=True)


def _mixer_kernel(xp_ref, xs_in_ref, ck_ref, cv_ref, n1g_ref, w_in_ref, sgng_ref, sink_ref, alibi_ref, sguw_ref,
                  sgubt_ref, aog_ref, sog_ref, w_out_ref, n2g_ref, wr_ref, br_ref,
                  x1_ref, h2_ref, idx_ref, rank_ref, gate_ref, cnt_ref, kp_ref, vp_ref, ks_ref, vs_ref, gv_ref,
                  kk0, kk1, vv0, vv1, cnt_scr):
    n = pl.program_id(0)
    s = n % SEQ_BLOCKS
    weights = (aog_ref, sog_ref, w_out_ref, n2g_ref, wr_ref, br_ref)
    outs = (x1_ref, h2_ref, idx_ref, rank_ref, gate_ref)

    @pl.when(n == 0)
    def _():
        cnt_scr[...] = jnp.zeros_like(cnt_scr)

    @pl.when(n < PROMPT_BLOCKS)
    def _():
        @pl.when(s == 0)
        def _():
            zeros = jnp.zeros((WINDOW, 2 * HEAD_DIM), BF16)
            for buf in (kk0, kk1, vv0, vv1):
                buf[0:WINDOW, :] = zeros

        x = xp_ref[...]
        q, k, v, gu, gvn = _in_proj(x, n1g_ref, w_in_ref, sgng_ref)

        @pl.when(s == SEQ_BLOCKS - 1)
        def _():
            kp_ref[...] = k[TOKEN_BLOCK - WINDOW:, :]
            vp_ref[...] = v[TOKEN_BLOCK - WINDOW:, :]

        k0, k1 = _dup_heads(k)
        v0, v1 = _dup_heads(v)
        for buf, val in ((kk0, k0), (kk1, k1), (vv0, v0), (vv1, v1)):
            buf[WINDOW:, :] = val.astype(BF16)

        attn_chunks = []
        for j in range(TOKEN_BLOCK // CHUNK):
            band = slice(j * CHUNK, j * CHUNK + BAND)
            key_bias = None
            if j * CHUNK < WINDOW:
                key_pos = lax.broadcasted_iota(jnp.int32, (1, BAND), 1) + (s * TOKEN_BLOCK + j * CHUNK - WINDOW)
                key_bias = jnp.where(key_pos < 0, NEG_INF, 0.0)
            attn_chunks.append(_attend_chunk(q[j * CHUNK:(j + 1) * CHUNK], (kk0[band, :], kk1[band, :]),
                                             (vv0[band, :], vv1[band, :]), alibi_ref, sink_ref, key_bias))
        attn = jnp.concatenate(attn_chunks, axis=0)

        for buf in (kk0, kk1, vv0, vv1):
            buf[0:WINDOW, :] = buf[TOKEN_BLOCK:TOKEN_BLOCK + WINDOW, :]

        ws = _sgu_weights(sguw_ref)
        sgu = jnp.concatenate(
            [_sgu_chunk(ws, gu[c * SGU_CHUNK:(c + 1) * SGU_CHUNK], gvn[c * SGU_CHUNK:(c + 1) * SGU_CHUNK],
                        sgubt_ref) for c in range(TOKEN_BLOCK // SGU_CHUNK)], axis=0)
        _merge_route(x, attn, sgu, n * TOKEN_BLOCK, *weights, cnt_scr, *outs)

    @pl.when(n == PROMPT_BLOCKS)
    def _():
        x = xs_in_ref[...]
        q, k, v, gu, gvn = _in_proj(x, n1g_ref, w_in_ref, sgng_ref)
        gv_ref[...] = gvn
        ws = _sgu_weights(sguw_ref)
        attn_chunks, sgu_chunks = [], []
        pad = jnp.zeros((SGU_CHUNK - DEC_SEQ, SGU_WIDTH), F32)
        for b in range(DEC_BATCH):
            rows = slice(b * DEC_SEQ, (b + 1) * DEC_SEQ)
            k_all = jnp.concatenate([ck_ref[b], k[rows]], axis=0)
            v_all = jnp.concatenate([cv_ref[b], v[rows]], axis=0)
            ks_ref[b] = k_all[DEC_SEQ:]
            vs_ref[b] = v_all[DEC_SEQ:]
            k0, k1 = _dup_heads(k_all)
            v0, v1 = _dup_heads(v_all)
            attn_chunks.append(_attend_chunk(q[rows], (k0.astype(BF16), k1.astype(BF16)),
                                             (v0.astype(BF16), v1.astype(BF16)), alibi_ref, sink_ref, None))
            gated = _sgu_chunk(ws, jnp.concatenate([gu[rows], pad], axis=0),
                               jnp.concatenate([gvn[rows], pad], axis=0), sgubt_ref)
            sgu_chunks.append(gated[:DEC_SEQ])
        attn = jnp.concatenate(attn_chunks, axis=0)
        sgu = jnp.concatenate(sgu_chunks, axis=0)
        _merge_route(x, attn, sgu, n * TOKEN_BLOCK, *weights, cnt_scr, *outs)

    cnt_ref[...] = cnt_scr[...].astype(jnp.int32)


def _dispatch_kernel(meta_ref, pstart_ref, idx_ref, rank_ref, h2_ref, xs_ref,
                     zero_buf, dest_vmem, dest_smem, sem_rows, sem_idx, sem_fill):
    i = pl.program_id(0)

    def fill_copy(e):
        start = pl.multiple_of(meta_ref[e] // ROW_TILE * ROW_TILE, ROW_TILE)
        return pltpu.make_async_copy(zero_buf, xs_ref.at[pl.ds(start, FILL_ROWS), :], sem_fill)

    def tail_copy(j):
        start = pl.multiple_of((meta_ref[N_EXPERTS] + j) * EXPERT_ROWS, EXPERT_ROWS)
        return pltpu.make_async_copy(zero_buf.at[pl.ds(0, EXPERT_ROWS), :],
                                     xs_ref.at[pl.ds(start, EXPERT_ROWS), :], sem_fill)

    @pl.when(i == 0)
    def _():
        zero_buf[...] = jnp.zeros_like(zero_buf)
        for e in range(N_EXPERTS):
            fill_copy(e).start()
        for e in range(N_EXPERTS):
            fill_copy(e).wait()
        for j in range(SORTED_BLOCKS - MIN_ROW_BLOCKS):
            @pl.when(meta_ref[N_EXPERTS] + j < SORTED_BLOCKS)
            def _():
                tail_copy(j).start()
        for j in range(SORTED_BLOCKS - MIN_ROW_BLOCKS):
            @pl.when(meta_ref[N_EXPERTS] + j < SORTED_BLOCKS)
            def _():
                tail_copy(j).wait()

    e_iota = lax.broadcasted_iota(jnp.int32, (N_EXPERTS, TOKEN_BLOCK), 0)
    pstart = pstart_ref[...]
    idx = idx_ref[...]
    dest = jnp.concatenate(
        [jnp.sum(jnp.where(e_iota == idx[kk:kk + 1, :], pstart, 0), axis=0, keepdims=True)
         for kk in range(TOP_K)], axis=0) + rank_ref[...]
    dest_vmem[...] = dest
    to_smem = pltpu.make_async_copy(dest_vmem, dest_smem, sem_idx)
    to_smem.start()
    to_smem.wait()

    def issue(t, carry):
        for kk in range(TOP_K):
            pltpu.make_async_copy(h2_ref.at[pl.ds(t, 1), :], xs_ref.at[pl.ds(dest_smem[kk, t], 1), :],
                                  sem_rows).start(priority=kk % 2)
        return carry

    lax.fori_loop(0, TOKEN_BLOCK, issue, 0, unroll=ISSUE_UNROLL)
    for _ in range(TOP_K):
        pltpu.make_async_copy(h2_ref, xs_ref.at[pl.ds(0, TOKEN_BLOCK), :], sem_rows).wait()


def _expert_kernel(be_ref, nrow_ref, nv_ref, xs_ref, wgu_ref, bgu_ref, wd_ref, bd_ref, out_ref,
                   wgu_bf, wd_bf, stage, ids_vmem, ids_smem, sem_rows, sem_ids):
    i = pl.program_id(0)
    nv = nv_ref[0]
    slot = i % 2

    def ids_copy(s):
        return pltpu.make_async_copy(ids_vmem.at[s], ids_smem.at[s], sem_ids.at[s])

    def rows_wait(s):
        pltpu.make_async_copy(stage.at[pl.ds(0, EXPERT_ROWS), :], out_ref.at[pl.ds(0, EXPERT_ROWS), :],
                              sem_rows.at[s]).wait()

    def issue_rows(s):
        ids_copy(s).wait()
        base = s * EXPERT_ROWS
        for r in range(EXPERT_ROWS):
            pltpu.make_async_copy(stage.at[pl.ds(base + r, 1), :], out_ref.at[pl.ds(ids_smem[s, 0, r], 1), :],
                                  sem_rows.at[s]).start(priority=r % 2)

    def compute(s):
        expert = be_ref[i]
        x = xs_ref[:, :D_MODEL].astype(BF16)
        gu = jnp.dot(x, wgu_bf[...], preferred_element_type=F32) + bgu_ref[...]
        x_glu = jnp.minimum(gu[:, :D_FF], SWIGLU_LIMIT)
        x_lin = jnp.clip(gu[:, D_FF:], -SWIGLU_LIMIT, SWIGLU_LIMIT)
        act = x_glu * (1.0 / (1.0 + jnp.exp(-SWIGLU_ALPHA * x_glu))) * (x_lin + 1.0)
        out = jnp.dot(act.astype(BF16), wd_bf[...], preferred_element_type=F32) + bd_ref[...]
        stage[pl.ds(pl.multiple_of(s * EXPERT_ROWS, EXPERT_ROWS), EXPERT_ROWS), :] = out
        tag = xs_ref[:, D_MODEL:]
        expert_f = expert.astype(F32)
        k_slot = sum(float(j) * jnp.where(tag[:, 1 + j:2 + j] == expert_f, 1.0, 0.0) for j in range(1, TOP_K))
        row = lax.broadcasted_iota(jnp.int32, (EXPERT_ROWS, 1), 0)
        dst = jnp.where(row < nrow_ref[i], k_slot * float(T_ALL) + tag[:, 0:1], (DUMP_ROW + row).astype(F32))
        dst_t = jnp.broadcast_to(dst, (EXPERT_ROWS, TAG_LANES)).T
        ids_vmem[s] = dst_t[0:ROW_TILE].astype(jnp.int32)
        ids_copy(s).start()

    @pl.when(i == 0)
    def _():
        stage[EXPERT_ROWS:, :] = jnp.zeros((EXPERT_ROWS, D_MODEL), F32)
        dump = pltpu.make_async_copy(stage.at[pl.ds(EXPERT_ROWS, EXPERT_ROWS), :],
                                     out_ref.at[pl.ds(DUMP_ROW, EXPERT_ROWS), :], sem_rows.at[1])
        dump.start()
        dump.wait()

    @pl.when((i >= 2) & (i <= nv))
    def _():
        rows_wait(slot)

    @pl.when((i < nv) & ((i == 0) | (be_ref[i] != be_ref[jnp.maximum(i - 1, 0)])))
    def _():
        rows = 128

        def cast_rows(c, carry):
            r = pl.multiple_of(c * rows, rows)
            wgu_bf[pl.ds(r, rows), :] = wgu_ref[pl.ds(r, rows), :].astype(BF16)
            wd_bf[pl.ds(r, rows), :] = wd_ref[pl.ds(r, rows), :].astype(BF16)
            return carry

        lax.fori_loop(0, D_MODEL // rows, cast_rows, 0)

    @pl.when(i == 0)
    def _():
        compute(0)

    @pl.when((i >= 1) & (i < nv))
    def _():
        issue_rows(1 - slot)
        compute(slot)

    @pl.when(i == nv)
    def _():
        issue_rows(1 - slot)
        rows_wait(1 - slot)


def _combine_kernel(o0_ref, o1_ref, o2_ref, o3_ref, gate_ref, x1_ref, fg_ref, yp_ref, ys_ref):
    i = pl.program_id(0)
    gates = gate_ref[...]
    y = x1_ref[...]
    for kk, o_ref in enumerate((o0_ref, o1_ref, o2_ref, o3_ref)):
        y = y + gates[:, kk:kk + 1] * o_ref[...]
    out = _rms(y, fg_ref[...])

    @pl.when(i < PROMPT_BLOCKS)
    def _():
        yp_ref[...] = out

    @pl.when(i >= PROMPT_BLOCKS)
    def _():
        ys_ref[...] = out


def _full(shape):
    return pl.BlockSpec(shape, lambda *_: (0,) * len(shape))


def kernel(x_prompt, x_sample, cache_k, cache_v, norm1_g, w_in, sgu_norm_g, attn_sinks, sgu_w, sgu_b, attn_out_g,
           sgu_out_g, w_out, norm2_g, w_router, b_router, w_gate_up, b_gate_up, w_down, b_down, final_g):
    i32 = jnp.int32
    n1g = norm1_g[0].reshape(1, D_MODEL)
    w_in_b = w_in[0].astype(BF16)
    sgng = sgu_norm_g[0].reshape(1, SGU_WIDTH)
    sink_cols = jnp.repeat(attn_sinks[0].reshape(N_KV_HEADS, Q_PER_KV, 1), CHUNK, axis=1).reshape(
        N_KV_HEADS, Q_PER_KV * CHUNK, 1)
    slopes = jnp.exp2(-8.0 * jnp.arange(1, N_Q_HEADS + 1, dtype=F32) / N_Q_HEADS).reshape(N_KV_HEADS, Q_PER_KV)
    dist = jnp.abs(jnp.arange(CHUNK)[:, None] + WINDOW - jnp.arange(BAND)[None, :]).astype(F32)
    alibi = (-slopes[:, :, None, None] * dist).reshape(N_KV_HEADS, Q_PER_KV * CHUNK, BAND)
    sguw = sgu_w[0]
    sgubt = sgu_b[0].T
    aog = attn_out_g[0].reshape(1, ATTN_WIDTH)
    sog = sgu_out_g[0].reshape(1, SGU_WIDTH)
    w_out_b = w_out[0].astype(BF16)
    n2g = norm2_g[0].reshape(1, D_MODEL)
    wr_t = w_router[0].T
    br_c = b_router[0].reshape(N_EXPERTS, 1)
    fg = final_g.reshape(1, D_MODEL)

    weight_specs = [
        _full((1, D_MODEL)), _full((D_MODEL, IN_COLS)), _full((1, SGU_WIDTH)),
        _full((N_KV_HEADS, Q_PER_KV * CHUNK, 1)), _full((N_KV_HEADS, Q_PER_KV * CHUNK, BAND)),
        _full((SGU_HEADS, SGU_CHUNK, SGU_CHUNK)), _full((SGU_CHUNK, SGU_HEADS)),
        _full((1, ATTN_WIDTH)), _full((1, SGU_WIDTH)), _full((D_MODEL, D_MODEL)), _full((1, D_MODEL)),
        _full((N_EXPERTS, D_MODEL)), _full((N_EXPERTS, 1)),
    ]
    weights = (n1g, w_in_b, sgng, sink_cols, alibi, sguw, sgubt, aog, sog, w_out_b, n2g, wr_t, br_c)
    cparams = functools.partial(pltpu.CompilerParams, vmem_limit_bytes=VMEM_LIMIT_BYTES)

    def tok_map(n):
        return (n, 0)

    def tok_map_t(n):
        return (0, n)

    def stream_map(n):
        return (jnp.minimum(n // SEQ_BLOCKS, BATCH - 1), 0, 0)

    kv_cache = (DEC_BATCH, WINDOW, KV_WIDTH)
    bf_band = pltpu.VMEM((WINDOW + TOKEN_BLOCK, 2 * HEAD_DIM), BF16)
    (x1_all, h2_all, idx_all, rank_all, gate_all, cnt_all, kp_new, vp_new, ks_new, vs_new, gv_s) = pl.pallas_call(
        _mixer_kernel,
        grid=(N_TOKEN_BLOCKS,),
        in_specs=[pl.BlockSpec((TOKEN_BLOCK, D_MODEL), lambda n: (jnp.minimum(n, PROMPT_BLOCKS - 1), 0)),
                  _full((T_SAMPLE, D_MODEL)), _full(kv_cache), _full(kv_cache)] + weight_specs,
        out_specs=[
            pl.BlockSpec((TOKEN_BLOCK, D_MODEL), tok_map), pl.BlockSpec((TOKEN_BLOCK, ROW_WIDTH), tok_map),
            pl.BlockSpec((TOP_K, TOKEN_BLOCK), tok_map_t), pl.BlockSpec((TOP_K, TOKEN_BLOCK), tok_map_t),
            pl.BlockSpec((TOP_K, TOKEN_BLOCK), tok_map_t),
            _full((N_EXPERTS, 1)),
            pl.BlockSpec((None, WINDOW, KV_WIDTH), stream_map), pl.BlockSpec((None, WINDOW, KV_WIDTH), stream_map),
            _full(kv_cache), _full(kv_cache), _full((T_SAMPLE, SGU_WIDTH)),
        ],
        out_shape=(
            jax.ShapeDtypeStruct((T_ALL, D_MODEL), F32),
            jax.ShapeDtypeStruct((T_ALL, ROW_WIDTH), F32),
            jax.ShapeDtypeStruct((TOP_K, T_ALL), i32),
            jax.ShapeDtypeStruct((TOP_K, T_ALL), i32),
            jax.ShapeDtypeStruct((TOP_K, T_ALL), F32),
            jax.ShapeDtypeStruct((N_EXPERTS, 1), i32),
            jax.ShapeDtypeStruct((BATCH, WINDOW, KV_WIDTH), F32),
            jax.ShapeDtypeStruct((BATCH, WINDOW, KV_WIDTH), F32),
            jax.ShapeDtypeStruct(kv_cache, F32),
            jax.ShapeDtypeStruct(kv_cache, F32),
            jax.ShapeDtypeStruct((T_SAMPLE, SGU_WIDTH), F32),
        ),
        scratch_shapes=[bf_band, bf_band, bf_band, bf_band, pltpu.VMEM((N_EXPERTS, 1), F32)],
        compiler_params=cparams(dimension_semantics=("arbitrary",)),
        name="mixer",
    )(x_prompt.reshape(T_PROMPT, D_MODEL), x_sample.reshape(T_SAMPLE, D_MODEL),
      cache_k[0].reshape(kv_cache), cache_v[0].reshape(kv_cache), *weights)

    counts = cnt_all[:, 0]
    blocks_per = (counts + EXPERT_ROWS - 1) // EXPERT_ROWS
    block_end = jnp.cumsum(blocks_per)
    pstart = (block_end - blocks_per) * EXPERT_ROWS
    n_valid = block_end[-1:].astype(i32)
    step = jnp.minimum(jnp.arange(N_ROW_BLOCKS, dtype=i32), n_valid - 1)
    expert_of_step = step[:, None] >= block_end[None, :]
    blk_e = jnp.minimum(jnp.sum(expert_of_step, axis=1), N_EXPERTS - 1).astype(i32)
    own = (jnp.arange(N_EXPERTS, dtype=i32)[None, :] == blk_e[:, None]).astype(i32)
    rows_left = jnp.sum(own * (counts + pstart)[None, :], axis=1) - step * EXPERT_ROWS
    blk_rows = jnp.clip(rows_left, 0, EXPERT_ROWS).astype(i32)
    dispatch_meta = jnp.concatenate([(pstart + counts).astype(i32), n_valid])

    xs = pl.pallas_call(
        _dispatch_kernel,
        grid_spec=pltpu.PrefetchScalarGridSpec(
            num_scalar_prefetch=1,
            grid=(N_TOKEN_BLOCKS,),
            in_specs=[
                pl.BlockSpec((N_EXPERTS, 1), lambda i, m: (0, 0)),
                pl.BlockSpec((TOP_K, TOKEN_BLOCK), lambda i, m: (0, i)),
                pl.BlockSpec((TOP_K, TOKEN_BLOCK), lambda i, m: (0, i)),
                pl.BlockSpec((TOKEN_BLOCK, ROW_WIDTH), lambda i, m: (i, 0)),
            ],
            out_specs=pl.BlockSpec(memory_space=pl.ANY),
            scratch_shapes=[
                pltpu.VMEM((FILL_ROWS, ROW_WIDTH), F32),
                pltpu.VMEM((TOP_K, TOKEN_BLOCK), i32),
                pltpu.SMEM((TOP_K, TOKEN_BLOCK), i32),
                pltpu.SemaphoreType.DMA, pltpu.SemaphoreType.DMA, pltpu.SemaphoreType.DMA,
            ],
        ),
        out_shape=jax.ShapeDtypeStruct((SORTED_ROWS, ROW_WIDTH), F32),
        compiler_params=cparams(dimension_semantics=("arbitrary",)),
        name="dispatch",
    )(dispatch_meta, pstart.reshape(N_EXPERTS, 1).astype(i32), idx_all, rank_all, h2_all)

    def row_map(i, be, nr, nv):
        return (i, 0)

    def expert_map(i, be, nr, nv):
        return (be[i], 0, 0)

    out_kt = pl.pallas_call(
        _expert_kernel,
        grid_spec=pltpu.PrefetchScalarGridSpec(
            num_scalar_prefetch=3,
            grid=(N_ROW_BLOCKS,),
            in_specs=[
                pl.BlockSpec((EXPERT_ROWS, ROW_WIDTH), row_map),
                pl.BlockSpec((None, D_MODEL, 2 * D_FF), expert_map),
                pl.BlockSpec((None, 1, 2 * D_FF), expert_map),
                pl.BlockSpec((None, D_FF, D_MODEL), expert_map),
                pl.BlockSpec((None, 1, D_MODEL), expert_map),
            ],
            out_specs=pl.BlockSpec(memory_space=pl.ANY),
            scratch_shapes=[
                pltpu.VMEM((D_MODEL, 2 * D_FF), BF16), pltpu.VMEM((D_FF, D_MODEL), BF16),
                pltpu.VMEM((2 * EXPERT_ROWS, D_MODEL), F32),
                pltpu.VMEM((2, ROW_TILE, EXPERT_ROWS), i32),
                pltpu.SMEM((2, ROW_TILE, EXPERT_ROWS), i32),
                pltpu.SemaphoreType.DMA((2,)), pltpu.SemaphoreType.DMA((2,)),
            ],
        ),
        out_shape=jax.ShapeDtypeStruct((OUT_ROWS, D_MODEL), F32),
        compiler_params=cparams(dimension_semantics=("arbitrary",)),
        name="experts",
    )(blk_e, blk_rows, n_valid, xs, w_gate_up[0], b_gate_up[0].reshape(N_EXPERTS, 1, 2 * D_FF), w_down[0],
      b_down[0].reshape(N_EXPERTS, 1, D_MODEL))

    def slot_spec(kk):
        return pl.BlockSpec((TOKEN_BLOCK, D_MODEL), lambda i: (kk * N_TOKEN_BLOCKS + i, 0))

    y_p, y_s = pl.pallas_call(
        _combine_kernel,
        grid=(N_TOKEN_BLOCKS,),
        in_specs=[slot_spec(kk) for kk in range(TOP_K)] + [
            pl.BlockSpec((TOKEN_BLOCK, TOP_K), lambda i: (i, 0)),
            pl.BlockSpec((TOKEN_BLOCK, D_MODEL), lambda i: (i, 0)),
            _full((1, D_MODEL)),
        ],
        out_specs=[
            pl.BlockSpec((TOKEN_BLOCK, D_MODEL), lambda i: (jnp.minimum(i, PROMPT_BLOCKS - 1), 0)),
            pl.BlockSpec((TOKEN_BLOCK, D_MODEL), lambda i: (jnp.maximum(i - PROMPT_BLOCKS, 0), 0)),
        ],
        out_shape=(jax.ShapeDtypeStruct((T_PROMPT, D_MODEL), F32), jax.ShapeDtypeStruct((T_SAMPLE, D_MODEL), F32)),
        compiler_params=cparams(dimension_semantics=("arbitrary",)),
        name="combine",
    )(out_kt, out_kt, out_kt, out_kt, gate_all.T, x1_all, fg)

    kv5 = (1, -1, WINDOW, N_KV_HEADS, HEAD_DIM)
    return (y_p.reshape(BATCH, SEQ, D_MODEL), y_s.reshape(DEC_BATCH, DEC_SEQ, D_MODEL),
            kp_new.reshape(kv5), vp_new.reshape(kv5), ks_new.reshape(kv5), vs_new.reshape(kv5),
            gv_s.reshape(1, DEC_BATCH, DEC_SEQ, SGU_HEADS, SGU_HEAD_DIM))
```

```python
import functools

import jax
import jax.numpy as jnp
from jax import lax
from jax.experimental import pallas as pl
from jax.experimental.pallas import tpu as pltpu

D_MODEL = 1024
BATCH = 2
SEQ = 8192
DEC_BATCH = 8
DEC_SEQ = 64
CHUNK = 64
WINDOW = 128
BAND = WINDOW + CHUNK
HEAD_DIM = 64
N_Q_HEADS = 8
N_KV_HEADS = 2
Q_PER_KV = N_Q_HEADS // N_KV_HEADS
ATTN_WIDTH = N_Q_HEADS * HEAD_DIM
KV_WIDTH = N_KV_HEADS * HEAD_DIM
QKV_COLS = ATTN_WIDTH + 2 * KV_WIDTH
SGU_CHUNK = 128
SGU_HEADS = 4
SGU_HEAD_DIM = 128
SGU_WIDTH = SGU_HEADS * SGU_HEAD_DIM
IN_COLS = QKV_COLS + 2 * SGU_WIDTH
N_EXPERTS = 32
TOP_K = 4
D_FF = 1024
SWIGLU_LIMIT = 7.0
SWIGLU_ALPHA = 1.702
RMS_EPS = 1e-5
NEG_INF = -1e30

LANES = 128
SUBLANES = 8
assert D_MODEL == LANES * SUBLANES

T_PROMPT = BATCH * SEQ
T_SAMPLE = DEC_BATCH * DEC_SEQ
T_ALL = T_PROMPT + T_SAMPLE
TOKEN_BLOCK = 512
SEQ_BLOCKS = SEQ // TOKEN_BLOCK
N_TOKEN_BLOCKS = T_ALL // TOKEN_BLOCK
PROMPT_BLOCKS = T_PROMPT // TOKEN_BLOCK
EXPERT_ROWS = 256
FF_TILE = 256
N_SLOTS = T_ALL * TOP_K
MIN_ROW_BLOCKS = N_SLOTS // EXPERT_ROWS
N_ROW_BLOCKS = MIN_ROW_BLOCKS + N_EXPERTS
SORTED_BLOCKS = N_ROW_BLOCKS + 1
SORTED_ROWS = SORTED_BLOCKS * EXPERT_ROWS
DUMP_ROW = N_SLOTS
OUT_ROWS = DUMP_ROW + EXPERT_ROWS
TAG_MASK = 0xFFFF
VMEM_LIMIT_BYTES = 56 * 1024 * 1024
ISSUE_UNROLL = 8

assert T_SAMPLE == TOKEN_BLOCK and SEQ % TOKEN_BLOCK == 0 and N_SLOTS % EXPERT_ROWS == 0
assert T_ALL <= TAG_MASK and D_FF % FF_TILE == 0 and EXPERT_ROWS % (D_FF // FF_TILE) == 0

F32 = jnp.float32
BF16 = jnp.bfloat16
I32 = jnp.int32


def _rms(x, g):
    return x * lax.rsqrt(jnp.mean(x * x, axis=-1, keepdims=True) + RMS_EPS) * g


def _gelu(x):
    return 0.5 * x * (1.0 + lax.erf(x * 0.7071067811865476))


def _store_tile_rows(ref, first_row, val):
    n = val.shape[0]
    for c in range(SUBLANES):
        ref[pl.ds(first_row * SUBLANES + c, n, stride=SUBLANES), :] = val[:, c * LANES:(c + 1) * LANES]


def _load_tile_rows(ref, n):
    return jnp.concatenate([ref[pl.ds(c, n, stride=SUBLANES), :] for c in range(SUBLANES)], axis=1)


def _tile_row(ref, row):
    return ref.at[pl.ds(pl.multiple_of(row * SUBLANES, SUBLANES), SUBLANES), :]


def _in_proj(x, n1g_ref, w_in_ref, sgng_ref):
    h = _rms(x, n1g_ref[...])
    cols = jnp.dot(h.astype(BF16), w_in_ref[...], preferred_element_type=F32)
    q = cols[:, :ATTN_WIDTH] * (HEAD_DIM ** -0.5)
    k = cols[:, ATTN_WIDTH:ATTN_WIDTH + KV_WIDTH]
    v = cols[:, ATTN_WIDTH + KV_WIDTH:QKV_COLS]
    gu = _gelu(cols[:, QKV_COLS:QKV_COLS + SGU_WIDTH])
    gv = _gelu(cols[:, QKV_COLS + SGU_WIDTH:])
    sg = sgng_ref[...]
    gvn = jnp.concatenate(
        [_rms(gv[:, i * SGU_HEAD_DIM:(i + 1) * SGU_HEAD_DIM], sg[:, i * SGU_HEAD_DIM:(i + 1) * SGU_HEAD_DIM])
         for i in range(SGU_HEADS)], axis=1)
    return q, k, v, gu, gvn


def _dup_heads(t):
    lane = lax.broadcasted_iota(I32, t.shape, 1)
    r = pltpu.roll(t, HEAD_DIM, axis=1)
    return jnp.where(lane < HEAD_DIM, t, r), jnp.where(lane < HEAD_DIM, r, t)


def _attend_chunk(q64, kk, vv, alibi_ref, sink_ref, key_bias):
    lane = lax.broadcasted_iota(I32, (CHUNK, 2 * HEAD_DIM), 1)
    lo = lane < HEAD_DIM
    pairs = []
    for g in range(N_KV_HEADS):
        rows = []
        for p in range(2):
            qp = q64[:, (2 * g + p) * 128:(2 * g + p + 1) * 128]
            rows.append(jnp.where(lo, qp, 0.0))
            rows.append(jnp.where(lo, 0.0, qp))
        q4 = jnp.concatenate(rows, axis=0).astype(BF16)
        sc = lax.dot_general(q4, kk[g], (((1,), (1,)), ((), ())), preferred_element_type=F32)
        sc = sc + alibi_ref[g]
        if key_bias is not None:
            sc = sc + key_bias
        sink = sink_ref[g]
        m = jnp.maximum(jnp.max(sc, axis=1, keepdims=True), sink)
        p_un = jnp.exp(sc - m)
        den = jnp.sum(p_un, axis=1, keepdims=True) + jnp.exp(sink - m)
        o = jnp.dot(p_un.astype(BF16), vv[g], preferred_element_type=F32) / den
        for p in range(2):
            pairs.append(jnp.where(lo, o[(2 * p) * CHUNK:(2 * p + 1) * CHUNK],
                                   o[(2 * p + 1) * CHUNK:(2 * p + 2) * CHUNK]))
    return jnp.concatenate(pairs, axis=1)


def _sgu_weights(sguw_ref):
    sub_i = lax.broadcasted_iota(I32, (SGU_CHUNK, SGU_CHUNK), 0) // CHUNK
    sub_j = lax.broadcasted_iota(I32, (SGU_CHUNK, SGU_CHUNK), 1) // CHUNK
    keep = sub_j <= sub_i
    return [jnp.where(keep, sguw_ref[i], 0.0).astype(BF16) for i in range(SGU_HEADS)]


def _sgu_chunk(ws, gu_c, gvn_c, sgubt_ref):
    outs = []
    for i in range(SGU_HEADS):
        sl = slice(i * SGU_HEAD_DIM, (i + 1) * SGU_HEAD_DIM)
        sp = jnp.dot(ws[i], gvn_c[:, sl].astype(BF16), preferred_element_type=F32) + sgubt_ref[:, i:i + 1]
        outs.append(gu_c[:, sl] * sp)
    return jnp.concatenate(outs, axis=1)


def _merge_route(x, attn, sgu, tok_base, aog_ref, sog_ref, w_out_ref, n2g_ref, wr_ref, br_ref, cnt_scr,
                 x1_ref, h2_ref, idx_ref, rank_ref, gate_ref):
    n = x.shape[0]
    a_n = _rms(attn, aog_ref[...]).astype(BF16)
    s_n = _rms(sgu, sog_ref[...]).astype(BF16)
    x1 = (x + jnp.dot(a_n, w_out_ref[:ATTN_WIDTH, :], preferred_element_type=F32)
          + jnp.dot(s_n, w_out_ref[ATTN_WIDTH:, :], preferred_element_type=F32))
    x1_ref[...] = x1
    h2 = _rms(x1, n2g_ref[...])
    logits = lax.dot_general(wr_ref[...], h2, (((1,), (1,)), ((), ())),
                             precision=lax.Precision.HIGHEST, preferred_element_type=F32) + br_ref[...]
    e_iota = lax.broadcasted_iota(I32, (N_EXPERTS, n), 0).astype(F32)
    cur = logits
    vals, idxs, sels = [], [], []
    for _ in range(TOP_K):
        m = jnp.max(cur, axis=0, keepdims=True)
        idx = jnp.min(jnp.where(cur == m, e_iota, float(N_EXPERTS)), axis=0, keepdims=True)
        sel = e_iota == idx
        vals.append(m)
        idxs.append(idx)
        sels.append(sel)
        cur = jnp.where(sel, -jnp.inf, cur)
    exps = [jnp.exp(v - vals[0]) for v in vals]
    den = exps[0] + exps[1] + exps[2] + exps[3]
    gate_ref[...] = jnp.concatenate([e / den for e in exps], axis=0)
    onehot = jnp.where(sels[0] | sels[1] | sels[2] | sels[3], 1.0, 0.0)
    before = lax.broadcasted_iota(I32, (n, n), 0) < lax.broadcasted_iota(I32, (n, n), 1)
    upper = jnp.where(before, 1.0, 0.0).astype(BF16)
    prior = jnp.dot(onehot.astype(BF16), upper, preferred_element_type=F32) + cnt_scr[...]
    ranks = [jnp.sum(jnp.where(s, prior, 0.0), axis=0, keepdims=True) for s in sels]
    idx_ref[...] = jnp.concatenate(idxs, axis=0).astype(I32)
    rank_ref[...] = jnp.concatenate(ranks, axis=0).astype(I32)
    cnt_scr[...] = cnt_scr[...] + jnp.sum(onehot, axis=1, keepdims=True)

    words = lax.bitcast_convert_type(h2.astype(BF16).astype(F32), I32)
    tok = (lax.broadcasted_iota(I32, (1, n), 1) + tok_base).astype(F32)
    tag_t = jnp.concatenate([tok] + idxs + [jnp.zeros((LANES - 1 - TOP_K, n), F32)], axis=0)
    tagged = words[:, :LANES] | tag_t.T.astype(I32)
    _store_tile_rows(h2_ref, 0, jnp.concatenate([tagged, words[:, LANES:]], axis=1))


def _mixer_kernel(xp_ref, xs_in_ref, ck_ref, cv_ref, n1g_ref, w_in_ref, sgng_ref, sink_ref, alibi_ref, sguw_ref,
                  sgubt_ref, aog_ref, sog_ref, w_out_ref, n2g_ref, wr_ref, br_ref,
                  x1_ref, h2_ref, idx_ref, rank_ref, gate_ref, cnt_ref, kp_ref, vp_ref, ks_ref, vs_ref, gv_ref,
                  kk0, kk1, vv0, vv1, cnt_scr):
    n = pl.program_id(0)
    s = n % SEQ_BLOCKS
    weights = (aog_ref, sog_ref, w_out_ref, n2g_ref, wr_ref, br_ref)
    outs = (x1_ref, h2_ref, idx_ref, rank_ref, gate_ref)

    @pl.when(n == 0)
    def _():
        cnt_scr[...] = jnp.zeros_like(cnt_scr)

    @pl.when(n < PROMPT_BLOCKS)
    def _():
        @pl.when(s == 0)
        def _():
            zeros = jnp.zeros((WINDOW, 2 * HEAD_DIM), BF16)
            for buf in (kk0, kk1, vv0, vv1):
                buf[0:WINDOW, :] = zeros

        x = xp_ref[...]
        q, k, v, gu, gvn = _in_proj(x, n1g_ref, w_in_ref, sgng_ref)

        @pl.when(s == SEQ_BLOCKS - 1)
        def _():
            kp_ref[...] = k[TOKEN_BLOCK - WINDOW:, :]
            vp_ref[...] = v[TOKEN_BLOCK - WINDOW:, :]

        k0, k1 = _dup_heads(k)
        v0, v1 = _dup_heads(v)
        for buf, val in ((kk0, k0), (kk1, k1), (vv0, v0), (vv1, v1)):
            buf[WINDOW:, :] = val.astype(BF16)

        attn_chunks = []
        for j in range(TOKEN_BLOCK // CHUNK):
            band = slice(j * CHUNK, j * CHUNK + BAND)
            key_bias = None
            if j * CHUNK < WINDOW:
                key_pos = lax.broadcasted_iota(I32, (1, BAND), 1) + (s * TOKEN_BLOCK + j * CHUNK - WINDOW)
                key_bias = jnp.where(key_pos < 0, NEG_INF, 0.0)
            attn_chunks.append(_attend_chunk(q[j * CHUNK:(j + 1) * CHUNK], (kk0[band, :], kk1[band, :]),
                                             (vv0[band, :], vv1[band, :]), alibi_ref, sink_ref, key_bias))
        attn = jnp.concatenate(attn_chunks, axis=0)

        for buf in (kk0, kk1, vv0, vv1):
            buf[0:WINDOW, :] = buf[TOKEN_BLOCK:TOKEN_BLOCK + WINDOW, :]

        ws = _sgu_weights(sguw_ref)
        sgu = jnp.concatenate(
            [_sgu_chunk(ws, gu[c * SGU_CHUNK:(c + 1) * SGU_CHUNK], gvn[c * SGU_CHUNK:(c + 1) * SGU_CHUNK],
                        sgubt_ref) for c in range(TOKEN_BLOCK // SGU_CHUNK)], axis=0)
        _merge_route(x, attn, sgu, n * TOKEN_BLOCK, *weights, cnt_scr, *outs)

    @pl.when(n == PROMPT_BLOCKS)
    def _():
        x = xs_in_ref[...]
        q, k, v, gu, gvn = _in_proj(x, n1g_ref, w_in_ref, sgng_ref)
        gv_ref[...] = gvn
        ws = _sgu_weights(sguw_ref)
        attn_chunks, sgu_chunks = [], []
        pad = jnp.zeros((SGU_CHUNK - DEC_SEQ, SGU_WIDTH), F32)
        for b in range(DEC_BATCH):
            rows = slice(b * DEC_SEQ, (b + 1) * DEC_SEQ)
            k_all = jnp.concatenate([ck_ref[b], k[rows]], axis=0)
            v_all = jnp.concatenate([cv_ref[b], v[rows]], axis=0)
            ks_ref[b] = k_all[DEC_SEQ:]
            vs_ref[b] = v_all[DEC_SEQ:]
            k0, k1 = _dup_heads(k_all)
            v0, v1 = _dup_heads(v_all)
            attn_chunks.append(_attend_chunk(q[rows], (k0.astype(BF16), k1.astype(BF16)),
                                             (v0.astype(BF16), v1.astype(BF16)), alibi_ref, sink_ref, None))
            gated = _sgu_chunk(ws, jnp.concatenate([gu[rows], pad], axis=0),
                               jnp.concatenate([gvn[rows], pad], axis=0), sgubt_ref)
            sgu_chunks.append(gated[:DEC_SEQ])
        attn = jnp.concatenate(attn_chunks, axis=0)
        sgu = jnp.concatenate(sgu_chunks, axis=0)
        _merge_route(x, attn, sgu, n * TOKEN_BLOCK, *weights, cnt_scr, *outs)

    cnt_ref[...] = cnt_scr[...].astype(I32)


def _dispatch_kernel(meta_ref, pstart_ref, idx_ref, rank_ref, h2_ref, xs_ref,
                     zero_buf, dest_vmem, dest_smem, sem_rows, sem_idx, sem_fill):
    i = pl.program_id(0)
    block_words = EXPERT_ROWS * SUBLANES

    def fill_copy(first_row):
        start = pl.multiple_of(first_row * SUBLANES, SUBLANES)
        return pltpu.make_async_copy(zero_buf, xs_ref.at[pl.ds(start, block_words), :], sem_fill)

    @pl.when(i == 0)
    def _():
        zero_buf[...] = jnp.zeros_like(zero_buf)
        for e in range(N_EXPERTS):
            fill_copy(meta_ref[e]).start()
        for e in range(N_EXPERTS):
            fill_copy(meta_ref[e]).wait()
        for j in range(SORTED_BLOCKS - MIN_ROW_BLOCKS):
            @pl.when(meta_ref[N_EXPERTS] + j < SORTED_BLOCKS)
            def _():
                fill_copy((meta_ref[N_EXPERTS] + j) * EXPERT_ROWS).start()
        for j in range(SORTED_BLOCKS - MIN_ROW_BLOCKS):
            @pl.when(meta_ref[N_EXPERTS] + j < SORTED_BLOCKS)
            def _():
                fill_copy((meta_ref[N_EXPERTS] + j) * EXPERT_ROWS).wait()

    e_iota = lax.broadcasted_iota(I32, (N_EXPERTS, TOKEN_BLOCK), 0)
    pstart = pstart_ref[...]
    idx = idx_ref[...]
    dest = jnp.concatenate(
        [jnp.sum(jnp.where(e_iota == idx[kk:kk + 1, :], pstart, 0), axis=0, keepdims=True)
         for kk in range(TOP_K)], axis=0) + rank_ref[...]
    dest_vmem[...] = dest
    to_smem = pltpu.make_async_copy(dest_vmem, dest_smem, sem_idx)
    to_smem.start()
    to_smem.wait()

    def issue(t, carry):
        for kk in range(TOP_K):
            pltpu.make_async_copy(_tile_row(h2_ref, t), _tile_row(xs_ref, dest_smem[kk, t]),
                                  sem_rows).start(priority=kk % 2)
        return carry

    lax.fori_loop(0, TOKEN_BLOCK, issue, 0, unroll=ISSUE_UNROLL)
    for _ in range(TOP_K):
        pltpu.make_async_copy(h2_ref, xs_ref.at[pl.ds(0, TOKEN_BLOCK * SUBLANES), :], sem_rows).wait()


def _expert_kernel(be_ref, nrow_ref, nv_ref, xs_ref, wgu_ref, bgu_ref, wd_ref, bd_ref, out_ref,
                   wgu_bf, wd_bf, stage, ids_vmem, ids_smem, sem_rows, sem_ids):
    i = pl.program_id(0)
    nv = nv_ref[0]
    slot = i % 2
    n_phase = D_FF // FF_TILE
    rows_per_phase = EXPERT_ROWS // n_phase

    def ids_copy(s):
        return pltpu.make_async_copy(ids_vmem.at[s], ids_smem.at[s], sem_ids.at[s])

    def rows_wait(s):
        whole = pl.ds(0, EXPERT_ROWS * SUBLANES)
        pltpu.make_async_copy(stage.at[whole, :], out_ref.at[whole, :], sem_rows.at[s]).wait()

    def issue_rows(s, first, last):
        for r in range(first, last):
            pltpu.make_async_copy(_tile_row(stage, s * EXPERT_ROWS + r), _tile_row(out_ref, ids_smem[s, 0, r]),
                                  sem_rows.at[s]).start(priority=r % 2)

    def compute(s, prev):
        expert = be_ref[i]
        words = _load_tile_rows(xs_ref, EXPERT_ROWS)
        tag = words[:, :LANES] & TAG_MASK
        x = lax.bitcast_convert_type(words & ~TAG_MASK, F32).astype(BF16)
        if prev is not None:
            ids_copy(prev).wait()
        acc = None
        for p in range(n_phase):
            glu_cols = slice(p * FF_TILE, (p + 1) * FF_TILE)
            lin_cols = slice(D_FF + p * FF_TILE, D_FF + (p + 1) * FF_TILE)
            x_glu = jnp.dot(x, wgu_bf[:, glu_cols], preferred_element_type=F32) + bgu_ref[:, glu_cols]
            x_lin = jnp.dot(x, wgu_bf[:, lin_cols], preferred_element_type=F32) + bgu_ref[:, lin_cols]
            x_glu = jnp.minimum(x_glu, SWIGLU_LIMIT)
            x_lin = jnp.clip(x_lin, -SWIGLU_LIMIT, SWIGLU_LIMIT)
            act = x_glu * (1.0 / (1.0 + jnp.exp(-SWIGLU_ALPHA * x_glu))) * (x_lin + 1.0)
            part = jnp.dot(act.astype(BF16), wd_bf[glu_cols, :], preferred_element_type=F32)
            acc = part if acc is None else acc + part
            if prev is not None:
                issue_rows(prev, p * rows_per_phase, (p + 1) * rows_per_phase)
        _store_tile_rows(stage, s * EXPERT_ROWS, acc + bd_ref[...])
        k_slot = sum(j * jnp.where(tag[:, 1 + j:2 + j] == expert, 1, 0) for j in range(1, TOP_K))
        row = lax.broadcasted_iota(I32, (EXPERT_ROWS, 1), 0)
        dst = jnp.where(row < nrow_ref[i], k_slot * T_ALL + tag[:, 0:1], DUMP_ROW + row)
        dst_t = jnp.broadcast_to(dst.astype(F32), (EXPERT_ROWS, LANES)).T
        ids_vmem[s] = dst_t[0:SUBLANES].astype(I32)
        ids_copy(s).start()

    @pl.when(i == 0)
    def _():
        second = pl.ds(EXPERT_ROWS * SUBLANES, EXPERT_ROWS * SUBLANES)
        stage[second, :] = jnp.zeros((EXPERT_ROWS * SUBLANES, LANES), F32)
        dump = pltpu.make_async_copy(stage.at[second, :],
                                     out_ref.at[pl.ds(DUMP_ROW * SUBLANES, EXPERT_ROWS * SUBLANES), :],
                                     sem_rows.at[1])
        dump.start()
        dump.wait()

    @pl.when((i >= 2) & (i <= nv))
    def _():
        rows_wait(slot)

    @pl.when((i < nv) & ((i == 0) | (be_ref[i] != be_ref[jnp.maximum(i - 1, 0)])))
    def _():
        rows = 128

        def cast_rows(c, carry):
            r = pl.multiple_of(c * rows, rows)
            wgu_bf[pl.ds(r, rows), :] = wgu_ref[pl.ds(r, rows), :].astype(BF16)
            wd_bf[pl.ds(r, rows), :] = wd_ref[pl.ds(r, rows), :].astype(BF16)
            return carry

        lax.fori_loop(0, D_MODEL // rows, cast_rows, 0)

    @pl.when(i == 0)
    def _():
        compute(0, None)

    @pl.when((i >= 1) & (i < nv))
    def _():
        compute(slot, 1 - slot)

    @pl.when(i == nv)
    def _():
        ids_copy(1 - slot).wait()
        issue_rows(1 - slot, 0, EXPERT_ROWS)
        rows_wait(1 - slot)


def _combine_kernel(o0_ref, o1_ref, o2_ref, o3_ref, gate_ref, x1_ref, fg_ref, yp_ref, ys_ref):
    i = pl.program_id(0)
    gates = gate_ref[...]
    y = x1_ref[...]
    for kk, o_ref in enumerate((o0_ref, o1_ref, o2_ref, o3_ref)):
        y = y + gates[:, kk:kk + 1] * _load_tile_rows(o_ref, TOKEN_BLOCK)
    out = _rms(y, fg_ref[...])

    @pl.when(i < PROMPT_BLOCKS)
    def _():
        yp_ref[...] = out

    @pl.when(i >= PROMPT_BLOCKS)
    def _():
        ys_ref[...] = out


def _full(shape):
    return pl.BlockSpec(shape, lambda *_: (0,) * len(shape))


def kernel(x_prompt, x_sample, cache_k, cache_v, norm1_g, w_in, sgu_norm_g, attn_sinks, sgu_w, sgu_b, attn_out_g,
           sgu_out_g, w_out, norm2_g, w_router, b_router, w_gate_up, b_gate_up, w_down, b_down, final_g):
    n1g = norm1_g[0].reshape(1, D_MODEL)
    w_in_b = w_in[0].astype(BF16)
    sgng = sgu_norm_g[0].reshape(1, SGU_WIDTH)
    sink_cols = jnp.repeat(attn_sinks[0].reshape(N_KV_HEADS, Q_PER_KV, 1), CHUNK, axis=1).reshape(
        N_KV_HEADS, Q_PER_KV * CHUNK, 1)
    slopes = jnp.exp2(-8.0 * jnp.arange(1, N_Q_HEADS + 1, dtype=F32) / N_Q_HEADS).reshape(N_KV_HEADS, Q_PER_KV)
    dist = jnp.abs(jnp.arange(CHUNK)[:, None] + WINDOW - jnp.arange(BAND)[None, :]).astype(F32)
    alibi = (-slopes[:, :, None, None] * dist).reshape(N_KV_HEADS, Q_PER_KV * CHUNK, BAND)
    sguw = sgu_w[0]
    sgubt = sgu_b[0].T
    aog = attn_out_g[0].reshape(1, ATTN_WIDTH)
    sog = sgu_out_g[0].reshape(1, SGU_WIDTH)
    w_out_b = w_out[0].astype(BF16)
    n2g = norm2_g[0].reshape(1, D_MODEL)
    wr_t = w_router[0].T
    br_c = b_router[0].reshape(N_EXPERTS, 1)
    fg = final_g.reshape(1, D_MODEL)

    weight_specs = [
        _full((1, D_MODEL)), _full((D_MODEL, IN_COLS)), _full((1, SGU_WIDTH)),
        _full((N_KV_HEADS, Q_PER_KV * CHUNK, 1)), _full((N_KV_HEADS, Q_PER_KV * CHUNK, BAND)),
        _full((SGU_HEADS, SGU_CHUNK, SGU_CHUNK)), _full((SGU_CHUNK, SGU_HEADS)),
        _full((1, ATTN_WIDTH)), _full((1, SGU_WIDTH)), _full((D_MODEL, D_MODEL)), _full((1, D_MODEL)),
        _full((N_EXPERTS, D_MODEL)), _full((N_EXPERTS, 1)),
    ]
    weights = (n1g, w_in_b, sgng, sink_cols, alibi, sguw, sgubt, aog, sog, w_out_b, n2g, wr_t, br_c)
    cparams = functools.partial(pltpu.CompilerParams, vmem_limit_bytes=VMEM_LIMIT_BYTES)

    def tok_map(n):
        return (n, 0)

    def tok_map_t(n):
        return (0, n)

    def stream_map(n):
        return (jnp.minimum(n // SEQ_BLOCKS, BATCH - 1), 0, 0)

    kv_cache = (DEC_BATCH, WINDOW, KV_WIDTH)
    bf_band = pltpu.VMEM((WINDOW + TOKEN_BLOCK, 2 * HEAD_DIM), BF16)
    (x1_all, h2_all, idx_all, rank_all, gate_all, cnt_all, kp_new, vp_new, ks_new, vs_new, gv_s) = pl.pallas_call(
        _mixer_kernel,
        grid=(N_TOKEN_BLOCKS,),
        in_specs=[pl.BlockSpec((TOKEN_BLOCK, D_MODEL), lambda n: (jnp.minimum(n, PROMPT_BLOCKS - 1), 0)),
                  _full((T_SAMPLE, D_MODEL)), _full(kv_cache), _full(kv_cache)] + weight_specs,
        out_specs=[
            pl.BlockSpec((TOKEN_BLOCK, D_MODEL), tok_map),
            pl.BlockSpec((TOKEN_BLOCK * SUBLANES, LANES), tok_map),
            pl.BlockSpec((TOP_K, TOKEN_BLOCK), tok_map_t), pl.BlockSpec((TOP_K, TOKEN_BLOCK), tok_map_t),
            pl.BlockSpec((TOP_K, TOKEN_BLOCK), tok_map_t),
            _full((N_EXPERTS, 1)),
            pl.BlockSpec((None, WINDOW, KV_WIDTH), stream_map), pl.BlockSpec((None, WINDOW, KV_WIDTH), stream_map),
            _full(kv_cache), _full(kv_cache), _full((T_SAMPLE, SGU_WIDTH)),
        ],
        out_shape=(
            jax.ShapeDtypeStruct((T_ALL, D_MODEL), F32),
            jax.ShapeDtypeStruct((T_ALL * SUBLANES, LANES), I32),
            jax.ShapeDtypeStruct((TOP_K, T_ALL), I32),
            jax.ShapeDtypeStruct((TOP_K, T_ALL), I32),
            jax.ShapeDtypeStruct((TOP_K, T_ALL), F32),
            jax.ShapeDtypeStruct((N_EXPERTS, 1), I32),
            jax.ShapeDtypeStruct((BATCH, WINDOW, KV_WIDTH), F32),
            jax.ShapeDtypeStruct((BATCH, WINDOW, KV_WIDTH), F32),
            jax.ShapeDtypeStruct(kv_cache, F32),
            jax.ShapeDtypeStruct(kv_cache, F32),
            jax.ShapeDtypeStruct((T_SAMPLE, SGU_WIDTH), F32),
        ),
        scratch_shapes=[bf_band, bf_band, bf_band, bf_band, pltpu.VMEM((N_EXPERTS, 1), F32)],
        compiler_params=cparams(dimension_semantics=("arbitrary",)),
        name="mixer",
    )(x_prompt.reshape(T_PROMPT, D_MODEL), x_sample.reshape(T_SAMPLE, D_MODEL),
      cache_k[0].reshape(kv_cache), cache_v[0].reshape(kv_cache), *weights)

    counts = cnt_all[:, 0]
    blocks_per = (counts + EXPERT_ROWS - 1) // EXPERT_ROWS
    block_end = jnp.cumsum(blocks_per)
    pstart = (block_end - blocks_per) * EXPERT_ROWS
    n_valid = block_end[-1:].astype(I32)
    step = jnp.minimum(jnp.arange(N_ROW_BLOCKS, dtype=I32), n_valid - 1)
    expert_of_step = step[:, None] >= block_end[None, :]
    blk_e = jnp.minimum(jnp.sum(expert_of_step, axis=1), N_EXPERTS - 1).astype(I32)
    own = (jnp.arange(N_EXPERTS, dtype=I32)[None, :] == blk_e[:, None]).astype(I32)
    rows_left = jnp.sum(own * (counts + pstart)[None, :], axis=1) - step * EXPERT_ROWS
    blk_rows = jnp.clip(rows_left, 0, EXPERT_ROWS).astype(I32)
    dispatch_meta = jnp.concatenate([(pstart + counts).astype(I32), n_valid])

    xs = pl.pallas_call(
        _dispatch_kernel,
        grid_spec=pltpu.PrefetchScalarGridSpec(
            num_scalar_prefetch=1,
            grid=(N_TOKEN_BLOCKS,),
            in_specs=[
                pl.BlockSpec((N_EXPERTS, 1), lambda i, m: (0, 0)),
                pl.BlockSpec((TOP_K, TOKEN_BLOCK), lambda i, m: (0, i)),
                pl.BlockSpec((TOP_K, TOKEN_BLOCK), lambda i, m: (0, i)),
                pl.BlockSpec((TOKEN_BLOCK * SUBLANES, LANES), lambda i, m: (i, 0)),
            ],
            out_specs=pl.BlockSpec(memory_space=pl.ANY),
            scratch_shapes=[
                pltpu.VMEM((EXPERT_ROWS * SUBLANES, LANES), I32),
                pltpu.VMEM((TOP_K, TOKEN_BLOCK), I32),
                pltpu.SMEM((TOP_K, TOKEN_BLOCK), I32),
                pltpu.SemaphoreType.DMA, pltpu.SemaphoreType.DMA, pltpu.SemaphoreType.DMA,
            ],
        ),
        out_shape=jax.ShapeDtypeStruct((SORTED_ROWS * SUBLANES, LANES), I32),
        compiler_params=cparams(dimension_semantics=("arbitrary",)),
        name="dispatch",
    )(dispatch_meta, pstart.reshape(N_EXPERTS, 1).astype(I32), idx_all, rank_all, h2_all)

    def row_map(i, be, nr, nv):
        return (i, 0)

    def expert_map(i, be, nr, nv):
        return (be[i], 0, 0)

    out_kt = pl.pallas_call(
        _expert_kernel,
        grid_spec=pltpu.PrefetchScalarGridSpec(
            num_scalar_prefetch=3,
            grid=(N_ROW_BLOCKS,),
            in_specs=[
                pl.BlockSpec((EXPERT_ROWS * SUBLANES, LANES), row_map),
                pl.BlockSpec((None, D_MODEL, 2 * D_FF), expert_map),
                pl.BlockSpec((None, 1, 2 * D_FF), expert_map),
                pl.BlockSpec((None, D_FF, D_MODEL), expert_map),
                pl.BlockSpec((None, 1, D_MODEL), expert_map),
            ],
            out_specs=pl.BlockSpec(memory_space=pl.ANY),
            scratch_shapes=[
                pltpu.VMEM((D_MODEL, 2 * D_FF), BF16), pltpu.VMEM((D_FF, D_MODEL), BF16),
                pltpu.VMEM((2 * EXPERT_ROWS * SUBLANES, LANES), F32),
                pltpu.VMEM((2, SUBLANES, EXPERT_ROWS), I32),
                pltpu.SMEM((2, SUBLANES, EXPERT_ROWS), I32),
                pltpu.SemaphoreType.DMA((2,)), pltpu.SemaphoreType.DMA((2,)),
            ],
        ),
        out_shape=jax.ShapeDtypeStruct((OUT_ROWS * SUBLANES, LANES), F32),
        compiler_params=cparams(dimension_semantics=("arbitrary",)),
        name="experts",
    )(blk_e, blk_rows, n_valid, xs, w_gate_up[0], b_gate_up[0].reshape(N_EXPERTS, 1, 2 * D_FF), w_down[0],
      b_down[0].reshape(N_EXPERTS, 1, D_MODEL))

    def slot_spec(kk):
        return pl.BlockSpec((TOKEN_BLOCK * SUBLANES, LANES), lambda i: (kk * N_TOKEN_BLOCKS + i, 0))

    y_p, y_s = pl.pallas_call(
        _combine_kernel,
        grid=(N_TOKEN_BLOCKS,),
        in_specs=[slot_spec(kk) for kk in range(TOP_K)] + [
            pl.BlockSpec((TOKEN_BLOCK, TOP_K), lambda i: (i, 0)),
            pl.BlockSpec((TOKEN_BLOCK, D_MODEL), lambda i: (i, 0)),
            _full((1, D_MODEL)),
        ],
        out_specs=[
            pl.BlockSpec((TOKEN_BLOCK, D_MODEL), lambda i: (jnp.minimum(i, PROMPT_BLOCKS - 1), 0)),
            pl.BlockSpec((TOKEN_BLOCK, D_MODEL), lambda i: (jnp.maximum(i - PROMPT_BLOCKS, 0), 0)),
        ],
        out_shape=(jax.ShapeDtypeStruct((T_PROMPT, D_MODEL), F32), jax.ShapeDtypeStruct((T_SAMPLE, D_MODEL), F32)),
        compiler_params=cparams(dimension_semantics=("arbitrary",)),
        name="combine",
    )(out_kt, out_kt, out_kt, out_kt, gate_all.T, x1_all, fg)

    kv5 = (1, -1, WINDOW, N_KV_HEADS, HEAD_DIM)
    return (y_p.reshape(BATCH, SEQ, D_MODEL), y_s.reshape(DEC_BATCH, DEC_SEQ, D_MODEL),
            kp_new.reshape(kv5), vp_new.reshape(kv5), ks_new.reshape(kv5), vs_new.reshape(kv5),
            gv_s.reshape(1, DEC_BATCH, DEC_SEQ, SGU_HEADS, SGU_HEAD_DIM))
```

```python
import functools

import jax
import jax.numpy as jnp
from jax import lax
from jax.experimental import pallas as pl
from jax.experimental.pallas import tpu as pltpu

D_MODEL = 1024
BATCH = 2
SEQ = 8192
DEC_BATCH = 8
DEC_SEQ = 64
CHUNK = 64
WINDOW = 128
BAND = WINDOW + CHUNK
HEAD_DIM = 64
N_Q_HEADS = 8
N_KV_HEADS = 2
Q_PER_KV = N_Q_HEADS // N_KV_HEADS
ATTN_WIDTH = N_Q_HEADS * HEAD_DIM
KV_WIDTH = N_KV_HEADS * HEAD_DIM
QKV_COLS = ATTN_WIDTH + 2 * KV_WIDTH
SGU_CHUNK = 128
SGU_HEADS = 4
SGU_HEAD_DIM = 128
SGU_WIDTH = SGU_HEADS * SGU_HEAD_DIM
IN_COLS = QKV_COLS + 2 * SGU_WIDTH
N_EXPERTS = 32
TOP_K = 4
D_FF = 1024
SWIGLU_LIMIT = 7.0
SWIGLU_ALPHA = 1.702
RMS_EPS = 1e-5
NEG_INF = -1e30

LANES = 128
SUBLANES = 8
assert D_MODEL == LANES * SUBLANES

T_PROMPT = BATCH * SEQ
T_SAMPLE = DEC_BATCH * DEC_SEQ
T_ALL = T_PROMPT + T_SAMPLE
TOKEN_BLOCK = 512
SEQ_BLOCKS = SEQ // TOKEN_BLOCK
N_TOKEN_BLOCKS = T_ALL // TOKEN_BLOCK
PROMPT_BLOCKS = T_PROMPT // TOKEN_BLOCK
EXPERT_ROWS = 256
FF_TILE = 256
N_SLOTS = T_ALL * TOP_K
MIN_ROW_BLOCKS = N_SLOTS // EXPERT_ROWS
N_ROW_BLOCKS = MIN_ROW_BLOCKS + N_EXPERTS
SORTED_BLOCKS = N_ROW_BLOCKS + 1
SORTED_ROWS = SORTED_BLOCKS * EXPERT_ROWS
DUMP_ROW = N_SLOTS
OUT_ROWS = DUMP_ROW + 2 * EXPERT_ROWS
TAG_MASK = 0xFFFF
VMEM_LIMIT_BYTES = 56 * 1024 * 1024
ISSUE_UNROLL = 8

assert T_SAMPLE == TOKEN_BLOCK and SEQ % TOKEN_BLOCK == 0 and N_SLOTS % EXPERT_ROWS == 0
assert T_ALL <= TAG_MASK and D_FF % FF_TILE == 0 and EXPERT_ROWS % (D_FF // FF_TILE) == 0

F32 = jnp.float32
BF16 = jnp.bfloat16
I32 = jnp.int32


def _rms(x, g):
    return x * lax.rsqrt(jnp.mean(x * x, axis=-1, keepdims=True) + RMS_EPS) * g


def _gelu(x):
    return 0.5 * x * (1.0 + lax.erf(x * 0.7071067811865476))


def _store_tile_rows(ref, first_row, val):
    n = val.shape[0]
    for c in range(SUBLANES):
        ref[pl.ds(first_row * SUBLANES + c, n, stride=SUBLANES), :] = val[:, c * LANES:(c + 1) * LANES]


def _load_tile_rows(ref, n):
    return jnp.concatenate([ref[pl.ds(c, n, stride=SUBLANES), :] for c in range(SUBLANES)], axis=1)


def _tile_row(ref, row):
    return ref.at[pl.ds(pl.multiple_of(row * SUBLANES, SUBLANES), SUBLANES), :]


def _in_proj(x, n1g_ref, w_in_ref, sgng_ref):
    h = _rms(x, n1g_ref[...])
    cols = jnp.dot(h.astype(BF16), w_in_ref[...], preferred_element_type=F32)
    q = cols[:, :ATTN_WIDTH] * (HEAD_DIM ** -0.5)
    k = cols[:, ATTN_WIDTH:ATTN_WIDTH + KV_WIDTH]
    v = cols[:, ATTN_WIDTH + KV_WIDTH:QKV_COLS]
    gu = _gelu(cols[:, QKV_COLS:QKV_COLS + SGU_WIDTH])
    gv = _gelu(cols[:, QKV_COLS + SGU_WIDTH:])
    sg = sgng_ref[...]
    gvn = jnp.concatenate(
        [_rms(gv[:, i * SGU_HEAD_DIM:(i + 1) * SGU_HEAD_DIM], sg[:, i * SGU_HEAD_DIM:(i + 1) * SGU_HEAD_DIM])
         for i in range(SGU_HEADS)], axis=1)
    return q, k, v, gu, gvn


def _dup_heads(t):
    lane = lax.broadcasted_iota(I32, t.shape, 1)
    r = pltpu.roll(t, HEAD_DIM, axis=1)
    return jnp.where(lane < HEAD_DIM, t, r), jnp.where(lane < HEAD_DIM, r, t)


def _attend_chunk(q64, kk, vv, alibi_ref, sink_ref, key_bias):
    lane = lax.broadcasted_iota(I32, (CHUNK, 2 * HEAD_DIM), 1)
    lo = lane < HEAD_DIM
    pairs = []
    for g in range(N_KV_HEADS):
        rows = []
        for p in range(2):
            qp = q64[:, (2 * g + p) * 128:(2 * g + p + 1) * 128]
            rows.append(jnp.where(lo, qp, 0.0))
            rows.append(jnp.where(lo, 0.0, qp))
        q4 = jnp.concatenate(rows, axis=0).astype(BF16)
        sc = lax.dot_general(q4, kk[g], (((1,), (1,)), ((), ())), preferred_element_type=F32)
        sc = sc + alibi_ref[g]
        if key_bias is not None:
            sc = sc + key_bias
        sink = sink_ref[g]
        m = jnp.maximum(jnp.max(sc, axis=1, keepdims=True), sink)
        p_un = jnp.exp(sc - m)
        den = jnp.sum(p_un, axis=1, keepdims=True) + jnp.exp(sink - m)
        o = jnp.dot(p_un.astype(BF16), vv[g], preferred_element_type=F32) / den
        for p in range(2):
            pairs.append(jnp.where(lo, o[(2 * p) * CHUNK:(2 * p + 1) * CHUNK],
                                   o[(2 * p + 1) * CHUNK:(2 * p + 2) * CHUNK]))
    return jnp.concatenate(pairs, axis=1)


def _sgu_weights(sguw_ref):
    sub_i = lax.broadcasted_iota(I32, (SGU_CHUNK, SGU_CHUNK), 0) // CHUNK
    sub_j = lax.broadcasted_iota(I32, (SGU_CHUNK, SGU_CHUNK), 1) // CHUNK
    keep = sub_j <= sub_i
    return [jnp.where(keep, sguw_ref[i], 0.0).astype(BF16) for i in range(SGU_HEADS)]


def _sgu_chunk(ws, gu_c, gvn_c, sgubt_ref):
    outs = []
    for i in range(SGU_HEADS):
        sl = slice(i * SGU_HEAD_DIM, (i + 1) * SGU_HEAD_DIM)
        sp = jnp.dot(ws[i], gvn_c[:, sl].astype(BF16), preferred_element_type=F32) + sgubt_ref[:, i:i + 1]
        outs.append(gu_c[:, sl] * sp)
    return jnp.concatenate(outs, axis=1)


def _merge_route(x, attn, sgu, tok_base, aog_ref, sog_ref, w_out_ref, n2g_ref, wr_ref, br_ref, cnt_scr,
                 x1_ref, h2_ref, idx_ref, rank_ref, gate_ref):
    n = x.shape[0]
    a_n = _rms(attn, aog_ref[...]).astype(BF16)
    s_n = _rms(sgu, sog_ref[...]).astype(BF16)
    x1 = (x + jnp.dot(a_n, w_out_ref[:ATTN_WIDTH, :], preferred_element_type=F32)
          + jnp.dot(s_n, w_out_ref[ATTN_WIDTH:, :], preferred_element_type=F32))
    x1_ref[...] = x1
    h2 = _rms(x1, n2g_ref[...])
    logits = lax.dot_general(wr_ref[...], h2, (((1,), (1,)), ((), ())),
                             precision=lax.Precision.HIGHEST, preferred_element_type=F32) + br_ref[...]
    e_iota = lax.broadcasted_iota(I32, (N_EXPERTS, n), 0).astype(F32)
    cur = logits
    vals, idxs, sels = [], [], []
    for _ in range(TOP_K):
        m = jnp.max(cur, axis=0, keepdims=True)
        idx = jnp.min(jnp.where(cur == m, e_iota, float(N_EXPERTS)), axis=0, keepdims=True)
        sel = e_iota == idx
        vals.append(m)
        idxs.append(idx)
        sels.append(sel)
        cur = jnp.where(sel, -jnp.inf, cur)
    exps = [jnp.exp(v - vals[0]) for v in vals]
    den = exps[0] + exps[1] + exps[2] + exps[3]
    gate_ref[...] = jnp.concatenate([e / den for e in exps], axis=0)
    onehot = jnp.where(sels[0] | sels[1] | sels[2] | sels[3], 1.0, 0.0)
    before = lax.broadcasted_iota(I32, (n, n), 0) < lax.broadcasted_iota(I32, (n, n), 1)
    upper = jnp.where(before, 1.0, 0.0).astype(BF16)
    prior = jnp.dot(onehot.astype(BF16), upper, preferred_element_type=F32) + cnt_scr[...]
    ranks = [jnp.sum(jnp.where(s, prior, 0.0), axis=0, keepdims=True) for s in sels]
    idx_ref[...] = jnp.concatenate(idxs, axis=0).astype(I32)
    rank_ref[...] = jnp.concatenate(ranks, axis=0).astype(I32)
    cnt_scr[...] = cnt_scr[...] + jnp.sum(onehot, axis=1, keepdims=True)

    words = lax.bitcast_convert_type(h2.astype(BF16).astype(F32), I32)
    tok = (lax.broadcasted_iota(I32, (1, n), 1) + tok_base).astype(F32)
    tag_t = jnp.concatenate([tok] + idxs + [jnp.zeros((LANES - 1 - TOP_K, n), F32)], axis=0)
    tagged = words[:, :LANES] | tag_t.T.astype(I32)
    _store_tile_rows(h2_ref, 0, jnp.concatenate([tagged, words[:, LANES:]], axis=1))


def _mixer_kernel(xp_ref, xs_in_ref, ck_ref, cv_ref, n1g_ref, w_in_ref, sgng_ref, sink_ref, alibi_ref, sguw_ref,
                  sgubt_ref, aog_ref, sog_ref, w_out_ref, n2g_ref, wr_ref, br_ref,
                  x1_ref, h2_ref, idx_ref, rank_ref, gate_ref, cnt_ref, kp_ref, vp_ref, ks_ref, vs_ref, gv_ref,
                  kk0, kk1, vv0, vv1, cnt_scr):
    n = pl.program_id(0)
    s = n % SEQ_BLOCKS
    weights = (aog_ref, sog_ref, w_out_ref, n2g_ref, wr_ref, br_ref)
    outs = (x1_ref, h2_ref, idx_ref, rank_ref, gate_ref)

    @pl.when(n == 0)
    def _():
        cnt_scr[...] = jnp.zeros_like(cnt_scr)

    @pl.when(n < PROMPT_BLOCKS)
    def _():
        @pl.when(s == 0)
        def _():
            zeros = jnp.zeros((WINDOW, 2 * HEAD_DIM), BF16)
            for buf in (kk0, kk1, vv0, vv1):
                buf[0:WINDOW, :] = zeros

        x = xp_ref[...]
        q, k, v, gu, gvn = _in_proj(x, n1g_ref, w_in_ref, sgng_ref)

        @pl.when(s == SEQ_BLOCKS - 1)
        def _():
            kp_ref[...] = k[TOKEN_BLOCK - WINDOW:, :]
            vp_ref[...] = v[TOKEN_BLOCK - WINDOW:, :]

        k0, k1 = _dup_heads(k)
        v0, v1 = _dup_heads(v)
        for buf, val in ((kk0, k0), (kk1, k1), (vv0, v0), (vv1, v1)):
            buf[WINDOW:, :] = val.astype(BF16)

        attn_chunks = []
        for j in range(TOKEN_BLOCK // CHUNK):
            band = slice(j * CHUNK, j * CHUNK + BAND)
            key_bias = None
            if j * CHUNK < WINDOW:
                key_pos = lax.broadcasted_iota(I32, (1, BAND), 1) + (s * TOKEN_BLOCK + j * CHUNK - WINDOW)
                key_bias = jnp.where(key_pos < 0, NEG_INF, 0.0)
            attn_chunks.append(_attend_chunk(q[j * CHUNK:(j + 1) * CHUNK], (kk0[band, :], kk1[band, :]),
                                             (vv0[band, :], vv1[band, :]), alibi_ref, sink_ref, key_bias))
        attn = jnp.concatenate(attn_chunks, axis=0)

        for buf in (kk0, kk1, vv0, vv1):
            buf[0:WINDOW, :] = buf[TOKEN_BLOCK:TOKEN_BLOCK + WINDOW, :]

        ws = _sgu_weights(sguw_ref)
        sgu = jnp.concatenate(
            [_sgu_chunk(ws, gu[c * SGU_CHUNK:(c + 1) * SGU_CHUNK], gvn[c * SGU_CHUNK:(c + 1) * SGU_CHUNK],
                        sgubt_ref) for c in range(TOKEN_BLOCK // SGU_CHUNK)], axis=0)
        _merge_route(x, attn, sgu, n * TOKEN_BLOCK, *weights, cnt_scr, *outs)

    @pl.when(n == PROMPT_BLOCKS)
    def _():
        x = xs_in_ref[...]
        q, k, v, gu, gvn = _in_proj(x, n1g_ref, w_in_ref, sgng_ref)
        gv_ref[...] = gvn
        ws = _sgu_weights(sguw_ref)
        attn_chunks, sgu_chunks = [], []
        pad = jnp.zeros((SGU_CHUNK - DEC_SEQ, SGU_WIDTH), F32)
        for b in range(DEC_BATCH):
            rows = slice(b * DEC_SEQ, (b + 1) * DEC_SEQ)
            k_all = jnp.concatenate([ck_ref[b], k[rows]], axis=0)
            v_all = jnp.concatenate([cv_ref[b], v[rows]], axis=0)
            ks_ref[b] = k_all[DEC_SEQ:]
            vs_ref[b] = v_all[DEC_SEQ:]
            k0, k1 = _dup_heads(k_all)
            v0, v1 = _dup_heads(v_all)
            attn_chunks.append(_attend_chunk(q[rows], (k0.astype(BF16), k1.astype(BF16)),
                                             (v0.astype(BF16), v1.astype(BF16)), alibi_ref, sink_ref, None))
            gated = _sgu_chunk(ws, jnp.concatenate([gu[rows], pad], axis=0),
                               jnp.concatenate([gvn[rows], pad], axis=0), sgubt_ref)
            sgu_chunks.append(gated[:DEC_SEQ])
        attn = jnp.concatenate(attn_chunks, axis=0)
        sgu = jnp.concatenate(sgu_chunks, axis=0)
        _merge_route(x, attn, sgu, n * TOKEN_BLOCK, *weights, cnt_scr, *outs)

    cnt_ref[...] = cnt_scr[...].astype(I32)


def _dispatch_kernel(meta_ref, pstart_ref, idx_ref, rank_ref, h2_ref, xs_ref,
                     zero_buf, dest_vmem, dest_smem, sem_rows, sem_idx, sem_fill):
    i = pl.program_id(0)
    block_words = EXPERT_ROWS * SUBLANES

    def fill_copy(first_row):
        start = pl.multiple_of(first_row * SUBLANES, SUBLANES)
        return pltpu.make_async_copy(zero_buf, xs_ref.at[pl.ds(start, block_words), :], sem_fill)

    @pl.when(i == 0)
    def _():
        zero_buf[...] = jnp.zeros_like(zero_buf)
        for e in range(N_EXPERTS):
            fill_copy(meta_ref[e]).start()
        for e in range(N_EXPERTS):
            fill_copy(meta_ref[e]).wait()
        for j in range(SORTED_BLOCKS - MIN_ROW_BLOCKS):
            @pl.when(meta_ref[N_EXPERTS] + j < SORTED_BLOCKS)
            def _():
                fill_copy((meta_ref[N_EXPERTS] + j) * EXPERT_ROWS).start()
        for j in range(SORTED_BLOCKS - MIN_ROW_BLOCKS):
            @pl.when(meta_ref[N_EXPERTS] + j < SORTED_BLOCKS)
            def _():
                fill_copy((meta_ref[N_EXPERTS] + j) * EXPERT_ROWS).wait()

    e_iota = lax.broadcasted_iota(I32, (N_EXPERTS, TOKEN_BLOCK), 0)
    pstart = pstart_ref[...]
    idx = idx_ref[...]
    dest = jnp.concatenate(
        [jnp.sum(jnp.where(e_iota == idx[kk:kk + 1, :], pstart, 0), axis=0, keepdims=True)
         for kk in range(TOP_K)], axis=0) + rank_ref[...]
    dest_vmem[...] = dest
    to_smem = pltpu.make_async_copy(dest_vmem, dest_smem, sem_idx)
    to_smem.start()
    to_smem.wait()

    def issue(t, carry):
        for kk in range(TOP_K):
            pltpu.make_async_copy(_tile_row(h2_ref, t), _tile_row(xs_ref, dest_smem[kk, t]),
                                  sem_rows).start(priority=kk % 2)
        return carry

    lax.fori_loop(0, TOKEN_BLOCK, issue, 0, unroll=ISSUE_UNROLL)
    for _ in range(TOP_K):
        pltpu.make_async_copy(h2_ref, xs_ref.at[pl.ds(0, TOKEN_BLOCK * SUBLANES), :], sem_rows).wait()


def _expert_kernel(bstart_ref, nrow_ref, xs_ref, wgu_ref, bgu_ref, wd_ref, bd_ref, out_ref,
                   wgu_bf, wd_bf, xbuf, stage, ids_vmem, ids_smem, sem_x, sem_rows, sem_ids):
    expert = pl.program_id(0)
    first_block = bstart_ref[expert]
    end_block = bstart_ref[expert + 1]
    nv = bstart_ref[N_EXPERTS]
    n_phase = D_FF // FF_TILE
    rows_per_phase = EXPERT_ROWS // n_phase
    block_words = EXPERT_ROWS * SUBLANES

    def x_copy(g, s):
        start = pl.multiple_of(g * block_words, block_words)
        return pltpu.make_async_copy(xs_ref.at[pl.ds(start, block_words), :], xbuf.at[s], sem_x.at[s])

    def ids_copy(s):
        return pltpu.make_async_copy(ids_vmem.at[s], ids_smem.at[s], sem_ids.at[s])

    def rows_wait(s):
        whole = pl.ds(0, block_words)
        pltpu.make_async_copy(stage.at[whole, :], out_ref.at[whole, :], sem_rows.at[s]).wait()

    def issue_rows(s, first, last):
        for r in range(first, last):
            pltpu.make_async_copy(_tile_row(stage, s * EXPERT_ROWS + r), _tile_row(out_ref, ids_smem[s, 0, r]),
                                  sem_rows.at[s]).start(priority=r % 2)

    def compute(g, s, prev):
        words = _load_tile_rows(xbuf.at[s], EXPERT_ROWS)
        tag = words[:, :LANES] & TAG_MASK
        x = lax.bitcast_convert_type(words & ~TAG_MASK, F32).astype(BF16)
        ids_copy(prev).wait()
        issue_rows(prev, 0, EXPERT_ROWS)
        gu = jnp.dot(x, wgu_bf[...], preferred_element_type=F32) + bgu_ref[...]
        x_glu = jnp.minimum(gu[:, :D_FF], SWIGLU_LIMIT)
        x_lin = jnp.clip(gu[:, D_FF:], -SWIGLU_LIMIT, SWIGLU_LIMIT)
        act = x_glu * (1.0 / (1.0 + jnp.exp(-SWIGLU_ALPHA * x_glu))) * (x_lin + 1.0)
        out = jnp.dot(act.astype(BF16), wd_bf[...], preferred_element_type=F32) + bd_ref[...]
        _store_tile_rows(stage, s * EXPERT_ROWS, out)
        k_slot = sum(j * jnp.where(tag[:, 1 + j:2 + j] == expert, 1, 0) for j in range(1, TOP_K))
        row = lax.broadcasted_iota(I32, (EXPERT_ROWS, 1), 0)
        dst = jnp.where(row < nrow_ref[g], k_slot * T_ALL + tag[:, 0:1], DUMP_ROW + row)
        dst_t = jnp.broadcast_to(dst.astype(F32), (EXPERT_ROWS, LANES)).T
        ids_vmem[s] = dst_t[0:SUBLANES].astype(I32)
        ids_copy(s).start()

    @pl.when(expert == 0)
    def _():
        stage[...] = jnp.zeros_like(stage)
        pltpu.make_async_copy(stage.at[pl.ds(0, block_words), :],
                              out_ref.at[pl.ds((DUMP_ROW + EXPERT_ROWS) * SUBLANES, block_words), :],
                              sem_rows.at[0]).start()
        ids_vmem[1] = DUMP_ROW + lax.broadcasted_iota(I32, (SUBLANES, EXPERT_ROWS), 1)
        ids_copy(1).start()
        x_copy(0, 0).start()

    @pl.when(end_block > first_block)
    def _():
        rows = 128

        def cast_rows(c, carry):
            r = pl.multiple_of(c * rows, rows)
            wgu_bf[pl.ds(r, rows), :] = wgu_ref[pl.ds(r, rows), :].astype(BF16)
            wd_bf[pl.ds(r, rows), :] = wd_ref[pl.ds(r, rows), :].astype(BF16)
            return carry

        lax.fori_loop(0, D_MODEL // rows, cast_rows, 0)

    def block_step(g, carry):
        slot = g % 2
        x_copy(g, slot).wait()
        x_copy(jnp.minimum(g + 1, nv - 1), 1 - slot).start()
        rows_wait(slot)
        compute(g, slot, 1 - slot)
        return carry

    lax.fori_loop(first_block, end_block, block_step, 0)

    @pl.when(expert == N_EXPERTS - 1)
    def _():
        last = (nv - 1) % 2
        x_copy(nv - 1, 1 - last).wait()
        ids_copy(last).wait()
        issue_rows(last, 0, EXPERT_ROWS)
        rows_wait(1 - last)
        rows_wait(last)


def _combine_kernel(o0_ref, o1_ref, o2_ref, o3_ref, gate_ref, x1_ref, fg_ref, yp_ref, ys_ref):
    i = pl.program_id(0)
    gates = gate_ref[...]
    y = x1_ref[...]
    for kk, o_ref in enumerate((o0_ref, o1_ref, o2_ref, o3_ref)):
        y = y + gates[:, kk:kk + 1] * _load_tile_rows(o_ref, TOKEN_BLOCK)
    out = _rms(y, fg_ref[...])

    @pl.when(i < PROMPT_BLOCKS)
    def _():
        yp_ref[...] = out

    @pl.when(i >= PROMPT_BLOCKS)
    def _():
        ys_ref[...] = out


def _full(shape):
    return pl.BlockSpec(shape, lambda *_: (0,) * len(shape))


def kernel(x_prompt, x_sample, cache_k, cache_v, norm1_g, w_in, sgu_norm_g, attn_sinks, sgu_w, sgu_b, attn_out_g,
           sgu_out_g, w_out, norm2_g, w_router, b_router, w_gate_up, b_gate_up, w_down, b_down, final_g):
    n1g = norm1_g[0].reshape(1, D_MODEL)
    w_in_b = w_in[0].astype(BF16)
    sgng = sgu_norm_g[0].reshape(1, SGU_WIDTH)
    sink_cols = jnp.repeat(attn_sinks[0].reshape(N_KV_HEADS, Q_PER_KV, 1), CHUNK, axis=1).reshape(
        N_KV_HEADS, Q_PER_KV * CHUNK, 1)
    slopes = jnp.exp2(-8.0 * jnp.arange(1, N_Q_HEADS + 1, dtype=F32) / N_Q_HEADS).reshape(N_KV_HEADS, Q_PER_KV)
    dist = jnp.abs(jnp.arange(CHUNK)[:, None] + WINDOW - jnp.arange(BAND)[None, :]).astype(F32)
    alibi = (-slopes[:, :, None, None] * dist).reshape(N_KV_HEADS, Q_PER_KV * CHUNK, BAND)
    sguw = sgu_w[0]
    sgubt = sgu_b[0].T
    aog = attn_out_g[0].reshape(1, ATTN_WIDTH)
    sog = sgu_out_g[0].reshape(1, SGU_WIDTH)
    w_out_b = w_out[0].astype(BF16)
    n2g = norm2_g[0].reshape(1, D_MODEL)
    wr_t = w_router[0].T
    br_c = b_router[0].reshape(N_EXPERTS, 1)
    fg = final_g.reshape(1, D_MODEL)

    weight_specs = [
        _full((1, D_MODEL)), _full((D_MODEL, IN_COLS)), _full((1, SGU_WIDTH)),
        _full((N_KV_HEADS, Q_PER_KV * CHUNK, 1)), _full((N_KV_HEADS, Q_PER_KV * CHUNK, BAND)),
        _full((SGU_HEADS, SGU_CHUNK, SGU_CHUNK)), _full((SGU_CHUNK, SGU_HEADS)),
        _full((1, ATTN_WIDTH)), _full((1, SGU_WIDTH)), _full((D_MODEL, D_MODEL)), _full((1, D_MODEL)),
        _full((N_EXPERTS, D_MODEL)), _full((N_EXPERTS, 1)),
    ]
    weights = (n1g, w_in_b, sgng, sink_cols, alibi, sguw, sgubt, aog, sog, w_out_b, n2g, wr_t, br_c)
    cparams = functools.partial(pltpu.CompilerParams, vmem_limit_bytes=VMEM_LIMIT_BYTES)

    def tok_map(n):
        return (n, 0)

    def tok_map_t(n):
        return (0, n)

    def stream_map(n):
        return (jnp.minimum(n // SEQ_BLOCKS, BATCH - 1), 0, 0)

    kv_cache = (DEC_BATCH, WINDOW, KV_WIDTH)
    bf_band = pltpu.VMEM((WINDOW + TOKEN_BLOCK, 2 * HEAD_DIM), BF16)
    (x1_all, h2_all, idx_all, rank_all, gate_all, cnt_all, kp_new, vp_new, ks_new, vs_new, gv_s) = pl.pallas_call(
        _mixer_kernel,
        grid=(N_TOKEN_BLOCKS,),
        in_specs=[pl.BlockSpec((TOKEN_BLOCK, D_MODEL), lambda n: (jnp.minimum(n, PROMPT_BLOCKS - 1), 0)),
                  _full((T_SAMPLE, D_MODEL)), _full(kv_cache), _full(kv_cache)] + weight_specs,
        out_specs=[
            pl.BlockSpec((TOKEN_BLOCK, D_MODEL), tok_map),
            pl.BlockSpec((TOKEN_BLOCK * SUBLANES, LANES), tok_map),
            pl.BlockSpec((TOP_K, TOKEN_BLOCK), tok_map_t), pl.BlockSpec((TOP_K, TOKEN_BLOCK), tok_map_t),
            pl.BlockSpec((TOP_K, TOKEN_BLOCK), tok_map_t),
            _full((N_EXPERTS, 1)),
            pl.BlockSpec((None, WINDOW, KV_WIDTH), stream_map), pl.BlockSpec((None, WINDOW, KV_WIDTH), stream_map),
            _full(kv_cache), _full(kv_cache), _full((T_SAMPLE, SGU_WIDTH)),
        ],
        out_shape=(
            jax.ShapeDtypeStruct((T_ALL, D_MODEL), F32),
            jax.ShapeDtypeStruct((T_ALL * SUBLANES, LANES), I32),
            jax.ShapeDtypeStruct((TOP_K, T_ALL), I32),
            jax.ShapeDtypeStruct((TOP_K, T_ALL), I32),
            jax.ShapeDtypeStruct((TOP_K, T_ALL), F32),
            jax.ShapeDtypeStruct((N_EXPERTS, 1), I32),
            jax.ShapeDtypeStruct((BATCH, WINDOW, KV_WIDTH), F32),
            jax.ShapeDtypeStruct((BATCH, WINDOW, KV_WIDTH), F32),
            jax.ShapeDtypeStruct(kv_cache, F32),
            jax.ShapeDtypeStruct(kv_cache, F32),
            jax.ShapeDtypeStruct((T_SAMPLE, SGU_WIDTH), F32),
        ),
        scratch_shapes=[bf_band, bf_band, bf_band, bf_band, pltpu.VMEM((N_EXPERTS, 1), F32)],
        compiler_params=cparams(dimension_semantics=("arbitrary",)),
        name="mixer",
    )(x_prompt.reshape(T_PROMPT, D_MODEL), x_sample.reshape(T_SAMPLE, D_MODEL),
      cache_k[0].reshape(kv_cache), cache_v[0].reshape(kv_cache), *weights)

    counts = cnt_all[:, 0]
    blocks_per = (counts + EXPERT_ROWS - 1) // EXPERT_ROWS
    block_end = jnp.cumsum(blocks_per)
    pstart = (block_end - blocks_per) * EXPERT_ROWS
    n_valid = block_end[-1:].astype(I32)
    step = jnp.minimum(jnp.arange(N_ROW_BLOCKS, dtype=I32), n_valid - 1)
    expert_of_step = step[:, None] >= block_end[None, :]
    blk_e = jnp.minimum(jnp.sum(expert_of_step, axis=1), N_EXPERTS - 1).astype(I32)
    own = (jnp.arange(N_EXPERTS, dtype=I32)[None, :] == blk_e[:, None]).astype(I32)
    rows_left = jnp.sum(own * (counts + pstart)[None, :], axis=1) - step * EXPERT_ROWS
    blk_rows = jnp.clip(rows_left, 0, EXPERT_ROWS).astype(I32)
    dispatch_meta = jnp.concatenate([(pstart + counts).astype(I32), n_valid])

    xs = pl.pallas_call(
        _dispatch_kernel,
        grid_spec=pltpu.PrefetchScalarGridSpec(
            num_scalar_prefetch=1,
            grid=(N_TOKEN_BLOCKS,),
            in_specs=[
                pl.BlockSpec((N_EXPERTS, 1), lambda i, m: (0, 0)),
                pl.BlockSpec((TOP_K, TOKEN_BLOCK), lambda i, m: (0, i)),
                pl.BlockSpec((TOP_K, TOKEN_BLOCK), lambda i, m: (0, i)),
                pl.BlockSpec((TOKEN_BLOCK * SUBLANES, LANES), lambda i, m: (i, 0)),
            ],
            out_specs=pl.BlockSpec(memory_space=pl.ANY),
            scratch_shapes=[
                pltpu.VMEM((EXPERT_ROWS * SUBLANES, LANES), I32),
                pltpu.VMEM((TOP_K, TOKEN_BLOCK), I32),
                pltpu.SMEM((TOP_K, TOKEN_BLOCK), I32),
                pltpu.SemaphoreType.DMA, pltpu.SemaphoreType.DMA, pltpu.SemaphoreType.DMA,
            ],
        ),
        out_shape=jax.ShapeDtypeStruct((SORTED_ROWS * SUBLANES, LANES), I32),
        compiler_params=cparams(dimension_semantics=("arbitrary",)),
        name="dispatch",
    )(dispatch_meta, pstart.reshape(N_EXPERTS, 1).astype(I32), idx_all, rank_all, h2_all)

    def expert_map(e, bounds, nr):
        return (e, 0, 0)

    block_bounds = jnp.concatenate([jnp.zeros((1,), I32), block_end.astype(I32)])
    out_kt = pl.pallas_call(
        _expert_kernel,
        grid_spec=pltpu.PrefetchScalarGridSpec(
            num_scalar_prefetch=2,
            grid=(N_EXPERTS,),
            in_specs=[
                pl.BlockSpec(memory_space=pl.ANY),
                pl.BlockSpec((None, D_MODEL, 2 * D_FF), expert_map),
                pl.BlockSpec((None, 1, 2 * D_FF), expert_map),
                pl.BlockSpec((None, D_FF, D_MODEL), expert_map),
                pl.BlockSpec((None, 1, D_MODEL), expert_map),
            ],
            out_specs=pl.BlockSpec(memory_space=pl.ANY),
            scratch_shapes=[
                pltpu.VMEM((D_MODEL, 2 * D_FF), BF16), pltpu.VMEM((D_FF, D_MODEL), BF16),
                pltpu.VMEM((2, EXPERT_ROWS * SUBLANES, LANES), I32),
                pltpu.VMEM((2 * EXPERT_ROWS * SUBLANES, LANES), F32),
                pltpu.VMEM((2, SUBLANES, EXPERT_ROWS), I32),
                pltpu.SMEM((2, SUBLANES, EXPERT_ROWS), I32),
                pltpu.SemaphoreType.DMA((2,)), pltpu.SemaphoreType.DMA((2,)), pltpu.SemaphoreType.DMA((2,)),
            ],
        ),
        out_shape=jax.ShapeDtypeStruct((OUT_ROWS * SUBLANES, LANES), F32),
        compiler_params=cparams(dimension_semantics=("arbitrary",)),
        name="experts",
    )(block_bounds, blk_rows, xs, w_gate_up[0], b_gate_up[0].reshape(N_EXPERTS, 1, 2 * D_FF), w_down[0],
      b_down[0].reshape(N_EXPERTS, 1, D_MODEL))

    def slot_spec(kk):
        return pl.BlockSpec((TOKEN_BLOCK * SUBLANES, LANES), lambda i: (kk * N_TOKEN_BLOCKS + i, 0))

    y_p, y_s = pl.pallas_call(
        _combine_kernel,
        grid=(N_TOKEN_BLOCKS,),
        in_specs=[slot_spec(kk) for kk in range(TOP_K)] + [
            pl.BlockSpec((TOKEN_BLOCK, TOP_K), lambda i: (i, 0)),
            pl.BlockSpec((TOKEN_BLOCK, D_MODEL), lambda i: (i, 0)),
            _full((1, D_MODEL)),
        ],
        out_specs=[
            pl.BlockSpec((TOKEN_BLOCK, D_MODEL), lambda i: (jnp.minimum(i, PROMPT_BLOCKS - 1), 0)),
            pl.BlockSpec((TOKEN_BLOCK, D_MODEL), lambda i: (jnp.maximum(i - PROMPT_BLOCKS, 0), 0)),
        ],
        out_shape=(jax.ShapeDtypeStruct((T_PROMPT, D_MODEL), F32), jax.ShapeDtypeStruct((T_SAMPLE, D_MODEL), F32)),
        compiler_params=cparams(dimension_semantics=("arbitrary",)),
        name="combine",
    )(out_kt, out_kt, out_kt, out_kt, gate_all.T, x1_all, fg)

    kv5 = (1, -1, WINDOW, N_KV_HEADS, HEAD_DIM)
    return (y_p.reshape(BATCH, SEQ, D_MODEL), y_s.reshape(DEC_BATCH, DEC_SEQ, D_MODEL),
            kp_new.reshape(kv5), vp_new.reshape(kv5), ks_new.reshape(kv5), vs_new.reshape(kv5),
            gv_s.reshape(1, DEC_BATCH, DEC_SEQ, SGU_HEADS, SGU_HEAD_DIM))
```

```python
import functools

import jax
import jax.numpy as jnp
from jax import lax
from jax.experimental import pallas as pl
from jax.experimental.pallas import tpu as pltpu

D_MODEL = 1024
BATCH = 2
SEQ = 8192
DEC_BATCH = 8
DEC_SEQ = 64
CHUNK = 64
WINDOW = 128
BAND = WINDOW + CHUNK
HEAD_DIM = 64
N_Q_HEADS = 8
N_KV_HEADS = 2
Q_PER_KV = N_Q_HEADS // N_KV_HEADS
ATTN_WIDTH = N_Q_HEADS * HEAD_DIM
KV_WIDTH = N_KV_HEADS * HEAD_DIM
QKV_COLS = ATTN_WIDTH + 2 * KV_WIDTH
SGU_CHUNK = 128
SGU_HEADS = 4
SGU_HEAD_DIM = 128
SGU_WIDTH = SGU_HEADS * SGU_HEAD_DIM
IN_COLS = QKV_COLS + 2 * SGU_WIDTH
N_EXPERTS = 32
TOP_K = 4
D_FF = 1024
SWIGLU_LIMIT = 7.0
SWIGLU_ALPHA = 1.702
RMS_EPS = 1e-5
NEG_INF = -1e30

LANES = 128
SUBLANES = 8
assert D_MODEL == LANES * SUBLANES

T_PROMPT = BATCH * SEQ
T_SAMPLE = DEC_BATCH * DEC_SEQ
T_ALL = T_PROMPT + T_SAMPLE
TOKEN_BLOCK = 512
SEQ_BLOCKS = SEQ // TOKEN_BLOCK
N_TOKEN_BLOCKS = T_ALL // TOKEN_BLOCK
PROMPT_BLOCKS = T_PROMPT // TOKEN_BLOCK
EXPERT_ROWS = 256
N_SLOTS = T_ALL * TOP_K
MIN_ROW_BLOCKS = N_SLOTS // EXPERT_ROWS
N_ROW_BLOCKS = MIN_ROW_BLOCKS + N_EXPERTS
SORTED_BLOCKS = N_ROW_BLOCKS + 1
SORTED_ROWS = SORTED_BLOCKS * EXPERT_ROWS
DUMP_ROW = N_SLOTS
STAGES = 4
OUT_ROWS = DUMP_ROW + STAGES * EXPERT_ROWS
TAG_MASK = 0xFFFF
VMEM_LIMIT_BYTES = 56 * 1024 * 1024
ISSUE_UNROLL = 8

assert T_SAMPLE == TOKEN_BLOCK and SEQ % TOKEN_BLOCK == 0 and N_SLOTS % EXPERT_ROWS == 0
assert T_ALL <= TAG_MASK
F32 = jnp.float32
BF16 = jnp.bfloat16
I32 = jnp.int32


def _rms(x, g):
    return x * lax.rsqrt(jnp.mean(x * x, axis=-1, keepdims=True) + RMS_EPS) * g


def _gelu(x):
    return 0.5 * x * (1.0 + lax.erf(x * 0.7071067811865476))


def _store_tile_rows(ref, first_row, val):
    n = val.shape[0]
    for c in range(SUBLANES):
        ref[pl.ds(first_row * SUBLANES + c, n, stride=SUBLANES), :] = val[:, c * LANES:(c + 1) * LANES]


def _load_tile_rows(ref, n):
    return jnp.concatenate([ref[pl.ds(c, n, stride=SUBLANES), :] for c in range(SUBLANES)], axis=1)


def _tile_row(ref, row):
    return ref.at[pl.ds(pl.multiple_of(row * SUBLANES, SUBLANES), SUBLANES), :]


def _in_proj(x, n1g_ref, w_in_ref, sgng_ref):
    h = _rms(x, n1g_ref[...])
    cols = jnp.dot(h.astype(BF16), w_in_ref[...], preferred_element_type=F32)
    q = cols[:, :ATTN_WIDTH] * (HEAD_DIM ** -0.5)
    k = cols[:, ATTN_WIDTH:ATTN_WIDTH + KV_WIDTH]
    v = cols[:, ATTN_WIDTH + KV_WIDTH:QKV_COLS]
    gu = _gelu(cols[:, QKV_COLS:QKV_COLS + SGU_WIDTH])
    gv = _gelu(cols[:, QKV_COLS + SGU_WIDTH:])
    sg = sgng_ref[...]
    gvn = jnp.concatenate(
        [_rms(gv[:, i * SGU_HEAD_DIM:(i + 1) * SGU_HEAD_DIM], sg[:, i * SGU_HEAD_DIM:(i + 1) * SGU_HEAD_DIM])
         for i in range(SGU_HEADS)], axis=1)
    return q, k, v, gu, gvn


def _dup_heads(t):
    lane = lax.broadcasted_iota(I32, t.shape, 1)
    r = pltpu.roll(t, HEAD_DIM, axis=1)
    return jnp.where(lane < HEAD_DIM, t, r), jnp.where(lane < HEAD_DIM, r, t)


def _attend_chunk(q64, kk, vv, alibi_ref, sink_ref, key_bias):
    lane = lax.broadcasted_iota(I32, (CHUNK, 2 * HEAD_DIM), 1)
    lo = lane < HEAD_DIM
    pairs = []
    for g in range(N_KV_HEADS):
        rows = []
        for p in range(2):
            qp = q64[:, (2 * g + p) * 128:(2 * g + p + 1) * 128]
            rows.append(jnp.where(lo, qp, 0.0))
            rows.append(jnp.where(lo, 0.0, qp))
        q4 = jnp.concatenate(rows, axis=0).astype(BF16)
        sc = lax.dot_general(q4, kk[g], (((1,), (1,)), ((), ())), preferred_element_type=F32)
        sc = sc + alibi_ref[g]
        if key_bias is not None:
            sc = sc + key_bias
        sink = sink_ref[g]
        m = jnp.maximum(jnp.max(sc, axis=1, keepdims=True), sink)
        p_un = jnp.exp(sc - m)
        den = jnp.sum(p_un, axis=1, keepdims=True) + jnp.exp(sink - m)
        o = jnp.dot(p_un.astype(BF16), vv[g], preferred_element_type=F32) / den
        for p in range(2):
            pairs.append(jnp.where(lo, o[(2 * p) * CHUNK:(2 * p + 1) * CHUNK],
                                   o[(2 * p + 1) * CHUNK:(2 * p + 2) * CHUNK]))
    return jnp.concatenate(pairs, axis=1)


def _sgu_weights(sguw_ref):
    sub_i = lax.broadcasted_iota(I32, (SGU_CHUNK, SGU_CHUNK), 0) // CHUNK
    sub_j = lax.broadcasted_iota(I32, (SGU_CHUNK, SGU_CHUNK), 1) // CHUNK
    keep = sub_j <= sub_i
    return [jnp.where(keep, sguw_ref[i], 0.0).astype(BF16) for i in range(SGU_HEADS)]


def _sgu_chunk(ws, gu_c, gvn_c, sgubt_ref):
    outs = []
    for i in range(SGU_HEADS):
        sl = slice(i * SGU_HEAD_DIM, (i + 1) * SGU_HEAD_DIM)
        sp = jnp.dot(ws[i], gvn_c[:, sl].astype(BF16), preferred_element_type=F32) + sgubt_ref[:, i:i + 1]
        outs.append(gu_c[:, sl] * sp)
    return jnp.concatenate(outs, axis=1)


def _merge_route(x, attn, sgu, tok_base, aog_ref, sog_ref, w_out_ref, n2g_ref, wr_ref, br_ref, cnt_scr,
                 x1_ref, h2_ref, idx_ref, rank_ref, gate_ref):
    n = x.shape[0]
    a_n = _rms(attn, aog_ref[...]).astype(BF16)
    s_n = _rms(sgu, sog_ref[...]).astype(BF16)
    x1 = (x + jnp.dot(a_n, w_out_ref[:ATTN_WIDTH, :], preferred_element_type=F32)
          + jnp.dot(s_n, w_out_ref[ATTN_WIDTH:, :], preferred_element_type=F32))
    x1_ref[...] = x1
    h2 = _rms(x1, n2g_ref[...])
    logits = lax.dot_general(wr_ref[...], h2, (((1,), (1,)), ((), ())),
                             precision=lax.Precision.HIGHEST, preferred_element_type=F32) + br_ref[...]
    e_iota = lax.broadcasted_iota(I32, (N_EXPERTS, n), 0).astype(F32)
    cur = logits
    vals, idxs, sels = [], [], []
    for _ in range(TOP_K):
        m = jnp.max(cur, axis=0, keepdims=True)
        idx = jnp.min(jnp.where(cur == m, e_iota, float(N_EXPERTS)), axis=0, keepdims=True)
        sel = e_iota == idx
        vals.append(m)
        idxs.append(idx)
        sels.append(sel)
        cur = jnp.where(sel, -jnp.inf, cur)
    exps = [jnp.exp(v - vals[0]) for v in vals]
    den = exps[0] + exps[1] + exps[2] + exps[3]
    gate_ref[...] = jnp.concatenate([e / den for e in exps], axis=0)
    onehot = jnp.where(sels[0] | sels[1] | sels[2] | sels[3], 1.0, 0.0)
    before = lax.broadcasted_iota(I32, (n, n), 0) < lax.broadcasted_iota(I32, (n, n), 1)
    upper = jnp.where(before, 1.0, 0.0).astype(BF16)
    prior = jnp.dot(onehot.astype(BF16), upper, preferred_element_type=F32) + cnt_scr[...]
    ranks = [jnp.sum(jnp.where(s, prior, 0.0), axis=0, keepdims=True) for s in sels]
    idx_ref[...] = jnp.concatenate(idxs, axis=0).astype(I32)
    rank_ref[...] = jnp.concatenate(ranks, axis=0).astype(I32)
    cnt_scr[...] = cnt_scr[...] + jnp.sum(onehot, axis=1, keepdims=True)

    words = lax.bitcast_convert_type(h2.astype(BF16).astype(F32), I32)
    tok = (lax.broadcasted_iota(I32, (1, n), 1) + tok_base).astype(F32)
    tag_t = jnp.concatenate([tok] + idxs + [jnp.zeros((LANES - 1 - TOP_K, n), F32)], axis=0)
    tagged = words[:, :LANES] | tag_t.T.astype(I32)
    _store_tile_rows(h2_ref, 0, jnp.concatenate([tagged, words[:, LANES:]], axis=1))


def _mixer_kernel(xp_ref, xs_in_ref, ck_ref, cv_ref, n1g_ref, w_in_ref, sgng_ref, sink_ref, alibi_ref, sguw_ref,
                  sgubt_ref, aog_ref, sog_ref, w_out_ref, n2g_ref, wr_ref, br_ref,
                  x1_ref, h2_ref, idx_ref, rank_ref, gate_ref, cnt_ref, kp_ref, vp_ref, ks_ref, vs_ref, gv_ref,
                  kk0, kk1, vv0, vv1, cnt_scr):
    n = pl.program_id(0)
    s = n % SEQ_BLOCKS
    weights = (aog_ref, sog_ref, w_out_ref, n2g_ref, wr_ref, br_ref)
    outs = (x1_ref, h2_ref, idx_ref, rank_ref, gate_ref)

    @pl.when(n == 0)
    def _():
        cnt_scr[...] = jnp.zeros_like(cnt_scr)

    @pl.when(n < PROMPT_BLOCKS)
    def _():
        @pl.when(s == 0)
        def _():
            zeros = jnp.zeros((WINDOW, 2 * HEAD_DIM), BF16)
            for buf in (kk0, kk1, vv0, vv1):
                buf[0:WINDOW, :] = zeros

        x = xp_ref[...]
        q, k, v, gu, gvn = _in_proj(x, n1g_ref, w_in_ref, sgng_ref)

        @pl.when(s == SEQ_BLOCKS - 1)
        def _():
            kp_ref[...] = k[TOKEN_BLOCK - WINDOW:, :]
            vp_ref[...] = v[TOKEN_BLOCK - WINDOW:, :]

        k0, k1 = _dup_heads(k)
        v0, v1 = _dup_heads(v)
        for buf, val in ((kk0, k0), (kk1, k1), (vv0, v0), (vv1, v1)):
            buf[WINDOW:, :] = val.astype(BF16)

        attn_chunks = []
        for j in range(TOKEN_BLOCK // CHUNK):
            band = slice(j * CHUNK, j * CHUNK + BAND)
            key_bias = None
            if j * CHUNK < WINDOW:
                key_pos = lax.broadcasted_iota(I32, (1, BAND), 1) + (s * TOKEN_BLOCK + j * CHUNK - WINDOW)
                key_bias = jnp.where(key_pos < 0, NEG_INF, 0.0)
            attn_chunks.append(_attend_chunk(q[j * CHUNK:(j + 1) * CHUNK], (kk0[band, :], kk1[band, :]),
                                             (vv0[band, :], vv1[band, :]), alibi_ref, sink_ref, key_bias))
        attn = jnp.concatenate(attn_chunks, axis=0)

        for buf in (kk0, kk1, vv0, vv1):
            buf[0:WINDOW, :] = buf[TOKEN_BLOCK:TOKEN_BLOCK + WINDOW, :]

        ws = _sgu_weights(sguw_ref)
        sgu = jnp.concatenate(
            [_sgu_chunk(ws, gu[c * SGU_CHUNK:(c + 1) * SGU_CHUNK], gvn[c * SGU_CHUNK:(c + 1) * SGU_CHUNK],
                        sgubt_ref) for c in range(TOKEN_BLOCK // SGU_CHUNK)], axis=0)
        _merge_route(x, attn, sgu, n * TOKEN_BLOCK, *weights, cnt_scr, *outs)

    @pl.when(n == PROMPT_BLOCKS)
    def _():
        x = xs_in_ref[...]
        q, k, v, gu, gvn = _in_proj(x, n1g_ref, w_in_ref, sgng_ref)
        gv_ref[...] = gvn
        ws = _sgu_weights(sguw_ref)
        attn_chunks, sgu_chunks = [], []
        pad = jnp.zeros((SGU_CHUNK - DEC_SEQ, SGU_WIDTH), F32)
        for b in range(DEC_BATCH):
            rows = slice(b * DEC_SEQ, (b + 1) * DEC_SEQ)
            k_all = jnp.concatenate([ck_ref[b], k[rows]], axis=0)
            v_all = jnp.concatenate([cv_ref[b], v[rows]], axis=0)
            ks_ref[b] = k_all[DEC_SEQ:]
            vs_ref[b] = v_all[DEC_SEQ:]
            k0, k1 = _dup_heads(k_all)
            v0, v1 = _dup_heads(v_all)
            attn_chunks.append(_attend_chunk(q[rows], (k0.astype(BF16), k1.astype(BF16)),
                                             (v0.astype(BF16), v1.astype(BF16)), alibi_ref, sink_ref, None))
            gated = _sgu_chunk(ws, jnp.concatenate([gu[rows], pad], axis=0),
                               jnp.concatenate([gvn[rows], pad], axis=0), sgubt_ref)
            sgu_chunks.append(gated[:DEC_SEQ])
        attn = jnp.concatenate(attn_chunks, axis=0)
        sgu = jnp.concatenate(sgu_chunks, axis=0)
        _merge_route(x, attn, sgu, n * TOKEN_BLOCK, *weights, cnt_scr, *outs)

    cnt_ref[...] = cnt_scr[...].astype(I32)


def _dispatch_kernel(meta_ref, pstart_ref, idx_ref, rank_ref, h2_ref, xs_ref,
                     zero_buf, dest_vmem, dest_smem, sem_rows, sem_idx, sem_fill):
    i = pl.program_id(0)
    block_words = EXPERT_ROWS * SUBLANES

    def fill_copy(first_row):
        start = pl.multiple_of(first_row * SUBLANES, SUBLANES)
        return pltpu.make_async_copy(zero_buf, xs_ref.at[pl.ds(start, block_words), :], sem_fill)

    @pl.when(i == 0)
    def _():
        zero_buf[...] = jnp.zeros_like(zero_buf)
        for e in range(N_EXPERTS):
            fill_copy(meta_ref[e]).start()
        for e in range(N_EXPERTS):
            fill_copy(meta_ref[e]).wait()
        for j in range(SORTED_BLOCKS - MIN_ROW_BLOCKS):
            @pl.when(meta_ref[N_EXPERTS] + j < SORTED_BLOCKS)
            def _():
                fill_copy((meta_ref[N_EXPERTS] + j) * EXPERT_ROWS).start()
        for j in range(SORTED_BLOCKS - MIN_ROW_BLOCKS):
            @pl.when(meta_ref[N_EXPERTS] + j < SORTED_BLOCKS)
            def _():
                fill_copy((meta_ref[N_EXPERTS] + j) * EXPERT_ROWS).wait()

    e_iota = lax.broadcasted_iota(I32, (N_EXPERTS, TOKEN_BLOCK), 0)
    pstart = pstart_ref[...]
    idx = idx_ref[...]
    dest = jnp.concatenate(
        [jnp.sum(jnp.where(e_iota == idx[kk:kk + 1, :], pstart, 0), axis=0, keepdims=True)
         for kk in range(TOP_K)], axis=0) + rank_ref[...]
    dest_vmem[...] = dest
    to_smem = pltpu.make_async_copy(dest_vmem, dest_smem, sem_idx)
    to_smem.start()
    to_smem.wait()

    def issue(t, carry):
        for kk in range(TOP_K):
            pltpu.make_async_copy(_tile_row(h2_ref, t), _tile_row(xs_ref, dest_smem[kk, t]),
                                  sem_rows).start(priority=kk % 2)
        return carry

    lax.fori_loop(0, TOKEN_BLOCK, issue, 0, unroll=ISSUE_UNROLL)
    for _ in range(TOP_K):
        pltpu.make_async_copy(h2_ref, xs_ref.at[pl.ds(0, TOKEN_BLOCK * SUBLANES), :], sem_rows).wait()


def _expert_kernel(bstart_ref, nrow_ref, xs_ref, wgu_ref, bgu_ref, wd_ref, bd_ref, out_ref,
                   wgu_bf, wd_bf, xbuf, stage, ids_vmem, ids_smem, sem_x, sem_rows, sem_ids):
    expert = pl.program_id(0)
    first_block = bstart_ref[expert]
    end_block = bstart_ref[expert + 1]
    nv = bstart_ref[N_EXPERTS]
    block_words = EXPERT_ROWS * SUBLANES

    def x_copy(g, s):
        start = pl.multiple_of(g * block_words, block_words)
        return pltpu.make_async_copy(xs_ref.at[pl.ds(start, block_words), :], xbuf.at[s], sem_x.at[s])

    def dump_row(s):
        return DUMP_ROW + s * EXPERT_ROWS

    def ids_copy(s):
        return pltpu.make_async_copy(ids_vmem.at[s], ids_smem.at[s], sem_ids.at[s])

    def rows_wait(s):
        whole = pl.ds(0, block_words)
        pltpu.make_async_copy(stage.at[whole, :], out_ref.at[whole, :], sem_rows.at[s]).wait()

    def issue_rows(s):
        ids_copy(s).wait()
        for r in range(EXPERT_ROWS):
            pltpu.make_async_copy(_tile_row(stage, s * EXPERT_ROWS + r), _tile_row(out_ref, ids_smem[s, 0, r]),
                                  sem_rows.at[s]).start(priority=r % 2)

    def compute(g, xs_slot, s):
        words = _load_tile_rows(xbuf.at[xs_slot], EXPERT_ROWS)
        tag = words[:, :LANES] & TAG_MASK
        x = lax.bitcast_convert_type(words & ~TAG_MASK, F32).astype(BF16)
        gu = jnp.dot(x, wgu_bf[...], preferred_element_type=F32) + bgu_ref[...]
        x_glu = jnp.minimum(gu[:, :D_FF], SWIGLU_LIMIT)
        x_lin = jnp.clip(gu[:, D_FF:], -SWIGLU_LIMIT, SWIGLU_LIMIT)
        act = x_glu * (1.0 / (1.0 + jnp.exp(-SWIGLU_ALPHA * x_glu))) * (x_lin + 1.0)
        out = jnp.dot(act.astype(BF16), wd_bf[...], preferred_element_type=F32) + bd_ref[...]
        _store_tile_rows(stage, s * EXPERT_ROWS, out)
        k_slot = sum(j * jnp.where(tag[:, 1 + j:2 + j] == expert, 1, 0) for j in range(1, TOP_K))
        row = lax.broadcasted_iota(I32, (EXPERT_ROWS, 1), 0)
        dst = jnp.where(row < nrow_ref[g], k_slot * T_ALL + tag[:, 0:1], dump_row(s) + row)
        dst_t = jnp.broadcast_to(dst.astype(F32), (EXPERT_ROWS, LANES)).T
        ids_vmem[s] = dst_t[0:SUBLANES].astype(I32)
        ids_copy(s).start()

    @pl.when(expert == 0)
    def _():
        stage[...] = jnp.zeros_like(stage)
        for s in range(STAGES):
            if s < STAGES - 2:
                pltpu.make_async_copy(stage.at[pl.ds(s * block_words, block_words), :],
                                      out_ref.at[pl.ds(dump_row(s) * SUBLANES, block_words), :],
                                      sem_rows.at[s]).start()
            else:
                ids_vmem[s] = dump_row(s) + lax.broadcasted_iota(I32, (SUBLANES, EXPERT_ROWS), 1)
                ids_copy(s).start()
        x_copy(0, 0).start()

    @pl.when(end_block > first_block)
    def _():
        rows = 128

        def cast_rows(c, carry):
            r = pl.multiple_of(c * rows, rows)
            wgu_bf[pl.ds(r, rows), :] = wgu_ref[pl.ds(r, rows), :].astype(BF16)
            wd_bf[pl.ds(r, rows), :] = wd_ref[pl.ds(r, rows), :].astype(BF16)
            return carry

        lax.fori_loop(0, D_MODEL // rows, cast_rows, 0)

    def block_step(g, carry):
        xs_slot = g % 2
        slot = g % STAGES
        x_copy(g, xs_slot).wait()
        x_copy(jnp.minimum(g + 1, nv - 1), 1 - xs_slot).start()
        rows_wait(slot)
        issue_rows((g + STAGES - 2) % STAGES)
        compute(g, xs_slot, slot)
        return carry

    lax.fori_loop(first_block, end_block, block_step, 0)

    @pl.when(expert == N_EXPERTS - 1)
    def _():
        x_copy(nv - 1, nv % 2).wait()
        issue_rows((nv + STAGES - 2) % STAGES)
        issue_rows((nv + STAGES - 1) % STAGES)
        for s in range(STAGES):
            rows_wait(s)


def _combine_kernel(o0_ref, o1_ref, o2_ref, o3_ref, gate_ref, x1_ref, fg_ref, yp_ref, ys_ref):
    i = pl.program_id(0)
    gates = gate_ref[...]
    y = x1_ref[...]
    for kk, o_ref in enumerate((o0_ref, o1_ref, o2_ref, o3_ref)):
        y = y + gates[:, kk:kk + 1] * _load_tile_rows(o_ref, TOKEN_BLOCK)
    out = _rms(y, fg_ref[...])

    @pl.when(i < PROMPT_BLOCKS)
    def _():
        yp_ref[...] = out

    @pl.when(i >= PROMPT_BLOCKS)
    def _():
        ys_ref[...] = out


def _full(shape):
    return pl.BlockSpec(shape, lambda *_: (0,) * len(shape))


def kernel(x_prompt, x_sample, cache_k, cache_v, norm1_g, w_in, sgu_norm_g, attn_sinks, sgu_w, sgu_b, attn_out_g,
           sgu_out_g, w_out, norm2_g, w_router, b_router, w_gate_up, b_gate_up, w_down, b_down, final_g):
    n1g = norm1_g[0].reshape(1, D_MODEL)
    w_in_b = w_in[0].astype(BF16)
    sgng = sgu_norm_g[0].reshape(1, SGU_WIDTH)
    sink_cols = jnp.repeat(attn_sinks[0].reshape(N_KV_HEADS, Q_PER_KV, 1), CHUNK, axis=1).reshape(
        N_KV_HEADS, Q_PER_KV * CHUNK, 1)
    slopes = jnp.exp2(-8.0 * jnp.arange(1, N_Q_HEADS + 1, dtype=F32) / N_Q_HEADS).reshape(N_KV_HEADS, Q_PER_KV)
    dist = jnp.abs(jnp.arange(CHUNK)[:, None] + WINDOW - jnp.arange(BAND)[None, :]).astype(F32)
    alibi = (-slopes[:, :, None, None] * dist).reshape(N_KV_HEADS, Q_PER_KV * CHUNK, BAND)
    sguw = sgu_w[0]
    sgubt = sgu_b[0].T
    aog = attn_out_g[0].reshape(1, ATTN_WIDTH)
    sog = sgu_out_g[0].reshape(1, SGU_WIDTH)
    w_out_b = w_out[0].astype(BF16)
    n2g = norm2_g[0].reshape(1, D_MODEL)
    wr_t = w_router[0].T
    br_c = b_router[0].reshape(N_EXPERTS, 1)
    fg = final_g.reshape(1, D_MODEL)

    weight_specs = [
        _full((1, D_MODEL)), _full((D_MODEL, IN_COLS)), _full((1, SGU_WIDTH)),
        _full((N_KV_HEADS, Q_PER_KV * CHUNK, 1)), _full((N_KV_HEADS, Q_PER_KV * CHUNK, BAND)),
        _full((SGU_HEADS, SGU_CHUNK, SGU_CHUNK)), _full((SGU_CHUNK, SGU_HEADS)),
        _full((1, ATTN_WIDTH)), _full((1, SGU_WIDTH)), _full((D_MODEL, D_MODEL)), _full((1, D_MODEL)),
        _full((N_EXPERTS, D_MODEL)), _full((N_EXPERTS, 1)),
    ]
    weights = (n1g, w_in_b, sgng, sink_cols, alibi, sguw, sgubt, aog, sog, w_out_b, n2g, wr_t, br_c)
    cparams = functools.partial(pltpu.CompilerParams, vmem_limit_bytes=VMEM_LIMIT_BYTES)

    def tok_map(n):
        return (n, 0)

    def tok_map_t(n):
        return (0, n)

    def stream_map(n):
        return (jnp.minimum(n // SEQ_BLOCKS, BATCH - 1), 0, 0)

    kv_cache = (DEC_BATCH, WINDOW, KV_WIDTH)
    bf_band = pltpu.VMEM((WINDOW + TOKEN_BLOCK, 2 * HEAD_DIM), BF16)
    (x1_all, h2_all, idx_all, rank_all, gate_all, cnt_all, kp_new, vp_new, ks_new, vs_new, gv_s) = pl.pallas_call(
        _mixer_kernel,
        grid=(N_TOKEN_BLOCKS,),
        in_specs=[pl.BlockSpec((TOKEN_BLOCK, D_MODEL), lambda n: (jnp.minimum(n, PROMPT_BLOCKS - 1), 0)),
                  _full((T_SAMPLE, D_MODEL)), _full(kv_cache), _full(kv_cache)] + weight_specs,
        out_specs=[
            pl.BlockSpec((TOKEN_BLOCK, D_MODEL), tok_map),
            pl.BlockSpec((TOKEN_BLOCK * SUBLANES, LANES), tok_map),
            pl.BlockSpec((TOP_K, TOKEN_BLOCK), tok_map_t), pl.BlockSpec((TOP_K, TOKEN_BLOCK), tok_map_t),
            pl.BlockSpec((TOP_K, TOKEN_BLOCK), tok_map_t),
            _full((N_EXPERTS, 1)),
            pl.BlockSpec((None, WINDOW, KV_WIDTH), stream_map), pl.BlockSpec((None, WINDOW, KV_WIDTH), stream_map),
            _full(kv_cache), _full(kv_cache), _full((T_SAMPLE, SGU_WIDTH)),
        ],
        out_shape=(
            jax.ShapeDtypeStruct((T_ALL, D_MODEL), F32),
            jax.ShapeDtypeStruct((T_ALL * SUBLANES, LANES), I32),
            jax.ShapeDtypeStruct((TOP_K, T_ALL), I32),
            jax.ShapeDtypeStruct((TOP_K, T_ALL), I32),
            jax.ShapeDtypeStruct((TOP_K, T_ALL), F32),
            jax.ShapeDtypeStruct((N_EXPERTS, 1), I32),
            jax.ShapeDtypeStruct((BATCH, WINDOW, KV_WIDTH), F32),
            jax.ShapeDtypeStruct((BATCH, WINDOW, KV_WIDTH), F32),
            jax.ShapeDtypeStruct(kv_cache, F32),
            jax.ShapeDtypeStruct(kv_cache, F32),
            jax.ShapeDtypeStruct((T_SAMPLE, SGU_WIDTH), F32),
        ),
        scratch_shapes=[bf_band, bf_band, bf_band, bf_band, pltpu.VMEM((N_EXPERTS, 1), F32)],
        compiler_params=cparams(dimension_semantics=("arbitrary",)),
        name="mixer",
    )(x_prompt.reshape(T_PROMPT, D_MODEL), x_sample.reshape(T_SAMPLE, D_MODEL),
      cache_k[0].reshape(kv_cache), cache_v[0].reshape(kv_cache), *weights)

    counts = cnt_all[:, 0]
    blocks_per = (counts + EXPERT_ROWS - 1) // EXPERT_ROWS
    block_end = jnp.cumsum(blocks_per)
    pstart = (block_end - blocks_per) * EXPERT_ROWS
    n_valid = block_end[-1:].astype(I32)
    step = jnp.minimum(jnp.arange(N_ROW_BLOCKS, dtype=I32), n_valid - 1)
    expert_of_step = step[:, None] >= block_end[None, :]
    blk_e = jnp.minimum(jnp.sum(expert_of_step, axis=1), N_EXPERTS - 1).astype(I32)
    own = (jnp.arange(N_EXPERTS, dtype=I32)[None, :] == blk_e[:, None]).astype(I32)
    rows_left = jnp.sum(own * (counts + pstart)[None, :], axis=1) - step * EXPERT_ROWS
    blk_rows = jnp.clip(rows_left, 0, EXPERT_ROWS).astype(I32)
    dispatch_meta = jnp.concatenate([(pstart + counts).astype(I32), n_valid])

    xs = pl.pallas_call(
        _dispatch_kernel,
        grid_spec=pltpu.PrefetchScalarGridSpec(
            num_scalar_prefetch=1,
            grid=(N_TOKEN_BLOCKS,),
            in_specs=[
                pl.BlockSpec((N_EXPERTS, 1), lambda i, m: (0, 0)),
                pl.BlockSpec((TOP_K, TOKEN_BLOCK), lambda i, m: (0, i)),
                pl.BlockSpec((TOP_K, TOKEN_BLOCK), lambda i, m: (0, i)),
                pl.BlockSpec((TOKEN_BLOCK * SUBLANES, LANES), lambda i, m: (i, 0)),
            ],
            out_specs=pl.BlockSpec(memory_space=pl.ANY),
            scratch_shapes=[
                pltpu.VMEM((EXPERT_ROWS * SUBLANES, LANES), I32),
                pltpu.VMEM((TOP_K, TOKEN_BLOCK), I32),
                pltpu.SMEM((TOP_K, TOKEN_BLOCK), I32),
                pltpu.SemaphoreType.DMA, pltpu.SemaphoreType.DMA, pltpu.SemaphoreType.DMA,
            ],
        ),
        out_shape=jax.ShapeDtypeStruct((SORTED_ROWS * SUBLANES, LANES), I32),
        compiler_params=cparams(dimension_semantics=("arbitrary",)),
        name="dispatch",
    )(dispatch_meta, pstart.reshape(N_EXPERTS, 1).astype(I32), idx_all, rank_all, h2_all)

    def expert_map(e, bounds, nr):
        return (e, 0, 0)

    block_bounds = jnp.concatenate([jnp.zeros((1,), I32), block_end.astype(I32)])
    out_kt = pl.pallas_call(
        _expert_kernel,
        grid_spec=pltpu.PrefetchScalarGridSpec(
            num_scalar_prefetch=2,
            grid=(N_EXPERTS,),
            in_specs=[
                pl.BlockSpec(memory_space=pl.ANY),
                pl.BlockSpec((None, D_MODEL, 2 * D_FF), expert_map),
                pl.BlockSpec((None, 1, 2 * D_FF), expert_map),
                pl.BlockSpec((None, D_FF, D_MODEL), expert_map),
                pl.BlockSpec((None, 1, D_MODEL), expert_map),
            ],
            out_specs=pl.BlockSpec(memory_space=pl.ANY),
            scratch_shapes=[
                pltpu.VMEM((D_MODEL, 2 * D_FF), BF16), pltpu.VMEM((D_FF, D_MODEL), BF16),
                pltpu.VMEM((2, EXPERT_ROWS * SUBLANES, LANES), I32),
                pltpu.VMEM((STAGES * EXPERT_ROWS * SUBLANES, LANES), F32),
                pltpu.VMEM((STAGES, SUBLANES, EXPERT_ROWS), I32),
                pltpu.SMEM((STAGES, SUBLANES, EXPERT_ROWS), I32),
                pltpu.SemaphoreType.DMA((2,)), pltpu.SemaphoreType.DMA((STAGES,)),
                pltpu.SemaphoreType.DMA((STAGES,)),
            ],
        ),
        out_shape=jax.ShapeDtypeStruct((OUT_ROWS * SUBLANES, LANES), F32),
        compiler_params=cparams(dimension_semantics=("arbitrary",)),
        name="experts",
    )(block_bounds, blk_rows, xs, w_gate_up[0], b_gate_up[0].reshape(N_EXPERTS, 1, 2 * D_FF), w_down[0],
      b_down[0].reshape(N_EXPERTS, 1, D_MODEL))

    def slot_spec(kk):
        return pl.BlockSpec((TOKEN_BLOCK * SUBLANES, LANES), lambda i: (kk * N_TOKEN_BLOCKS + i, 0))

    y_p, y_s = pl.pallas_call(
        _combine_kernel,
        grid=(N_TOKEN_BLOCKS,),
        in_specs=[slot_spec(kk) for kk in range(TOP_K)] + [
            pl.BlockSpec((TOKEN_BLOCK, TOP_K), lambda i: (i, 0)),
            pl.BlockSpec((TOKEN_BLOCK, D_MODEL), lambda i: (i, 0)),
            _full((1, D_MODEL)),
        ],
        out_specs=[
            pl.BlockSpec((TOKEN_BLOCK, D_MODEL), lambda i: (jnp.minimum(i, PROMPT_BLOCKS - 1), 0)),
            pl.BlockSpec((TOKEN_BLOCK, D_MODEL), lambda i: (jnp.maximum(i - PROMPT_BLOCKS, 0), 0)),
        ],
        out_shape=(jax.ShapeDtypeStruct((T_PROMPT, D_MODEL), F32), jax.ShapeDtypeStruct((T_SAMPLE, D_MODEL), F32)),
        compiler_params=cparams(dimension_semantics=("arbitrary",)),
        name="combine",
    )(out_kt, out_kt, out_kt, out_kt, gate_all.T, x1_all, fg)

    kv5 = (1, -1, WINDOW, N_KV_HEADS, HEAD_DIM)
    return (y_p.reshape(BATCH, SEQ, D_MODEL), y_s.reshape(DEC_BATCH, DEC_SEQ, D_MODEL),
            kp_new.reshape(kv5), vp_new.reshape(kv5), ks_new.reshape(kv5), vs_new.reshape(kv5),
            gv_s.reshape(1, DEC_BATCH, DEC_SEQ, SGU_HEADS, SGU_HEAD_DIM))
```

```python
import functools

import jax
import jax.numpy as jnp
from jax import lax
from jax.experimental import pallas as pl
from jax.experimental.pallas import tpu as pltpu

D_MODEL = 1024
BATCH = 2
SEQ = 8192
DEC_BATCH = 8
DEC_SEQ = 64
CHUNK = 64
WINDOW = 128
BAND = WINDOW + CHUNK
HEAD_DIM = 64
N_Q_HEADS = 8
N_KV_HEADS = 2
Q_PER_KV = N_Q_HEADS // N_KV_HEADS
ATTN_WIDTH = N_Q_HEADS * HEAD_DIM
KV_WIDTH = N_KV_HEADS * HEAD_DIM
QKV_COLS = ATTN_WIDTH + 2 * KV_WIDTH
SGU_CHUNK = 128
SGU_HEADS = 4
SGU_HEAD_DIM = 128
SGU_WIDTH = SGU_HEADS * SGU_HEAD_DIM
IN_COLS = QKV_COLS + 2 * SGU_WIDTH
N_EXPERTS = 32
TOP_K = 4
D_FF = 1024
SWIGLU_LIMIT = 7.0
SWIGLU_ALPHA = 1.702
RMS_EPS = 1e-5
NEG_INF = -1e30

LANES = 128
SUBLANES = 8
assert D_MODEL == LANES * SUBLANES

T_PROMPT = BATCH * SEQ
T_SAMPLE = DEC_BATCH * DEC_SEQ
T_ALL = T_PROMPT + T_SAMPLE
TOKEN_BLOCK = 512
SEQ_BLOCKS = SEQ // TOKEN_BLOCK
N_TOKEN_BLOCKS = T_ALL // TOKEN_BLOCK
PROMPT_BLOCKS = T_PROMPT // TOKEN_BLOCK
EXPERT_ROWS = 256
N_SLOTS = T_ALL * TOP_K
MIN_PAGES = N_SLOTS // EXPERT_ROWS
N_PAGES = MIN_PAGES + N_EXPERTS
STAGES = 4
BLOCK_COPIES = TOP_K * TOKEN_BLOCK
DUMP_PAGE = N_PAGES
XS_PAGES = N_PAGES + STAGES * BLOCK_COPIES // EXPERT_ROWS
DUMP_ROW = N_SLOTS
OUT_ROWS = DUMP_ROW + STAGES * EXPERT_ROWS
TAG_MASK = 0xFFFF
VMEM_LIMIT_BYTES = 56 * 1024 * 1024
ISSUE_UNROLL = 8

assert T_SAMPLE == TOKEN_BLOCK and SEQ % TOKEN_BLOCK == 0 and N_SLOTS % EXPERT_ROWS == 0
assert T_ALL <= TAG_MASK and N_PAGES <= LANES * N_EXPERTS and T_ALL // EXPERT_ROWS + 1 <= LANES

F32 = jnp.float32
BF16 = jnp.bfloat16
I32 = jnp.int32


def _rms(x, g):
    return x * lax.rsqrt(jnp.mean(x * x, axis=-1, keepdims=True) + RMS_EPS) * g


def _gelu(x):
    return 0.5 * x * (1.0 + lax.erf(x * 0.7071067811865476))


def _store_tile_rows(ref, first_row, val):
    n = val.shape[0]
    for c in range(SUBLANES):
        ref[pl.ds(first_row * SUBLANES + c, n, stride=SUBLANES), :] = val[:, c * LANES:(c + 1) * LANES]


def _load_tile_rows(ref, n):
    return jnp.concatenate([ref[pl.ds(c, n, stride=SUBLANES), :] for c in range(SUBLANES)], axis=1)


def _tile_row(ref, row):
    return ref.at[pl.ds(pl.multiple_of(row * SUBLANES, SUBLANES), SUBLANES), :]


def _tile_rows(ref, first_row, n):
    return ref.at[pl.ds(pl.multiple_of(first_row * SUBLANES, SUBLANES), n * SUBLANES), :]


def _in_proj(x, n1g_ref, w_in_ref, sgng_ref):
    h = _rms(x, n1g_ref[...])
    cols = jnp.dot(h.astype(BF16), w_in_ref[...], preferred_element_type=F32)
    q = cols[:, :ATTN_WIDTH] * (HEAD_DIM ** -0.5)
    k = cols[:, ATTN_WIDTH:ATTN_WIDTH + KV_WIDTH]
    v = cols[:, ATTN_WIDTH + KV_WIDTH:QKV_COLS]
    gu = _gelu(cols[:, QKV_COLS:QKV_COLS + SGU_WIDTH])
    gv = _gelu(cols[:, QKV_COLS + SGU_WIDTH:])
    sg = sgng_ref[...]
    gvn = jnp.concatenate(
        [_rms(gv[:, i * SGU_HEAD_DIM:(i + 1) * SGU_HEAD_DIM], sg[:, i * SGU_HEAD_DIM:(i + 1) * SGU_HEAD_DIM])
         for i in range(SGU_HEADS)], axis=1)
    return q, k, v, gu, gvn


def _dup_heads(t):
    lane = lax.broadcasted_iota(I32, t.shape, 1)
    r = pltpu.roll(t, HEAD_DIM, axis=1)
    return jnp.where(lane < HEAD_DIM, t, r), jnp.where(lane < HEAD_DIM, r, t)


def _attend_chunk(q64, kk, vv, alibi_ref, sink_ref, key_bias):
    lane = lax.broadcasted_iota(I32, (CHUNK, 2 * HEAD_DIM), 1)
    lo = lane < HEAD_DIM
    pairs = []
    for g in range(N_KV_HEADS):
        rows = []
        for p in range(2):
            qp = q64[:, (2 * g + p) * 128:(2 * g + p + 1) * 128]
            rows.append(jnp.where(lo, qp, 0.0))
            rows.append(jnp.where(lo, 0.0, qp))
        q4 = jnp.concatenate(rows, axis=0).astype(BF16)
        sc = lax.dot_general(q4, kk[g], (((1,), (1,)), ((), ())), preferred_element_type=F32)
        sc = sc + alibi_ref[g]
        if key_bias is not None:
            sc = sc + key_bias
        sink = sink_ref[g]
        m = jnp.maximum(jnp.max(sc, axis=1, keepdims=True), sink)
        p_un = jnp.exp(sc - m)
        den = jnp.sum(p_un, axis=1, keepdims=True) + jnp.exp(sink - m)
        o = jnp.dot(p_un.astype(BF16), vv[g], preferred_element_type=F32) / den
        for p in range(2):
            pairs.append(jnp.where(lo, o[(2 * p) * CHUNK:(2 * p + 1) * CHUNK],
                                   o[(2 * p + 1) * CHUNK:(2 * p + 2) * CHUNK]))
    return jnp.concatenate(pairs, axis=1)


def _sgu_weights(sguw_ref):
    sub_i = lax.broadcasted_iota(I32, (SGU_CHUNK, SGU_CHUNK), 0) // CHUNK
    sub_j = lax.broadcasted_iota(I32, (SGU_CHUNK, SGU_CHUNK), 1) // CHUNK
    keep = sub_j <= sub_i
    return [jnp.where(keep, sguw_ref[i], 0.0).astype(BF16) for i in range(SGU_HEADS)]


def _sgu_chunk(ws, gu_c, gvn_c, sgubt_ref):
    outs = []
    for i in range(SGU_HEADS):
        sl = slice(i * SGU_HEAD_DIM, (i + 1) * SGU_HEAD_DIM)
        sp = jnp.dot(ws[i], gvn_c[:, sl].astype(BF16), preferred_element_type=F32) + sgubt_ref[:, i:i + 1]
        outs.append(gu_c[:, sl] * sp)
    return jnp.concatenate(outs, axis=1)


def _merge_route(x, attn, sgu, tok_base, slot, aog_ref, sog_ref, w_out_ref, n2g_ref, wr_ref, br_ref,
                 cnt_scr, cur_scr, free_scr, table_scr, x1_ref, gate_ref, stage, dest_vmem):
    n = x.shape[0]
    a_n = _rms(attn, aog_ref[...]).astype(BF16)
    s_n = _rms(sgu, sog_ref[...]).astype(BF16)
    x1 = (x + jnp.dot(a_n, w_out_ref[:ATTN_WIDTH, :], preferred_element_type=F32)
          + jnp.dot(s_n, w_out_ref[ATTN_WIDTH:, :], preferred_element_type=F32))
    x1_ref[...] = x1
    h2 = _rms(x1, n2g_ref[...])
    logits = lax.dot_general(wr_ref[...], h2, (((1,), (1,)), ((), ())),
                             precision=lax.Precision.HIGHEST, preferred_element_type=F32) + br_ref[...]
    e_iota = lax.broadcasted_iota(I32, (N_EXPERTS, n), 0).astype(F32)
    cur = logits
    vals, idxs, sels = [], [], []
    for _ in range(TOP_K):
        m = jnp.max(cur, axis=0, keepdims=True)
        idx = jnp.min(jnp.where(cur == m, e_iota, float(N_EXPERTS)), axis=0, keepdims=True)
        sel = e_iota == idx
        vals.append(m)
        idxs.append(idx)
        sels.append(sel)
        cur = jnp.where(sel, -jnp.inf, cur)
    exps = [jnp.exp(v - vals[0]) for v in vals]
    den = exps[0] + exps[1] + exps[2] + exps[3]
    gate_ref[...] = jnp.concatenate([e / den for e in exps], axis=0)
    onehot = jnp.where(sels[0] | sels[1] | sels[2] | sels[3], 1.0, 0.0)
    before = lax.broadcasted_iota(I32, (n, n), 0) < lax.broadcasted_iota(I32, (n, n), 1)
    upper = jnp.where(before, 1.0, 0.0).astype(BF16)
    count = cnt_scr[...]
    rank = jnp.dot(onehot.astype(BF16), upper, preferred_element_type=F32) + count
    added = jnp.sum(onehot, axis=1, keepdims=True)
    inv_page = 1.0 / EXPERT_ROWS
    had_pages = jnp.floor((count + (EXPERT_ROWS - 1)) * inv_page)
    need_pages = jnp.floor((count + added + (EXPERT_ROWS - 1)) * inv_page)
    new_pages = need_pages - had_pages
    below = (lax.broadcasted_iota(I32, (N_EXPERTS, N_EXPERTS), 1)
             < lax.broadcasted_iota(I32, (N_EXPERTS, N_EXPERTS), 0))
    first_new = free_scr[...] + jnp.dot(
        jnp.where(below, 1.0, 0.0).astype(BF16), jnp.broadcast_to(new_pages, (N_EXPERTS, LANES)).astype(BF16),
        preferred_element_type=F32)[:, 0:1]
    page_idx = jnp.floor(rank * inv_page)
    page = jnp.where(page_idx < had_pages, cur_scr[...], first_new + (page_idx - had_pages))
    place = page * EXPERT_ROWS + (rank - page_idx * EXPERT_ROWS)
    dest_vmem[slot] = jnp.concatenate(
        [jnp.sum(jnp.where(s, place, 0.0), axis=0, keepdims=True) for s in sels], axis=0).astype(I32)
    lane = lax.broadcasted_iota(I32, (N_EXPERTS, LANES), 1).astype(F32)
    table_scr[...] = jnp.where((lane >= had_pages) & (lane < need_pages), first_new + (lane - had_pages),
                               table_scr[...])
    cur_scr[...] = jnp.where(new_pages > 0, first_new + new_pages - 1.0, cur_scr[...])
    cnt_scr[...] = count + added
    free_scr[...] = free_scr[...] + jnp.sum(new_pages, axis=0, keepdims=True)

    words = lax.bitcast_convert_type(h2.astype(BF16).astype(F32), I32)
    tok = (lax.broadcasted_iota(I32, (1, n), 1) + tok_base).astype(F32)
    tag_t = jnp.concatenate([tok] + idxs + [jnp.zeros((LANES - 1 - TOP_K, n), F32)], axis=0)
    tagged = words[:, :LANES] | tag_t.T.astype(I32)
    _store_tile_rows(stage, slot * TOKEN_BLOCK, jnp.concatenate([tagged, words[:, LANES:]], axis=1))


def _mixer_kernel(xp_ref, xs_in_ref, ck_ref, cv_ref, n1g_ref, w_in_ref, sgng_ref, sink_ref, alibi_ref, sguw_ref,
                  sgubt_ref, aog_ref, sog_ref, w_out_ref, n2g_ref, wr_ref, br_ref,
                  x1_ref, gate_ref, cnt_ref, table_ref, kp_ref, vp_ref, ks_ref, vs_ref, gv_ref, xs_ref,
                  kk0, kk1, vv0, vv1, cnt_scr, cur_scr, free_scr, table_scr, stage, zero_page,
                  dest_vmem, dest_smem, meta_vmem, meta_smem, sem_rows, sem_dest, sem_fill):
    n = pl.program_id(0)
    s = n % SEQ_BLOCKS
    slot = n % STAGES
    leaving = (n + STAGES - 2) % STAGES
    block_words = TOKEN_BLOCK * SUBLANES
    weights = (aog_ref, sog_ref, w_out_ref, n2g_ref, wr_ref, br_ref)
    state = (cnt_scr, cur_scr, free_scr, table_scr)

    def dump_row(st):
        return DUMP_PAGE * EXPERT_ROWS + st * BLOCK_COPIES

    def dest_copy(st):
        return pltpu.make_async_copy(dest_vmem.at[st], dest_smem.at[st], sem_dest.at[st])

    def rows_wait(st):
        for _ in range(TOP_K):
            pltpu.make_async_copy(_tile_rows(stage, 0, TOKEN_BLOCK), _tile_rows(xs_ref, 0, TOKEN_BLOCK),
                                  sem_rows.at[st]).wait()

    def row_copy(st, kk, t):
        return pltpu.make_async_copy(_tile_row(stage, st * TOKEN_BLOCK + t), _tile_row(xs_ref, dest_smem[st, kk, t]),
                                     sem_rows.at[st])

    def send_rows(st):
        dest_copy(st).wait()
        for t in range(TOKEN_BLOCK):
            for kk in range(TOP_K):
                row_copy(st, kk, t).start(priority=kk % 2)

    def send_rows_looped(st):
        dest_copy(st).wait()

        def issue(t, carry):
            for kk in range(TOP_K):
                row_copy(st, kk, t).start(priority=kk % 2)
            return carry

        lax.fori_loop(0, TOKEN_BLOCK, issue, 0, unroll=ISSUE_UNROLL)

    @pl.when(n == 0)
    def _():
        for ref in state:
            ref[...] = jnp.zeros_like(ref)
        stage[...] = jnp.zeros_like(stage)
        zero_page[...] = jnp.zeros_like(zero_page)
        for buf in (kk0, kk1, vv0, vv1):
            buf[0:WINDOW, :] = jnp.zeros((WINDOW, 2 * HEAD_DIM), BF16)
        for st in range(STAGES):
            if st < STAGES - 2:
                for kk in range(TOP_K):
                    pltpu.make_async_copy(_tile_rows(stage, st * TOKEN_BLOCK, TOKEN_BLOCK),
                                          _tile_rows(xs_ref, dump_row(st) + kk * TOKEN_BLOCK, TOKEN_BLOCK),
                                          sem_rows.at[st]).start()
            else:
                dest_vmem[st] = (dump_row(st) + lax.broadcasted_iota(I32, (TOP_K, TOKEN_BLOCK), 0) * TOKEN_BLOCK
                                 + lax.broadcasted_iota(I32, (TOP_K, TOKEN_BLOCK), 1))
                dest_copy(st).start()

    @pl.when(n < PROMPT_BLOCKS)
    def _():
        rows_wait(slot)
        send_rows(leaving)
        x = xp_ref[...]
        q, k, v, gu, gvn = _in_proj(x, n1g_ref, w_in_ref, sgng_ref)
        kp_ref[...] = k[TOKEN_BLOCK - WINDOW:, :]
        vp_ref[...] = v[TOKEN_BLOCK - WINDOW:, :]

        k0, k1 = _dup_heads(k)
        v0, v1 = _dup_heads(v)
        for buf, val in ((kk0, k0), (kk1, k1), (vv0, v0), (vv1, v1)):
            buf[WINDOW:, :] = val.astype(BF16)

        attn_chunks = []
        for j in range(TOKEN_BLOCK // CHUNK):
            band = slice(j * CHUNK, j * CHUNK + BAND)
            key_bias = None
            if j * CHUNK < WINDOW:
                key_pos = lax.broadcasted_iota(I32, (1, BAND), 1) + (s * TOKEN_BLOCK + j * CHUNK - WINDOW)
                key_bias = jnp.where(key_pos < 0, NEG_INF, 0.0)
            attn_chunks.append(_attend_chunk(q[j * CHUNK:(j + 1) * CHUNK], (kk0[band, :], kk1[band, :]),
                                             (vv0[band, :], vv1[band, :]), alibi_ref, sink_ref, key_bias))
        attn = jnp.concatenate(attn_chunks, axis=0)

        for buf in (kk0, kk1, vv0, vv1):
            tail = buf[TOKEN_BLOCK:TOKEN_BLOCK + WINDOW, :]
            buf[0:WINDOW, :] = jnp.where(s == SEQ_BLOCKS - 1, jnp.zeros_like(tail), tail)

        ws = _sgu_weights(sguw_ref)
        sgu = jnp.concatenate(
            [_sgu_chunk(ws, gu[c * SGU_CHUNK:(c + 1) * SGU_CHUNK], gvn[c * SGU_CHUNK:(c + 1) * SGU_CHUNK],
                        sgubt_ref) for c in range(TOKEN_BLOCK // SGU_CHUNK)], axis=0)
        _merge_route(x, attn, sgu, n * TOKEN_BLOCK, slot, *weights, *state, x1_ref, gate_ref, stage, dest_vmem)
        dest_copy(slot).start()

    @pl.when(n == PROMPT_BLOCKS)
    def _():
        rows_wait(slot)
        send_rows(leaving)
        x = xs_in_ref[...]
        q, k, v, gu, gvn = _in_proj(x, n1g_ref, w_in_ref, sgng_ref)
        gv_ref[...] = gvn
        ws = _sgu_weights(sguw_ref)
        attn_chunks, sgu_chunks = [], []
        pad = jnp.zeros((SGU_CHUNK - DEC_SEQ, SGU_WIDTH), F32)
        for b in range(DEC_BATCH):
            rows = slice(b * DEC_SEQ, (b + 1) * DEC_SEQ)
            k_all = jnp.concatenate([ck_ref[b], k[rows]], axis=0)
            v_all = jnp.concatenate([cv_ref[b], v[rows]], axis=0)
            ks_ref[b] = k_all[DEC_SEQ:]
            vs_ref[b] = v_all[DEC_SEQ:]
            k0, k1 = _dup_heads(k_all)
            v0, v1 = _dup_heads(v_all)
            attn_chunks.append(_attend_chunk(q[rows], (k0.astype(BF16), k1.astype(BF16)),
                                             (v0.astype(BF16), v1.astype(BF16)), alibi_ref, sink_ref, None))
            gated = _sgu_chunk(ws, jnp.concatenate([gu[rows], pad], axis=0),
                               jnp.concatenate([gvn[rows], pad], axis=0), sgubt_ref)
            sgu_chunks.append(gated[:DEC_SEQ])
        attn = jnp.concatenate(attn_chunks, axis=0)
        sgu = jnp.concatenate(sgu_chunks, axis=0)
        _merge_route(x, attn, sgu, n * TOKEN_BLOCK, slot, *weights, *state, x1_ref, gate_ref, stage, dest_vmem)
        dest_copy(slot).start()

        send_rows_looped((N_TOKEN_BLOCKS - 2) % STAGES)
        send_rows_looped((N_TOKEN_BLOCKS - 1) % STAGES)
        for st in range(STAGES):
            rows_wait(st)
        count = cnt_scr[...]
        left = count - jnp.floor(count * (1.0 / EXPERT_ROWS)) * EXPERT_ROWS
        lane = lax.broadcasted_iota(I32, (N_EXPERTS, LANES), 1)
        meta = jnp.where(lane == 0, cur_scr[...], jnp.where(lane == 1, left, jnp.where(lane == 2, free_scr[...], 0.0)))
        meta_t = jnp.concatenate([meta, jnp.zeros((LANES - N_EXPERTS, LANES), F32)], axis=0).T
        meta_vmem[...] = meta_t[0:SUBLANES].astype(I32)
        to_smem = pltpu.make_async_copy(meta_vmem, meta_smem, sem_dest.at[0])
        to_smem.start()
        to_smem.wait()

        def zero_row_copy(row):
            return pltpu.make_async_copy(_tile_row(zero_page, 0), _tile_row(xs_ref, row), sem_fill)

        n_fill = jnp.int32(0)
        for e in range(N_EXPERTS):
            used = meta_smem[1, e]
            first = jnp.where(used > 0, used, EXPERT_ROWS)
            base = meta_smem[0, e] * EXPERT_ROWS

            def fill(r, carry):
                zero_row_copy(base + r).start()
                return carry

            lax.fori_loop(first, EXPERT_ROWS, fill, 0)
            n_fill = n_fill + (EXPERT_ROWS - first)

        def drain(r, carry):
            zero_row_copy(0).wait()
            return carry

        lax.fori_loop(0, n_fill, drain, 0)
        in_use = meta_smem[2, 0]
        for j in range(N_PAGES - MIN_PAGES):
            @pl.when(in_use + j < N_PAGES)
            def _():
                pltpu.make_async_copy(zero_page, _tile_rows(xs_ref, (in_use + j) * EXPERT_ROWS, EXPERT_ROWS),
                                      sem_fill).start()
        for j in range(N_PAGES - MIN_PAGES):
            @pl.when(in_use + j < N_PAGES)
            def _():
                pltpu.make_async_copy(zero_page, _tile_rows(xs_ref, 0, EXPERT_ROWS), sem_fill).wait()

    cnt_ref[...] = cnt_scr[...].astype(I32)
    table_ref[...] = table_scr[...].astype(I32)


def _expert_kernel(bstart_ref, nrow_ref, page_ref, xs_ref, wgu_ref, bgu_ref, wd_ref, bd_ref, out_ref,
                   wgu_bf, wd_bf, xbuf, stage, ids_vmem, ids_smem, sem_x, sem_rows, sem_ids):
    expert = pl.program_id(0)
    first_block = bstart_ref[expert]
    end_block = bstart_ref[expert + 1]
    nv = bstart_ref[N_EXPERTS]
    block_words = EXPERT_ROWS * SUBLANES

    def x_copy(g, s):
        return pltpu.make_async_copy(_tile_rows(xs_ref, page_ref[g] * EXPERT_ROWS, EXPERT_ROWS), xbuf.at[s],
                                     sem_x.at[s])

    def dump_row(s):
        return DUMP_ROW + s * EXPERT_ROWS

    def ids_copy(s):
        return pltpu.make_async_copy(ids_vmem.at[s], ids_smem.at[s], sem_ids.at[s])

    def rows_wait(s):
        whole = pl.ds(0, block_words)
        pltpu.make_async_copy(stage.at[whole, :], out_ref.at[whole, :], sem_rows.at[s]).wait()

    def issue_rows(s):
        ids_copy(s).wait()
        for r in range(EXPERT_ROWS):
            pltpu.make_async_copy(_tile_row(stage, s * EXPERT_ROWS + r), _tile_row(out_ref, ids_smem[s, 0, r]),
                                  sem_rows.at[s]).start(priority=r % 2)

    def compute(g, xs_slot, s):
        words = _load_tile_rows(xbuf.at[xs_slot], EXPERT_ROWS)
        tag = words[:, :LANES] & TAG_MASK
        x = lax.bitcast_convert_type(words & ~TAG_MASK, F32).astype(BF16)
        gu = jnp.dot(x, wgu_bf[...], preferred_element_type=F32) + bgu_ref[...]
        x_glu = jnp.minimum(gu[:, :D_FF], SWIGLU_LIMIT)
        x_lin = jnp.clip(gu[:, D_FF:], -SWIGLU_LIMIT, SWIGLU_LIMIT)
        act = x_glu * (1.0 / (1.0 + jnp.exp(-SWIGLU_ALPHA * x_glu))) * (x_lin + 1.0)
        out = jnp.dot(act.astype(BF16), wd_bf[...], preferred_element_type=F32) + bd_ref[...]
        _store_tile_rows(stage, s * EXPERT_ROWS, out)
        k_slot = sum(j * jnp.where(tag[:, 1 + j:2 + j] == expert, 1, 0) for j in range(1, TOP_K))
        row = lax.broadcasted_iota(I32, (EXPERT_ROWS, 1), 0)
        dst = jnp.where(row < nrow_ref[g], k_slot * T_ALL + tag[:, 0:1], dump_row(s) + row)
        dst_t = jnp.broadcast_to(dst.astype(F32), (EXPERT_ROWS, LANES)).T
        ids_vmem[s] = dst_t[0:SUBLANES].astype(I32)
        ids_copy(s).start()

    @pl.when(expert == 0)
    def _():
        stage[...] = jnp.zeros_like(stage)
        for s in range(STAGES):
            if s < STAGES - 2:
                pltpu.make_async_copy(stage.at[pl.ds(s * block_words, block_words), :],
                                      out_ref.at[pl.ds(dump_row(s) * SUBLANES, block_words), :],
                                      sem_rows.at[s]).start()
            else:
                ids_vmem[s] = dump_row(s) + lax.broadcasted_iota(I32, (SUBLANES, EXPERT_ROWS), 1)
                ids_copy(s).start()
        x_copy(0, 0).start()

    @pl.when(end_block > first_block)
    def _():
        rows = 128

        def cast_rows(c, carry):
            r = pl.multiple_of(c * rows, rows)
            wgu_bf[pl.ds(r, rows), :] = wgu_ref[pl.ds(r, rows), :].astype(BF16)
            wd_bf[pl.ds(r, rows), :] = wd_ref[pl.ds(r, rows), :].astype(BF16)
            return carry

        lax.fori_loop(0, D_MODEL // rows, cast_rows, 0)

    def block_step(g, carry):
        xs_slot = g % 2
        slot = g % STAGES
        x_copy(g, xs_slot).wait()
        x_copy(jnp.minimum(g + 1, nv - 1), 1 - xs_slot).start()
        rows_wait(slot)
        issue_rows((g + STAGES - 2) % STAGES)
        compute(g, xs_slot, slot)
        return carry

    lax.fori_loop(first_block, end_block, block_step, 0)

    @pl.when(expert == N_EXPERTS - 1)
    def _():
        x_copy(nv - 1, nv % 2).wait()
        issue_rows((nv + STAGES - 2) % STAGES)
        issue_rows((nv + STAGES - 1) % STAGES)
        for s in range(STAGES):
            rows_wait(s)


def _combine_kernel(o0_ref, o1_ref, o2_ref, o3_ref, gate_ref, x1_ref, fg_ref, yp_ref, ys_ref):
    i = pl.program_id(0)
    gates = gate_ref[...]
    y = x1_ref[...]
    for kk, o_ref in enumerate((o0_ref, o1_ref, o2_ref, o3_ref)):
        y = y + gates[:, kk:kk + 1] * _load_tile_rows(o_ref, TOKEN_BLOCK)
    out = _rms(y, fg_ref[...])

    @pl.when(i < PROMPT_BLOCKS)
    def _():
        yp_ref[...] = out

    @pl.when(i >= PROMPT_BLOCKS)
    def _():
        ys_ref[...] = out


def _full(shape):
    return pl.BlockSpec(shape, lambda *_: (0,) * len(shape))


def kernel(x_prompt, x_sample, cache_k, cache_v, norm1_g, w_in, sgu_norm_g, attn_sinks, sgu_w, sgu_b, attn_out_g,
           sgu_out_g, w_out, norm2_g, w_router, b_router, w_gate_up, b_gate_up, w_down, b_down, final_g):
    n1g = norm1_g[0].reshape(1, D_MODEL)
    w_in_b = w_in[0].astype(BF16)
    sgng = sgu_norm_g[0].reshape(1, SGU_WIDTH)
    sink_cols = jnp.repeat(attn_sinks[0].reshape(N_KV_HEADS, Q_PER_KV, 1), CHUNK, axis=1).reshape(
        N_KV_HEADS, Q_PER_KV * CHUNK, 1)
    slopes = jnp.exp2(-8.0 * jnp.arange(1, N_Q_HEADS + 1, dtype=F32) / N_Q_HEADS).reshape(N_KV_HEADS, Q_PER_KV)
    dist = jnp.abs(jnp.arange(CHUNK)[:, None] + WINDOW - jnp.arange(BAND)[None, :]).astype(F32)
    alibi = (-slopes[:, :, None, None] * dist).reshape(N_KV_HEADS, Q_PER_KV * CHUNK, BAND)
    sguw = sgu_w[0]
    sgubt = sgu_b[0].T
    aog = attn_out_g[0].reshape(1, ATTN_WIDTH)
    sog = sgu_out_g[0].reshape(1, SGU_WIDTH)
    w_out_b = w_out[0].astype(BF16)
    n2g = norm2_g[0].reshape(1, D_MODEL)
    wr_t = w_router[0].T
    br_c = b_router[0].reshape(N_EXPERTS, 1)
    fg = final_g.reshape(1, D_MODEL)

    weight_specs = [
        _full((1, D_MODEL)), _full((D_MODEL, IN_COLS)), _full((1, SGU_WIDTH)),
        _full((N_KV_HEADS, Q_PER_KV * CHUNK, 1)), _full((N_KV_HEADS, Q_PER_KV * CHUNK, BAND)),
        _full((SGU_HEADS, SGU_CHUNK, SGU_CHUNK)), _full((SGU_CHUNK, SGU_HEADS)),
        _full((1, ATTN_WIDTH)), _full((1, SGU_WIDTH)), _full((D_MODEL, D_MODEL)), _full((1, D_MODEL)),
        _full((N_EXPERTS, D_MODEL)), _full((N_EXPERTS, 1)),
    ]
    weights = (n1g, w_in_b, sgng, sink_cols, alibi, sguw, sgubt, aog, sog, w_out_b, n2g, wr_t, br_c)
    cparams = functools.partial(pltpu.CompilerParams, vmem_limit_bytes=VMEM_LIMIT_BYTES)
    any_spec = pl.BlockSpec(memory_space=pl.ANY)

    def tok_map(n):
        return (n, 0)

    def tok_map_t(n):
        return (0, n)

    def stream_map(n):
        return (jnp.minimum(n // SEQ_BLOCKS, BATCH - 1), 0, 0)

    kv_cache = (DEC_BATCH, WINDOW, KV_WIDTH)
    bf_band = pltpu.VMEM((WINDOW + TOKEN_BLOCK, 2 * HEAD_DIM), BF16)
    state_col = pltpu.VMEM((N_EXPERTS, 1), F32)
    (x1_all, gate_all, cnt_all, table, kp_new, vp_new, ks_new, vs_new, gv_s, xs) = pl.pallas_call(
        _mixer_kernel,
        grid=(N_TOKEN_BLOCKS,),
        in_specs=[pl.BlockSpec((TOKEN_BLOCK, D_MODEL), lambda n: (jnp.minimum(n, PROMPT_BLOCKS - 1), 0)),
                  _full((T_SAMPLE, D_MODEL)), _full(kv_cache), _full(kv_cache)] + weight_specs,
        out_specs=[
            pl.BlockSpec((TOKEN_BLOCK, D_MODEL), tok_map),
            pl.BlockSpec((TOP_K, TOKEN_BLOCK), tok_map_t),
            _full((N_EXPERTS, 1)), _full((N_EXPERTS, LANES)),
            pl.BlockSpec((None, WINDOW, KV_WIDTH), stream_map), pl.BlockSpec((None, WINDOW, KV_WIDTH), stream_map),
            _full(kv_cache), _full(kv_cache), _full((T_SAMPLE, SGU_WIDTH)),
            any_spec,
        ],
        out_shape=(
            jax.ShapeDtypeStruct((T_ALL, D_MODEL), F32),
            jax.ShapeDtypeStruct((TOP_K, T_ALL), F32),
            jax.ShapeDtypeStruct((N_EXPERTS, 1), I32),
            jax.ShapeDtypeStruct((N_EXPERTS, LANES), I32),
            jax.ShapeDtypeStruct((BATCH, WINDOW, KV_WIDTH), F32),
            jax.ShapeDtypeStruct((BATCH, WINDOW, KV_WIDTH), F32),
            jax.ShapeDtypeStruct(kv_cache, F32),
            jax.ShapeDtypeStruct(kv_cache, F32),
            jax.ShapeDtypeStruct((T_SAMPLE, SGU_WIDTH), F32),
            jax.ShapeDtypeStruct((XS_PAGES * EXPERT_ROWS * SUBLANES, LANES), I32),
        ),
        scratch_shapes=[
            bf_band, bf_band, bf_band, bf_band,
            state_col, state_col, pltpu.VMEM((1, 1), F32), pltpu.VMEM((N_EXPERTS, LANES), F32),
            pltpu.VMEM((STAGES * TOKEN_BLOCK * SUBLANES, LANES), I32),
            pltpu.VMEM((EXPERT_ROWS * SUBLANES, LANES), I32),
            pltpu.VMEM((STAGES, TOP_K, TOKEN_BLOCK), I32), pltpu.SMEM((STAGES, TOP_K, TOKEN_BLOCK), I32),
            pltpu.VMEM((SUBLANES, LANES), I32), pltpu.SMEM((SUBLANES, LANES), I32),
            pltpu.SemaphoreType.DMA((STAGES,)), pltpu.SemaphoreType.DMA((STAGES,)), pltpu.SemaphoreType.DMA,
        ],
        compiler_params=cparams(dimension_semantics=("arbitrary",)),
        name="mixer",
    )(x_prompt.reshape(T_PROMPT, D_MODEL), x_sample.reshape(T_SAMPLE, D_MODEL),
      cache_k[0].reshape(kv_cache), cache_v[0].reshape(kv_cache), *weights)

    counts = cnt_all[:, 0]
    blocks_per = (counts + EXPERT_ROWS - 1) // EXPERT_ROWS
    block_end = jnp.cumsum(blocks_per)
    block_start = block_end - blocks_per
    n_valid = block_end[-1:].astype(I32)
    step = jnp.minimum(jnp.arange(N_PAGES, dtype=I32), n_valid - 1)
    expert_of_step = step[:, None] >= block_end[None, :]
    blk_e = jnp.minimum(jnp.sum(expert_of_step, axis=1), N_EXPERTS - 1).astype(I32)
    own = (jnp.arange(N_EXPERTS, dtype=I32)[None, :] == blk_e[:, None]).astype(I32)
    blk_in_expert = step - jnp.sum(own * block_start[None, :], axis=1)
    rows_left = jnp.sum(own * counts[None, :], axis=1) - blk_in_expert * EXPERT_ROWS
    blk_rows = jnp.clip(rows_left, 0, EXPERT_ROWS).astype(I32)
    blk_page = table.reshape(-1)[blk_e * LANES + blk_in_expert].astype(I32)
    block_bounds = jnp.concatenate([jnp.zeros((1,), I32), block_end.astype(I32)])

    def expert_map(e, bounds, nr, pg):
        return (e, 0, 0)

    out_kt = pl.pallas_call(
        _expert_kernel,
        grid_spec=pltpu.PrefetchScalarGridSpec(
            num_scalar_prefetch=3,
            grid=(N_EXPERTS,),
            in_specs=[
                any_spec,
                pl.BlockSpec((None, D_MODEL, 2 * D_FF), expert_map),
                pl.BlockSpec((None, 1, 2 * D_FF), expert_map),
                pl.BlockSpec((None, D_FF, D_MODEL), expert_map),
                pl.BlockSpec((None, 1, D_MODEL), expert_map),
            ],
            out_specs=any_spec,
            scratch_shapes=[
                pltpu.VMEM((D_MODEL, 2 * D_FF), BF16), pltpu.VMEM((D_FF, D_MODEL), BF16),
                pltpu.VMEM((2, EXPERT_ROWS * SUBLANES, LANES), I32),
                pltpu.VMEM((STAGES * EXPERT_ROWS * SUBLANES, LANES), F32),
                pltpu.VMEM((STAGES, SUBLANES, EXPERT_ROWS), I32),
                pltpu.SMEM((STAGES, SUBLANES, EXPERT_ROWS), I32),
                pltpu.SemaphoreType.DMA((2,)), pltpu.SemaphoreType.DMA((STAGES,)),
                pltpu.SemaphoreType.DMA((STAGES,)),
            ],
        ),
        out_shape=jax.ShapeDtypeStruct((OUT_ROWS * SUBLANES, LANES), F32),
        compiler_params=cparams(dimension_semantics=("arbitrary",)),
        name="experts",
    )(block_bounds, blk_rows, blk_page, xs, w_gate_up[0], b_gate_up[0].reshape(N_EXPERTS, 1, 2 * D_FF), w_down[0],
      b_down[0].reshape(N_EXPERTS, 1, D_MODEL))

    def slot_spec(kk):
        return pl.BlockSpec((TOKEN_BLOCK * SUBLANES, LANES), lambda i: (kk * N_TOKEN_BLOCKS + i, 0))

    y_p, y_s = pl.pallas_call(
        _combine_kernel,
        grid=(N_TOKEN_BLOCKS,),
        in_specs=[slot_spec(kk) for kk in range(TOP_K)] + [
            pl.BlockSpec((TOKEN_BLOCK, TOP_K), lambda i: (i, 0)),
            pl.BlockSpec((TOKEN_BLOCK, D_MODEL), lambda i: (i, 0)),
            _full((1, D_MODEL)),
        ],
        out_specs=[
            pl.BlockSpec((TOKEN_BLOCK, D_MODEL), lambda i: (jnp.minimum(i, PROMPT_BLOCKS - 1), 0)),
            pl.BlockSpec((TOKEN_BLOCK, D_MODEL), lambda i: (jnp.maximum(i - PROMPT_BLOCKS, 0), 0)),
        ],
        out_shape=(jax.ShapeDtypeStruct((T_PROMPT, D_MODEL), F32), jax.ShapeDtypeStruct((T_SAMPLE, D_MODEL), F32)),
        compiler_params=cparams(dimension_semantics=("arbitrary",)),
        name="combine",
    )(out_kt, out_kt, out_kt, out_kt, gate_all.T, x1_all, fg)

    kv5 = (1, -1, WINDOW, N_KV_HEADS, HEAD_DIM)
    return (y_p.reshape(BATCH, SEQ, D_MODEL), y_s.reshape(DEC_BATCH, DEC_SEQ, D_MODEL),
            kp_new.reshape(kv5), vp_new.reshape(kv5), ks_new.reshape(kv5), vs_new.reshape(kv5),
            gv_s.reshape(1, DEC_BATCH, DEC_SEQ, SGU_HEADS, SGU_HEAD_DIM))
```

```python
import functools

import jax
import jax.numpy as jnp
from jax import lax
from jax.experimental import pallas as pl
from jax.experimental.pallas import tpu as pltpu

D_MODEL = 1024
BATCH = 2
SEQ = 8192
DEC_BATCH = 8
DEC_SEQ = 64
CHUNK = 64
WINDOW = 128
BAND = WINDOW + CHUNK
HEAD_DIM = 64
N_Q_HEADS = 8
N_KV_HEADS = 2
Q_PER_KV = N_Q_HEADS // N_KV_HEADS
ATTN_WIDTH = N_Q_HEADS * HEAD_DIM
KV_WIDTH = N_KV_HEADS * HEAD_DIM
QKV_COLS = ATTN_WIDTH + 2 * KV_WIDTH
SGU_CHUNK = 128
SGU_HEADS = 4
SGU_HEAD_DIM = 128
SGU_WIDTH = SGU_HEADS * SGU_HEAD_DIM
IN_COLS = QKV_COLS + 2 * SGU_WIDTH
N_EXPERTS = 32
TOP_K = 4
D_FF = 1024
SWIGLU_LIMIT = 7.0
SWIGLU_ALPHA = 1.702
RMS_EPS = 1e-5
NEG_INF = -1e30

LANES = 128
SUBLANES = 8
assert D_MODEL == LANES * SUBLANES

T_PROMPT = BATCH * SEQ
T_SAMPLE = DEC_BATCH * DEC_SEQ
T_ALL = T_PROMPT + T_SAMPLE
TOKEN_BLOCK = 512
SEQ_BLOCKS = SEQ // TOKEN_BLOCK
N_TOKEN_BLOCKS = T_ALL // TOKEN_BLOCK
PROMPT_BLOCKS = T_PROMPT // TOKEN_BLOCK
EXPERT_ROWS = 256
N_SLOTS = T_ALL * TOP_K
MIN_PAGES = N_SLOTS // EXPERT_ROWS
N_PAGES = MIN_PAGES + N_EXPERTS
STAGES = 4
BLOCK_COPIES = TOP_K * TOKEN_BLOCK
DUMP_PAGE = N_PAGES
XS_PAGES = N_PAGES + STAGES * BLOCK_COPIES // EXPERT_ROWS
DUMP_ROW = N_SLOTS
OUT_ROWS = DUMP_ROW + STAGES * EXPERT_ROWS
TAG_MASK = 0xFFFF
VMEM_LIMIT_BYTES = 56 * 1024 * 1024
ISSUE_UNROLL = 8

assert T_SAMPLE == TOKEN_BLOCK and SEQ % TOKEN_BLOCK == 0 and N_SLOTS % EXPERT_ROWS == 0
assert T_ALL <= TAG_MASK and N_PAGES <= LANES * N_EXPERTS and T_ALL // EXPERT_ROWS + 1 <= LANES

F32 = jnp.float32
BF16 = jnp.bfloat16
I32 = jnp.int32


def _rms(x, g):
    return x * lax.rsqrt(jnp.mean(x * x, axis=-1, keepdims=True) + RMS_EPS) * g


def _gelu(x):
    return 0.5 * x * (1.0 + lax.erf(x * 0.7071067811865476))


def _store_tile_rows(ref, first_row, val):
    n = val.shape[0]
    for c in range(SUBLANES):
        ref[pl.ds(first_row * SUBLANES + c, n, stride=SUBLANES), :] = val[:, c * LANES:(c + 1) * LANES]


def _load_tile_rows(ref, n):
    return jnp.concatenate([ref[pl.ds(c, n, stride=SUBLANES), :] for c in range(SUBLANES)], axis=1)


def _tile_row(ref, row):
    return ref.at[pl.ds(pl.multiple_of(row * SUBLANES, SUBLANES), SUBLANES), :]


def _tile_rows(ref, first_row, n):
    return ref.at[pl.ds(pl.multiple_of(first_row * SUBLANES, SUBLANES), n * SUBLANES), :]


def _in_proj(x, n1g_ref, w_in_ref, sgng_ref):
    h = _rms(x, n1g_ref[...])
    cols = jnp.dot(h.astype(BF16), w_in_ref[...], preferred_element_type=F32)
    q = cols[:, :ATTN_WIDTH] * (HEAD_DIM ** -0.5)
    k = cols[:, ATTN_WIDTH:ATTN_WIDTH + KV_WIDTH]
    v = cols[:, ATTN_WIDTH + KV_WIDTH:QKV_COLS]
    gu = _gelu(cols[:, QKV_COLS:QKV_COLS + SGU_WIDTH])
    gv = _gelu(cols[:, QKV_COLS + SGU_WIDTH:])
    sg = sgng_ref[...]
    gvn = jnp.concatenate(
        [_rms(gv[:, i * SGU_HEAD_DIM:(i + 1) * SGU_HEAD_DIM], sg[:, i * SGU_HEAD_DIM:(i + 1) * SGU_HEAD_DIM])
         for i in range(SGU_HEADS)], axis=1)
    return q, k, v, gu, gvn


def _dup_heads(t):
    lane = lax.broadcasted_iota(I32, t.shape, 1)
    r = pltpu.roll(t, HEAD_DIM, axis=1)
    return jnp.where(lane < HEAD_DIM, t, r), jnp.where(lane < HEAD_DIM, r, t)


def _attend_chunk(q64, kk, vv, alibi_ref, sink_ref, key_bias):
    lane = lax.broadcasted_iota(I32, (CHUNK, 2 * HEAD_DIM), 1)
    lo = lane < HEAD_DIM
    pairs = []
    for g in range(N_KV_HEADS):
        rows = []
        for p in range(2):
            qp = q64[:, (2 * g + p) * 128:(2 * g + p + 1) * 128]
            rows.append(jnp.where(lo, qp, 0.0))
            rows.append(jnp.where(lo, 0.0, qp))
        q4 = jnp.concatenate(rows, axis=0).astype(BF16)
        sc = lax.dot_general(q4, kk[g], (((1,), (1,)), ((), ())), preferred_element_type=F32)
        sc = sc + alibi_ref[g]
        if key_bias is not None:
            sc = sc + key_bias
        sink = sink_ref[g]
        m = jnp.maximum(jnp.max(sc, axis=1, keepdims=True), sink)
        p_un = jnp.exp(sc - m)
        den = jnp.sum(p_un, axis=1, keepdims=True) + jnp.exp(sink - m)
        o = jnp.dot(p_un.astype(BF16), vv[g], preferred_element_type=F32) / den
        for p in range(2):
            pairs.append(jnp.where(lo, o[(2 * p) * CHUNK:(2 * p + 1) * CHUNK],
                                   o[(2 * p + 1) * CHUNK:(2 * p + 2) * CHUNK]))
    return jnp.concatenate(pairs, axis=1)


def _sgu_weights(sguw_ref):
    sub_i = lax.broadcasted_iota(I32, (SGU_CHUNK, SGU_CHUNK), 0) // CHUNK
    sub_j = lax.broadcasted_iota(I32, (SGU_CHUNK, SGU_CHUNK), 1) // CHUNK
    keep = sub_j <= sub_i
    return [jnp.where(keep, sguw_ref[i], 0.0).astype(BF16) for i in range(SGU_HEADS)]


def _sgu_chunk(ws, gu_c, gvn_c, sgubt_ref):
    outs = []
    for i in range(SGU_HEADS):
        sl = slice(i * SGU_HEAD_DIM, (i + 1) * SGU_HEAD_DIM)
        sp = jnp.dot(ws[i], gvn_c[:, sl].astype(BF16), preferred_element_type=F32) + sgubt_ref[:, i:i + 1]
        outs.append(gu_c[:, sl] * sp)
    return jnp.concatenate(outs, axis=1)


def _merge_route(x, attn, sgu, tok_base, slot, aog_ref, sog_ref, w_out_ref, n2g_ref, wr_ref, br_ref,
                 cnt_scr, cur_scr, free_scr, table_scr, x1_ref, gate_ref, stage, dest_vmem):
    n = x.shape[0]
    a_n = _rms(attn, aog_ref[...]).astype(BF16)
    s_n = _rms(sgu, sog_ref[...]).astype(BF16)
    x1 = (x + jnp.dot(a_n, w_out_ref[:ATTN_WIDTH, :], preferred_element_type=F32)
          + jnp.dot(s_n, w_out_ref[ATTN_WIDTH:, :], preferred_element_type=F32))
    x1_ref[...] = x1
    h2 = _rms(x1, n2g_ref[...]).astype(BF16)
    logits = lax.dot_general(wr_ref[...], h2, (((1,), (1,)), ((), ())), preferred_element_type=F32) + br_ref[...]
    e_iota = lax.broadcasted_iota(I32, (N_EXPERTS, n), 0).astype(F32)
    cur = logits
    vals, idxs, sels = [], [], []
    for _ in range(TOP_K):
        m = jnp.max(cur, axis=0, keepdims=True)
        idx = jnp.min(jnp.where(cur == m, e_iota, float(N_EXPERTS)), axis=0, keepdims=True)
        sel = e_iota == idx
        vals.append(m)
        idxs.append(idx)
        sels.append(sel)
        cur = jnp.where(sel, -jnp.inf, cur)
    exps = [jnp.exp(v - vals[0]) for v in vals]
    den = exps[0] + exps[1] + exps[2] + exps[3]
    gate_ref[...] = jnp.concatenate([e / den for e in exps], axis=0)
    onehot = jnp.where(sels[0] | sels[1] | sels[2] | sels[3], 1.0, 0.0)
    before = lax.broadcasted_iota(I32, (n, n), 0) < lax.broadcasted_iota(I32, (n, n), 1)
    upper = jnp.where(before, 1.0, 0.0).astype(BF16)
    count = cnt_scr[...]
    rank = jnp.dot(onehot.astype(BF16), upper, preferred_element_type=F32) + count
    added = jnp.sum(onehot, axis=1, keepdims=True)
    inv_page = 1.0 / EXPERT_ROWS
    had_pages = jnp.floor((count + (EXPERT_ROWS - 1)) * inv_page)
    need_pages = jnp.floor((count + added + (EXPERT_ROWS - 1)) * inv_page)
    new_pages = need_pages - had_pages
    below = (lax.broadcasted_iota(I32, (N_EXPERTS, N_EXPERTS), 1)
             < lax.broadcasted_iota(I32, (N_EXPERTS, N_EXPERTS), 0))
    first_new = free_scr[...] + jnp.dot(
        jnp.where(below, 1.0, 0.0).astype(BF16), jnp.broadcast_to(new_pages, (N_EXPERTS, LANES)).astype(BF16),
        preferred_element_type=F32)[:, 0:1]
    page_idx = jnp.floor(rank * inv_page)
    page = jnp.where(page_idx < had_pages, cur_scr[...], first_new + (page_idx - had_pages))
    place = page * EXPERT_ROWS + (rank - page_idx * EXPERT_ROWS)
    dest_vmem[slot] = jnp.concatenate(
        [jnp.sum(jnp.where(s, place, 0.0), axis=0, keepdims=True) for s in sels], axis=0).astype(I32)
    lane = lax.broadcasted_iota(I32, (N_EXPERTS, LANES), 1).astype(F32)
    table_scr[...] = jnp.where((lane >= had_pages) & (lane < need_pages), first_new + (lane - had_pages),
                               table_scr[...])
    cur_scr[...] = jnp.where(new_pages > 0, first_new + new_pages - 1.0, cur_scr[...])
    cnt_scr[...] = count + added
    free_scr[...] = free_scr[...] + jnp.sum(new_pages, axis=0, keepdims=True)

    words = lax.bitcast_convert_type(h2.astype(F32), I32)
    tok = (lax.broadcasted_iota(I32, (1, n), 1) + tok_base).astype(F32)
    tag_t = jnp.concatenate([tok] + idxs + [jnp.zeros((LANES - 1 - TOP_K, n), F32)], axis=0)
    tagged = words[:, :LANES] | tag_t.T.astype(I32)
    _store_tile_rows(stage, slot * TOKEN_BLOCK, jnp.concatenate([tagged, words[:, LANES:]], axis=1))


def _mixer_kernel(xp_ref, xs_in_ref, ck_ref, cv_ref, n1g_ref, w_in_ref, sgng_ref, sink_ref, alibi_ref, sguw_ref,
                  sgubt_ref, aog_ref, sog_ref, w_out_ref, n2g_ref, wr_ref, br_ref,
                  x1_ref, gate_ref, cnt_ref, table_ref, kp_ref, vp_ref, ks_ref, vs_ref, gv_ref, xs_ref,
                  kk0, kk1, vv0, vv1, cnt_scr, cur_scr, free_scr, table_scr, stage, zero_page,
                  dest_vmem, dest_smem, meta_vmem, meta_smem, sem_rows, sem_dest, sem_fill):
    n = pl.program_id(0)
    s = n % SEQ_BLOCKS
    slot = n % STAGES
    leaving = (n + STAGES - 2) % STAGES
    block_words = TOKEN_BLOCK * SUBLANES
    weights = (aog_ref, sog_ref, w_out_ref, n2g_ref, wr_ref, br_ref)
    state = (cnt_scr, cur_scr, free_scr, table_scr)

    def dump_row(st):
        return DUMP_PAGE * EXPERT_ROWS + st * BLOCK_COPIES

    def dest_copy(st):
        return pltpu.make_async_copy(dest_vmem.at[st], dest_smem.at[st], sem_dest.at[st])

    def rows_wait(st):
        for _ in range(TOP_K):
            pltpu.make_async_copy(_tile_rows(stage, 0, TOKEN_BLOCK), _tile_rows(xs_ref, 0, TOKEN_BLOCK),
                                  sem_rows.at[st]).wait()

    def row_copy(st, kk, t):
        return pltpu.make_async_copy(_tile_row(stage, st * TOKEN_BLOCK + t), _tile_row(xs_ref, dest_smem[st, kk, t]),
                                     sem_rows.at[st])

    def send_rows(st):
        dest_copy(st).wait()
        for t in range(TOKEN_BLOCK):
            for kk in range(TOP_K):
                row_copy(st, kk, t).start(priority=kk % 2)

    def send_rows_looped(st):
        dest_copy(st).wait()

        def issue(t, carry):
            for kk in range(TOP_K):
                row_copy(st, kk, t).start(priority=kk % 2)
            return carry

        lax.fori_loop(0, TOKEN_BLOCK, issue, 0, unroll=ISSUE_UNROLL)

    @pl.when(n == 0)
    def _():
        for ref in state:
            ref[...] = jnp.zeros_like(ref)
        stage[...] = jnp.zeros_like(stage)
        zero_page[...] = jnp.zeros_like(zero_page)
        for buf in (kk0, kk1, vv0, vv1):
            buf[0:WINDOW, :] = jnp.zeros((WINDOW, 2 * HEAD_DIM), BF16)
        for st in range(STAGES):
            if st < STAGES - 2:
                for kk in range(TOP_K):
                    pltpu.make_async_copy(_tile_rows(stage, st * TOKEN_BLOCK, TOKEN_BLOCK),
                                          _tile_rows(xs_ref, dump_row(st) + kk * TOKEN_BLOCK, TOKEN_BLOCK),
                                          sem_rows.at[st]).start()
            else:
                dest_vmem[st] = (dump_row(st) + lax.broadcasted_iota(I32, (TOP_K, TOKEN_BLOCK), 0) * TOKEN_BLOCK
                                 + lax.broadcasted_iota(I32, (TOP_K, TOKEN_BLOCK), 1))
                dest_copy(st).start()

    @pl.when(n < PROMPT_BLOCKS)
    def _():
        rows_wait(slot)
        send_rows(leaving)
        x = xp_ref[...]
        q, k, v, gu, gvn = _in_proj(x, n1g_ref, w_in_ref, sgng_ref)
        kp_ref[...] = k[TOKEN_BLOCK - WINDOW:, :]
        vp_ref[...] = v[TOKEN_BLOCK - WINDOW:, :]

        k0, k1 = _dup_heads(k)
        v0, v1 = _dup_heads(v)
        for buf, val in ((kk0, k0), (kk1, k1), (vv0, v0), (vv1, v1)):
            buf[WINDOW:, :] = val.astype(BF16)

        attn_chunks = []
        for j in range(TOKEN_BLOCK // CHUNK):
            band = slice(j * CHUNK, j * CHUNK + BAND)
            key_bias = None
            if j * CHUNK < WINDOW:
                key_pos = lax.broadcasted_iota(I32, (1, BAND), 1) + (s * TOKEN_BLOCK + j * CHUNK - WINDOW)
                key_bias = jnp.where(key_pos < 0, NEG_INF, 0.0)
            attn_chunks.append(_attend_chunk(q[j * CHUNK:(j + 1) * CHUNK], (kk0[band, :], kk1[band, :]),
                                             (vv0[band, :], vv1[band, :]), alibi_ref, sink_ref, key_bias))
        attn = jnp.concatenate(attn_chunks, axis=0)

        for buf in (kk0, kk1, vv0, vv1):
            tail = buf[TOKEN_BLOCK:TOKEN_BLOCK + WINDOW, :]
            buf[0:WINDOW, :] = jnp.where(s == SEQ_BLOCKS - 1, jnp.zeros_like(tail), tail)

        ws = _sgu_weights(sguw_ref)
        sgu = jnp.concatenate(
            [_sgu_chunk(ws, gu[c * SGU_CHUNK:(c + 1) * SGU_CHUNK], gvn[c * SGU_CHUNK:(c + 1) * SGU_CHUNK],
                        sgubt_ref) for c in range(TOKEN_BLOCK // SGU_CHUNK)], axis=0)
        _merge_route(x, attn, sgu, n * TOKEN_BLOCK, slot, *weights, *state, x1_ref, gate_ref, stage, dest_vmem)
        dest_copy(slot).start()

    @pl.when(n == PROMPT_BLOCKS)
    def _():
        rows_wait(slot)
        send_rows(leaving)
        x = xs_in_ref[...]
        q, k, v, gu, gvn = _in_proj(x, n1g_ref, w_in_ref, sgng_ref)
        gv_ref[...] = gvn
        ws = _sgu_weights(sguw_ref)
        attn_chunks, sgu_chunks = [], []
        pad = jnp.zeros((SGU_CHUNK - DEC_SEQ, SGU_WIDTH), F32)
        for b in range(DEC_BATCH):
            rows = slice(b * DEC_SEQ, (b + 1) * DEC_SEQ)
            k_all = jnp.concatenate([ck_ref[b], k[rows]], axis=0)
            v_all = jnp.concatenate([cv_ref[b], v[rows]], axis=0)
            ks_ref[b] = k_all[DEC_SEQ:]
            vs_ref[b] = v_all[DEC_SEQ:]
            k0, k1 = _dup_heads(k_all)
            v0, v1 = _dup_heads(v_all)
            attn_chunks.append(_attend_chunk(q[rows], (k0.astype(BF16), k1.astype(BF16)),
                                             (v0.astype(BF16), v1.astype(BF16)), alibi_ref, sink_ref, None))
            gated = _sgu_chunk(ws, jnp.concatenate([gu[rows], pad], axis=0),
                               jnp.concatenate([gvn[rows], pad], axis=0), sgubt_ref)
            sgu_chunks.append(gated[:DEC_SEQ])
        attn = jnp.concatenate(attn_chunks, axis=0)
        sgu = jnp.concatenate(sgu_chunks, axis=0)
        _merge_route(x, attn, sgu, n * TOKEN_BLOCK, slot, *weights, *state, x1_ref, gate_ref, stage, dest_vmem)
        dest_copy(slot).start()

        send_rows_looped((N_TOKEN_BLOCKS - 2) % STAGES)
        send_rows_looped((N_TOKEN_BLOCKS - 1) % STAGES)
        for st in range(STAGES):
            rows_wait(st)
        count = cnt_scr[...]
        left = count - jnp.floor(count * (1.0 / EXPERT_ROWS)) * EXPERT_ROWS
        lane = lax.broadcasted_iota(I32, (N_EXPERTS, LANES), 1)
        meta = jnp.where(lane == 0, cur_scr[...], jnp.where(lane == 1, left, jnp.where(lane == 2, free_scr[...], 0.0)))
        meta_t = jnp.concatenate([meta, jnp.zeros((LANES - N_EXPERTS, LANES), F32)], axis=0).T
        meta_vmem[...] = meta_t[0:SUBLANES].astype(I32)
        to_smem = pltpu.make_async_copy(meta_vmem, meta_smem, sem_dest.at[0])
        to_smem.start()
        to_smem.wait()

        def zero_row_copy(row):
            return pltpu.make_async_copy(_tile_row(zero_page, 0), _tile_row(xs_ref, row), sem_fill)

        n_fill = jnp.int32(0)
        for e in range(N_EXPERTS):
            used = meta_smem[1, e]
            first = jnp.where(used > 0, used, EXPERT_ROWS)
            base = meta_smem[0, e] * EXPERT_ROWS

            def fill(r, carry):
                zero_row_copy(base + r).start()
                return carry

            lax.fori_loop(first, EXPERT_ROWS, fill, 0)
            n_fill = n_fill + (EXPERT_ROWS - first)

        def drain(r, carry):
            zero_row_copy(0).wait()
            return carry

        lax.fori_loop(0, n_fill, drain, 0)
        in_use = meta_smem[2, 0]
        for j in range(N_PAGES - MIN_PAGES):
            @pl.when(in_use + j < N_PAGES)
            def _():
                pltpu.make_async_copy(zero_page, _tile_rows(xs_ref, (in_use + j) * EXPERT_ROWS, EXPERT_ROWS),
                                      sem_fill).start()
        for j in range(N_PAGES - MIN_PAGES):
            @pl.when(in_use + j < N_PAGES)
            def _():
                pltpu.make_async_copy(zero_page, _tile_rows(xs_ref, 0, EXPERT_ROWS), sem_fill).wait()

    cnt_ref[...] = cnt_scr[...].astype(I32)
    table_ref[...] = table_scr[...].astype(I32)


def _expert_kernel(bstart_ref, nrow_ref, page_ref, xs_ref, wgu_ref, bgu_ref, wd_ref, bd_ref, out_ref,
                   wgu_bf, wd_bf, xbuf, stage, ids_vmem, ids_smem, sem_x, sem_rows, sem_ids):
    expert = pl.program_id(0)
    first_block = bstart_ref[expert]
    end_block = bstart_ref[expert + 1]
    nv = bstart_ref[N_EXPERTS]
    block_words = EXPERT_ROWS * SUBLANES

    def x_copy(g, s):
        return pltpu.make_async_copy(_tile_rows(xs_ref, page_ref[g] * EXPERT_ROWS, EXPERT_ROWS), xbuf.at[s],
                                     sem_x.at[s])

    def dump_row(s):
        return DUMP_ROW + s * EXPERT_ROWS

    def ids_copy(s):
        return pltpu.make_async_copy(ids_vmem.at[s], ids_smem.at[s], sem_ids.at[s])

    def rows_wait(s):
        whole = pl.ds(0, block_words)
        pltpu.make_async_copy(stage.at[whole, :], out_ref.at[whole, :], sem_rows.at[s]).wait()

    def issue_rows(s):
        ids_copy(s).wait()
        for r in range(EXPERT_ROWS):
            pltpu.make_async_copy(_tile_row(stage, s * EXPERT_ROWS + r), _tile_row(out_ref, ids_smem[s, 0, r]),
                                  sem_rows.at[s]).start(priority=r % 2)

    def compute(g, xs_slot, s):
        words = _load_tile_rows(xbuf.at[xs_slot], EXPERT_ROWS)
        tag = words[:, :LANES] & TAG_MASK
        x = lax.bitcast_convert_type(words & ~TAG_MASK, F32).astype(BF16)
        gu = jnp.dot(x, wgu_bf[...], preferred_element_type=F32) + bgu_ref[...]
        x_glu = jnp.minimum(gu[:, :D_FF], SWIGLU_LIMIT)
        x_lin = jnp.clip(gu[:, D_FF:], -SWIGLU_LIMIT, SWIGLU_LIMIT)
        act = x_glu * (1.0 / (1.0 + jnp.exp(-SWIGLU_ALPHA * x_glu))) * (x_lin + 1.0)
        out = jnp.dot(act.astype(BF16), wd_bf[...], preferred_element_type=F32) + bd_ref[...]
        _store_tile_rows(stage, s * EXPERT_ROWS, out)
        k_slot = sum(j * jnp.where(tag[:, 1 + j:2 + j] == expert, 1, 0) for j in range(1, TOP_K))
        row = lax.broadcasted_iota(I32, (EXPERT_ROWS, 1), 0)
        dst = jnp.where(row < nrow_ref[g], k_slot * T_ALL + tag[:, 0:1], dump_row(s) + row)
        dst_t = jnp.broadcast_to(dst.astype(F32), (EXPERT_ROWS, LANES)).T
        ids_vmem[s] = dst_t[0:SUBLANES].astype(I32)
        ids_copy(s).start()

    @pl.when(expert == 0)
    def _():
        stage[...] = jnp.zeros_like(stage)
        for s in range(STAGES):
            if s < STAGES - 2:
                pltpu.make_async_copy(stage.at[pl.ds(s * block_words, block_words), :],
                                      out_ref.at[pl.ds(dump_row(s) * SUBLANES, block_words), :],
                                      sem_rows.at[s]).start()
            else:
                ids_vmem[s] = dump_row(s) + lax.broadcasted_iota(I32, (SUBLANES, EXPERT_ROWS), 1)
                ids_copy(s).start()
        x_copy(0, 0).start()

    @pl.when(end_block > first_block)
    def _():
        rows = 128

        def cast_rows(c, carry):
            r = pl.multiple_of(c * rows, rows)
            wgu_bf[pl.ds(r, rows), :] = wgu_ref[pl.ds(r, rows), :].astype(BF16)
            wd_bf[pl.ds(r, rows), :] = wd_ref[pl.ds(r, rows), :].astype(BF16)
            return carry

        lax.fori_loop(0, D_MODEL // rows, cast_rows, 0)

    def block_step(g, carry):
        xs_slot = g % 2
        slot = g % STAGES
        x_copy(g, xs_slot).wait()
        x_copy(jnp.minimum(g + 1, nv - 1), 1 - xs_slot).start()
        rows_wait(slot)
        issue_rows((g + STAGES - 2) % STAGES)
        compute(g, xs_slot, slot)
        return carry

    lax.fori_loop(first_block, end_block, block_step, 0)

    @pl.when(expert == N_EXPERTS - 1)
    def _():
        x_copy(nv - 1, nv % 2).wait()
        issue_rows((nv + STAGES - 2) % STAGES)
        issue_rows((nv + STAGES - 1) % STAGES)
        for s in range(STAGES):
            rows_wait(s)


def _combine_kernel(o0_ref, o1_ref, o2_ref, o3_ref, gate_ref, x1_ref, fg_ref, yp_ref, ys_ref):
    i = pl.program_id(0)
    gates = gate_ref[...]
    y = x1_ref[...]
    for kk, o_ref in enumerate((o0_ref, o1_ref, o2_ref, o3_ref)):
        y = y + gates[:, kk:kk + 1] * _load_tile_rows(o_ref, TOKEN_BLOCK)
    out = _rms(y, fg_ref[...])

    @pl.when(i < PROMPT_BLOCKS)
    def _():
        yp_ref[...] = out

    @pl.when(i >= PROMPT_BLOCKS)
    def _():
        ys_ref[...] = out


def _full(shape):
    return pl.BlockSpec(shape, lambda *_: (0,) * len(shape))


def kernel(x_prompt, x_sample, cache_k, cache_v, norm1_g, w_in, sgu_norm_g, attn_sinks, sgu_w, sgu_b, attn_out_g,
           sgu_out_g, w_out, norm2_g, w_router, b_router, w_gate_up, b_gate_up, w_down, b_down, final_g):
    n1g = norm1_g[0].reshape(1, D_MODEL)
    w_in_b = w_in[0].astype(BF16)
    sgng = sgu_norm_g[0].reshape(1, SGU_WIDTH)
    sink_cols = jnp.repeat(attn_sinks[0].reshape(N_KV_HEADS, Q_PER_KV, 1), CHUNK, axis=1).reshape(
        N_KV_HEADS, Q_PER_KV * CHUNK, 1)
    slopes = jnp.exp2(-8.0 * jnp.arange(1, N_Q_HEADS + 1, dtype=F32) / N_Q_HEADS).reshape(N_KV_HEADS, Q_PER_KV)
    dist = jnp.abs(jnp.arange(CHUNK)[:, None] + WINDOW - jnp.arange(BAND)[None, :]).astype(F32)
    alibi = (-slopes[:, :, None, None] * dist).reshape(N_KV_HEADS, Q_PER_KV * CHUNK, BAND)
    sguw = sgu_w[0]
    sgubt = sgu_b[0].T
    aog = attn_out_g[0].reshape(1, ATTN_WIDTH)
    sog = sgu_out_g[0].reshape(1, SGU_WIDTH)
    w_out_b = w_out[0].astype(BF16)
    n2g = norm2_g[0].reshape(1, D_MODEL)
    wr_t = w_router[0].T.astype(BF16)
    br_c = b_router[0].reshape(N_EXPERTS, 1)
    fg = final_g.reshape(1, D_MODEL)

    weight_specs = [
        _full((1, D_MODEL)), _full((D_MODEL, IN_COLS)), _full((1, SGU_WIDTH)),
        _full((N_KV_HEADS, Q_PER_KV * CHUNK, 1)), _full((N_KV_HEADS, Q_PER_KV * CHUNK, BAND)),
        _full((SGU_HEADS, SGU_CHUNK, SGU_CHUNK)), _full((SGU_CHUNK, SGU_HEADS)),
        _full((1, ATTN_WIDTH)), _full((1, SGU_WIDTH)), _full((D_MODEL, D_MODEL)), _full((1, D_MODEL)),
        _full((N_EXPERTS, D_MODEL)), _full((N_EXPERTS, 1)),
    ]
    weights = (n1g, w_in_b, sgng, sink_cols, alibi, sguw, sgubt, aog, sog, w_out_b, n2g, wr_t, br_c)
    cparams = functools.partial(pltpu.CompilerParams, vmem_limit_bytes=VMEM_LIMIT_BYTES)
    any_spec = pl.BlockSpec(memory_space=pl.ANY)

    def tok_map(n):
        return (n, 0)

    def tok_map_t(n):
        return (0, n)

    def stream_map(n):
        return (jnp.minimum(n // SEQ_BLOCKS, BATCH - 1), 0, 0)

    kv_cache = (DEC_BATCH, WINDOW, KV_WIDTH)
    bf_band = pltpu.VMEM((WINDOW + TOKEN_BLOCK, 2 * HEAD_DIM), BF16)
    state_col = pltpu.VMEM((N_EXPERTS, 1), F32)
    (x1_all, gate_all, cnt_all, table, kp_new, vp_new, ks_new, vs_new, gv_s, xs) = pl.pallas_call(
        _mixer_kernel,
        grid=(N_TOKEN_BLOCKS,),
        in_specs=[pl.BlockSpec((TOKEN_BLOCK, D_MODEL), lambda n: (jnp.minimum(n, PROMPT_BLOCKS - 1), 0)),
                  _full((T_SAMPLE, D_MODEL)), _full(kv_cache), _full(kv_cache)] + weight_specs,
        out_specs=[
            pl.BlockSpec((TOKEN_BLOCK, D_MODEL), tok_map),
            pl.BlockSpec((TOP_K, TOKEN_BLOCK), tok_map_t),
            _full((N_EXPERTS, 1)), _full((N_EXPERTS, LANES)),
            pl.BlockSpec((None, WINDOW, KV_WIDTH), stream_map), pl.BlockSpec((None, WINDOW, KV_WIDTH), stream_map),
            _full(kv_cache), _full(kv_cache), _full((T_SAMPLE, SGU_WIDTH)),
            any_spec,
        ],
        out_shape=(
            jax.ShapeDtypeStruct((T_ALL, D_MODEL), F32),
            jax.ShapeDtypeStruct((TOP_K, T_ALL), F32),
            jax.ShapeDtypeStruct((N_EXPERTS, 1), I32),
            jax.ShapeDtypeStruct((N_EXPERTS, LANES), I32),
            jax.ShapeDtypeStruct((BATCH, WINDOW, KV_WIDTH), F32),
            jax.ShapeDtypeStruct((BATCH, WINDOW, KV_WIDTH), F32),
            jax.ShapeDtypeStruct(kv_cache, F32),
            jax.ShapeDtypeStruct(kv_cache, F32),
            jax.ShapeDtypeStruct((T_SAMPLE, SGU_WIDTH), F32),
            jax.ShapeDtypeStruct((XS_PAGES * EXPERT_ROWS * SUBLANES, LANES), I32),
        ),
        scratch_shapes=[
            bf_band, bf_band, bf_band, bf_band,
            state_col, state_col, pltpu.VMEM((1, 1), F32), pltpu.VMEM((N_EXPERTS, LANES), F32),
            pltpu.VMEM((STAGES * TOKEN_BLOCK * SUBLANES, LANES), I32),
            pltpu.VMEM((EXPERT_ROWS * SUBLANES, LANES), I32),
            pltpu.VMEM((STAGES, TOP_K, TOKEN_BLOCK), I32), pltpu.SMEM((STAGES, TOP_K, TOKEN_BLOCK), I32),
            pltpu.VMEM((SUBLANES, LANES), I32), pltpu.SMEM((SUBLANES, LANES), I32),
            pltpu.SemaphoreType.DMA((STAGES,)), pltpu.SemaphoreType.DMA((STAGES,)), pltpu.SemaphoreType.DMA,
        ],
        compiler_params=cparams(dimension_semantics=("arbitrary",)),
        name="mixer",
    )(x_prompt.reshape(T_PROMPT, D_MODEL), x_sample.reshape(T_SAMPLE, D_MODEL),
      cache_k[0].reshape(kv_cache), cache_v[0].reshape(kv_cache), *weights)

    counts = cnt_all[:, 0]
    blocks_per = (counts + EXPERT_ROWS - 1) // EXPERT_ROWS
    block_end = jnp.cumsum(blocks_per)
    block_start = block_end - blocks_per
    n_valid = block_end[-1:].astype(I32)
    step = jnp.minimum(jnp.arange(N_PAGES, dtype=I32), n_valid - 1)
    expert_of_step = step[:, None] >= block_end[None, :]
    blk_e = jnp.minimum(jnp.sum(expert_of_step, axis=1), N_EXPERTS - 1).astype(I32)
    own = (jnp.arange(N_EXPERTS, dtype=I32)[None, :] == blk_e[:, None]).astype(I32)
    blk_in_expert = step - jnp.sum(own * block_start[None, :], axis=1)
    rows_left = jnp.sum(own * counts[None, :], axis=1) - blk_in_expert * EXPERT_ROWS
    blk_rows = jnp.clip(rows_left, 0, EXPERT_ROWS).astype(I32)
    blk_page = table.reshape(-1)[blk_e * LANES + blk_in_expert].astype(I32)
    block_bounds = jnp.concatenate([jnp.zeros((1,), I32), block_end.astype(I32)])

    def expert_map(e, bounds, nr, pg):
        return (e, 0, 0)

    out_kt = pl.pallas_call(
        _expert_kernel,
        grid_spec=pltpu.PrefetchScalarGridSpec(
            num_scalar_prefetch=3,
            grid=(N_EXPERTS,),
            in_specs=[
                any_spec,
                pl.BlockSpec((None, D_MODEL, 2 * D_FF), expert_map),
                pl.BlockSpec((None, 1, 2 * D_FF), expert_map),
                pl.BlockSpec((None, D_FF, D_MODEL), expert_map),
                pl.BlockSpec((None, 1, D_MODEL), expert_map),
            ],
            out_specs=any_spec,
            scratch_shapes=[
                pltpu.VMEM((D_MODEL, 2 * D_FF), BF16), pltpu.VMEM((D_FF, D_MODEL), BF16),
                pltpu.VMEM((2, EXPERT_ROWS * SUBLANES, LANES), I32),
                pltpu.VMEM((STAGES * EXPERT_ROWS * SUBLANES, LANES), F32),
                pltpu.VMEM((STAGES, SUBLANES, EXPERT_ROWS), I32),
                pltpu.SMEM((STAGES, SUBLANES, EXPERT_ROWS), I32),
                pltpu.SemaphoreType.DMA((2,)), pltpu.SemaphoreType.DMA((STAGES,)),
                pltpu.SemaphoreType.DMA((STAGES,)),
            ],
        ),
        out_shape=jax.ShapeDtypeStruct((OUT_ROWS * SUBLANES, LANES), F32),
        compiler_params=cparams(dimension_semantics=("arbitrary",)),
        name="experts",
    )(block_bounds, blk_rows, blk_page, xs, w_gate_up[0], b_gate_up[0].reshape(N_EXPERTS, 1, 2 * D_FF), w_down[0],
      b_down[0].reshape(N_EXPERTS, 1, D_MODEL))

    def slot_spec(kk):
        return pl.BlockSpec((TOKEN_BLOCK * SUBLANES, LANES), lambda i: (kk * N_TOKEN_BLOCKS + i, 0))

    y_p, y_s = pl.pallas_call(
        _combine_kernel,
        grid=(N_TOKEN_BLOCKS,),
        in_specs=[slot_spec(kk) for kk in range(TOP_K)] + [
            pl.BlockSpec((TOKEN_BLOCK, TOP_K), lambda i: (i, 0)),
            pl.BlockSpec((TOKEN_BLOCK, D_MODEL), lambda i: (i, 0)),
            _full((1, D_MODEL)),
        ],
        out_specs=[
            pl.BlockSpec((TOKEN_BLOCK, D_MODEL), lambda i: (jnp.minimum(i, PROMPT_BLOCKS - 1), 0)),
            pl.BlockSpec((TOKEN_BLOCK, D_MODEL), lambda i: (jnp.maximum(i - PROMPT_BLOCKS, 0), 0)),
        ],
        out_shape=(jax.ShapeDtypeStruct((T_PROMPT, D_MODEL), F32), jax.ShapeDtypeStruct((T_SAMPLE, D_MODEL), F32)),
        compiler_params=cparams(dimension_semantics=("arbitrary",)),
        name="combine",
    )(out_kt, out_kt, out_kt, out_kt, gate_all.T, x1_all, fg)

    kv5 = (1, -1, WINDOW, N_KV_HEADS, HEAD_DIM)
    return (y_p.reshape(BATCH, SEQ, D_MODEL), y_s.reshape(DEC_BATCH, DEC_SEQ, D_MODEL),
            kp_new.reshape(kv5), vp_new.reshape(kv5), ks_new.reshape(kv5), vs_new.reshape(kv5),
            gv_s.reshape(1, DEC_BATCH, DEC_SEQ, SGU_HEADS, SGU_HEAD_DIM))
```

```python
import functools

import jax
import jax.numpy as jnp
from jax import lax
from jax.experimental import pallas as pl
from jax.experimental.pallas import tpu as pltpu

D_MODEL = 1024
BATCH = 2
SEQ = 8192
DEC_BATCH = 8
DEC_SEQ = 64
CHUNK = 64
WINDOW = 128
BAND = WINDOW + CHUNK
HEAD_DIM = 64
N_Q_HEADS = 8
N_KV_HEADS = 2
Q_PER_KV = N_Q_HEADS // N_KV_HEADS
ATTN_WIDTH = N_Q_HEADS * HEAD_DIM
KV_WIDTH = N_KV_HEADS * HEAD_DIM
QKV_COLS = ATTN_WIDTH + 2 * KV_WIDTH
SGU_CHUNK = 128
SGU_HEADS = 4
SGU_HEAD_DIM = 128
SGU_WIDTH = SGU_HEADS * SGU_HEAD_DIM
IN_COLS = QKV_COLS + 2 * SGU_WIDTH
N_EXPERTS = 32
TOP_K = 4
D_FF = 1024
SWIGLU_LIMIT = 7.0
SWIGLU_ALPHA = 1.702
RMS_EPS = 1e-5
NEG_INF = -1e30

LANES = 128
SUBLANES = 8
assert D_MODEL == LANES * SUBLANES

T_PROMPT = BATCH * SEQ
T_SAMPLE = DEC_BATCH * DEC_SEQ
T_ALL = T_PROMPT + T_SAMPLE
TOKEN_BLOCK = 512
SEQ_BLOCKS = SEQ // TOKEN_BLOCK
N_TOKEN_BLOCKS = T_ALL // TOKEN_BLOCK
PROMPT_BLOCKS = T_PROMPT // TOKEN_BLOCK
EXPERT_ROWS = 256
N_SLOTS = T_ALL * TOP_K
MIN_PAGES = N_SLOTS // EXPERT_ROWS
N_PAGES = MIN_PAGES + N_EXPERTS
STAGES = 4
BLOCK_COPIES = TOP_K * TOKEN_BLOCK
DUMP_PAGE = N_PAGES
XS_PAGES = N_PAGES + STAGES * BLOCK_COPIES // EXPERT_ROWS
DUMP_ROW = N_SLOTS
OUT_ROWS = DUMP_ROW + STAGES * EXPERT_ROWS
TAG_MASK = 0xFFFF
VMEM_LIMIT_BYTES = 56 * 1024 * 1024
ISSUE_UNROLL = 8

assert T_SAMPLE == TOKEN_BLOCK and SEQ % TOKEN_BLOCK == 0 and N_SLOTS % EXPERT_ROWS == 0
assert T_ALL <= TAG_MASK and N_PAGES <= LANES * N_EXPERTS and T_ALL // EXPERT_ROWS + 1 <= LANES

F32 = jnp.float32
BF16 = jnp.bfloat16
I32 = jnp.int32


def _rms(x, g):
    return x * lax.rsqrt(jnp.mean(x * x, axis=-1, keepdims=True) + RMS_EPS) * g


def _gelu(x):
    return 0.5 * x * (1.0 + lax.erf(x * 0.7071067811865476))


def _store_tile_rows(ref, first_row, val):
    n = val.shape[0]
    for c in range(SUBLANES):
        ref[pl.ds(first_row * SUBLANES + c, n, stride=SUBLANES), :] = val[:, c * LANES:(c + 1) * LANES]


def _load_tile_rows(ref, n):
    return jnp.concatenate([ref[pl.ds(c, n, stride=SUBLANES), :] for c in range(SUBLANES)], axis=1)


def _tile_row(ref, row):
    return ref.at[pl.ds(pl.multiple_of(row * SUBLANES, SUBLANES), SUBLANES), :]


def _tile_rows(ref, first_row, n):
    return ref.at[pl.ds(pl.multiple_of(first_row * SUBLANES, SUBLANES), n * SUBLANES), :]


def _in_proj(x, n1g_ref, w_in_ref, sgng_ref):
    h = _rms(x, n1g_ref[...])
    cols = jnp.dot(h.astype(BF16), w_in_ref[...], preferred_element_type=F32)
    q = cols[:, :ATTN_WIDTH] * (HEAD_DIM ** -0.5)
    k = cols[:, ATTN_WIDTH:ATTN_WIDTH + KV_WIDTH]
    v = cols[:, ATTN_WIDTH + KV_WIDTH:QKV_COLS]
    gu = _gelu(cols[:, QKV_COLS:QKV_COLS + SGU_WIDTH])
    gv = _gelu(cols[:, QKV_COLS + SGU_WIDTH:])
    sg = sgng_ref[...]
    gvn = jnp.concatenate(
        [_rms(gv[:, i * SGU_HEAD_DIM:(i + 1) * SGU_HEAD_DIM], sg[:, i * SGU_HEAD_DIM:(i + 1) * SGU_HEAD_DIM])
         for i in range(SGU_HEADS)], axis=1)
    return q, k, v, gu, gvn


def _dup_heads(t):
    lane = lax.broadcasted_iota(I32, t.shape, 1)
    r = pltpu.roll(t, HEAD_DIM, axis=1)
    return jnp.where(lane < HEAD_DIM, t, r), jnp.where(lane < HEAD_DIM, r, t)


def _attend_chunk(q64, kk, vv, alibi_ref, sink_ref, key_bias):
    lane = lax.broadcasted_iota(I32, (CHUNK, 2 * HEAD_DIM), 1)
    lo = lane < HEAD_DIM
    pairs = []
    for g in range(N_KV_HEADS):
        rows = []
        for p in range(2):
            qp = q64[:, (2 * g + p) * 128:(2 * g + p + 1) * 128]
            rows.append(jnp.where(lo, qp, 0.0))
            rows.append(jnp.where(lo, 0.0, qp))
        q4 = jnp.concatenate(rows, axis=0).astype(BF16)
        sc = lax.dot_general(q4, kk[g], (((1,), (1,)), ((), ())), preferred_element_type=F32)
        sc = sc + alibi_ref[g]
        if key_bias is not None:
            sc = sc + key_bias
        sink = sink_ref[g]
        m = jnp.maximum(jnp.max(sc, axis=1, keepdims=True), sink)
        p_un = jnp.exp(sc - m)
        den = jnp.sum(p_un, axis=1, keepdims=True) + jnp.exp(sink - m)
        o = jnp.dot(p_un.astype(BF16), vv[g], preferred_element_type=F32) / den
        for p in range(2):
            pairs.append(jnp.where(lo, o[(2 * p) * CHUNK:(2 * p + 1) * CHUNK],
                                   o[(2 * p + 1) * CHUNK:(2 * p + 2) * CHUNK]))
    return jnp.concatenate(pairs, axis=1)


def _sgu_weights(sguw_ref):
    sub_i = lax.broadcasted_iota(I32, (SGU_CHUNK, SGU_CHUNK), 0) // CHUNK
    sub_j = lax.broadcasted_iota(I32, (SGU_CHUNK, SGU_CHUNK), 1) // CHUNK
    keep = sub_j <= sub_i
    return [jnp.where(keep, sguw_ref[i], 0.0).astype(BF16) for i in range(SGU_HEADS)]


def _sgu_chunk(ws, gu_c, gvn_c, sgubt_ref):
    outs = []
    for i in range(SGU_HEADS):
        sl = slice(i * SGU_HEAD_DIM, (i + 1) * SGU_HEAD_DIM)
        sp = jnp.dot(ws[i], gvn_c[:, sl].astype(BF16), preferred_element_type=F32) + sgubt_ref[:, i:i + 1]
        outs.append(gu_c[:, sl] * sp)
    return jnp.concatenate(outs, axis=1)


def _merge_route(x, attn, sgu, tok_base, slot, aog_ref, sog_ref, w_out_ref, n2g_ref, wr_ref, br_ref,
                 cnt_scr, cur_scr, free_scr, table_scr, x1_ref, gate_ref, stage, dest_vmem):
    n = x.shape[0]
    a_n = _rms(attn, aog_ref[...]).astype(BF16)
    s_n = _rms(sgu, sog_ref[...]).astype(BF16)
    x1 = (x + jnp.dot(a_n, w_out_ref[:ATTN_WIDTH, :], preferred_element_type=F32)
          + jnp.dot(s_n, w_out_ref[ATTN_WIDTH:, :], preferred_element_type=F32))
    x1_ref[...] = x1
    h2 = _rms(x1, n2g_ref[...]).astype(BF16)
    logits = lax.dot_general(wr_ref[...], h2, (((1,), (1,)), ((), ())), preferred_element_type=F32) + br_ref[...]
    e_iota = lax.broadcasted_iota(I32, (N_EXPERTS, n), 0).astype(F32)
    cur = logits
    vals, idxs, sels = [], [], []
    for _ in range(TOP_K):
        m = jnp.max(cur, axis=0, keepdims=True)
        idx = jnp.min(jnp.where(cur == m, e_iota, float(N_EXPERTS)), axis=0, keepdims=True)
        sel = e_iota == idx
        vals.append(m)
        idxs.append(idx)
        sels.append(sel)
        cur = jnp.where(sel, -jnp.inf, cur)
    exps = [jnp.exp(v - vals[0]) for v in vals]
    den = exps[0] + exps[1] + exps[2] + exps[3]
    gate_ref[...] = jnp.concatenate([e / den for e in exps], axis=0)
    onehot = jnp.where(sels[0] | sels[1] | sels[2] | sels[3], 1.0, 0.0)
    before = lax.broadcasted_iota(I32, (n, n), 0) < lax.broadcasted_iota(I32, (n, n), 1)
    upper = jnp.where(before, 1.0, 0.0).astype(BF16)
    count = cnt_scr[...]
    rank = jnp.dot(onehot.astype(BF16), upper, preferred_element_type=F32) + count
    added = jnp.sum(onehot, axis=1, keepdims=True)
    inv_page = 1.0 / EXPERT_ROWS
    had_pages = jnp.floor((count + (EXPERT_ROWS - 1)) * inv_page)
    need_pages = jnp.floor((count + added + (EXPERT_ROWS - 1)) * inv_page)
    new_pages = need_pages - had_pages
    below = (lax.broadcasted_iota(I32, (N_EXPERTS, N_EXPERTS), 1)
             < lax.broadcasted_iota(I32, (N_EXPERTS, N_EXPERTS), 0))
    first_new = free_scr[...] + jnp.dot(
        jnp.where(below, 1.0, 0.0).astype(BF16), jnp.broadcast_to(new_pages, (N_EXPERTS, LANES)).astype(BF16),
        preferred_element_type=F32)[:, 0:1]
    page_idx = jnp.floor(rank * inv_page)
    page = jnp.where(page_idx < had_pages, cur_scr[...], first_new + (page_idx - had_pages))
    place = page * EXPERT_ROWS + (rank - page_idx * EXPERT_ROWS)
    dest_vmem[slot] = jnp.concatenate(
        [jnp.sum(jnp.where(s, place, 0.0), axis=0, keepdims=True) for s in sels], axis=0).astype(I32)
    lane = lax.broadcasted_iota(I32, (N_EXPERTS, LANES), 1).astype(F32)
    table_scr[...] = jnp.where((lane >= had_pages) & (lane < need_pages), first_new + (lane - had_pages),
                               table_scr[...])
    cur_scr[...] = jnp.where(new_pages > 0, first_new + new_pages - 1.0, cur_scr[...])
    cnt_scr[...] = count + added
    free_scr[...] = free_scr[...] + jnp.sum(new_pages, axis=0, keepdims=True)

    words = lax.bitcast_convert_type(h2.astype(F32), I32)
    tok = (lax.broadcasted_iota(I32, (1, n), 1) + tok_base).astype(F32)
    tag_t = jnp.concatenate([tok] + idxs + [jnp.zeros((LANES - 1 - TOP_K, n), F32)], axis=0)
    tagged = words[:, :LANES] | tag_t.T.astype(I32)
    _store_tile_rows(stage, slot * TOKEN_BLOCK, jnp.concatenate([tagged, words[:, LANES:]], axis=1))


def _mixer_kernel(xp_ref, xs_in_ref, ck_ref, cv_ref, n1g_ref, w_in_ref, sgng_ref, sink_ref, alibi_ref, sguw_ref,
                  sgubt_ref, aog_ref, sog_ref, w_out_ref, n2g_ref, wr_ref, br_ref,
                  x1_ref, gate_ref, cnt_ref, table_ref, kp_ref, vp_ref, ks_ref, vs_ref, gv_ref, xs_ref,
                  kk0, kk1, vv0, vv1, cnt_scr, cur_scr, free_scr, table_scr, stage, zero_page,
                  dest_vmem, dest_smem, meta_vmem, meta_smem, sem_rows, sem_dest, sem_fill):
    n = pl.program_id(0)
    s = n % SEQ_BLOCKS
    slot = n % STAGES
    leaving = (n + STAGES - 2) % STAGES
    block_words = TOKEN_BLOCK * SUBLANES
    weights = (aog_ref, sog_ref, w_out_ref, n2g_ref, wr_ref, br_ref)
    state = (cnt_scr, cur_scr, free_scr, table_scr)

    def dump_row(st):
        return DUMP_PAGE * EXPERT_ROWS + st * BLOCK_COPIES

    def dest_copy(st):
        return pltpu.make_async_copy(dest_vmem.at[st], dest_smem.at[st], sem_dest.at[st])

    def rows_wait(st):
        for _ in range(TOP_K):
            pltpu.make_async_copy(_tile_rows(stage, 0, TOKEN_BLOCK), _tile_rows(xs_ref, 0, TOKEN_BLOCK),
                                  sem_rows.at[st]).wait()

    def row_copy(st, kk, t):
        return pltpu.make_async_copy(_tile_row(stage, st * TOKEN_BLOCK + t), _tile_row(xs_ref, dest_smem[st, kk, t]),
                                     sem_rows.at[st])

    def send_rows(st):
        dest_copy(st).wait()
        for t in range(TOKEN_BLOCK):
            for kk in range(TOP_K):
                row_copy(st, kk, t).start(priority=kk % 2)

    def send_rows_looped(st):
        dest_copy(st).wait()

        def issue(t, carry):
            for kk in range(TOP_K):
                row_copy(st, kk, t).start(priority=kk % 2)
            return carry

        lax.fori_loop(0, TOKEN_BLOCK, issue, 0, unroll=ISSUE_UNROLL)

    @pl.when(n == 0)
    def _():
        for ref in state:
            ref[...] = jnp.zeros_like(ref)
        stage[...] = jnp.zeros_like(stage)
        zero_page[...] = jnp.zeros_like(zero_page)
        for buf in (kk0, kk1, vv0, vv1):
            buf[0:WINDOW, :] = jnp.zeros((WINDOW, 2 * HEAD_DIM), BF16)
        for st in range(STAGES):
            if st < STAGES - 2:
                for kk in range(TOP_K):
                    pltpu.make_async_copy(_tile_rows(stage, st * TOKEN_BLOCK, TOKEN_BLOCK),
                                          _tile_rows(xs_ref, dump_row(st) + kk * TOKEN_BLOCK, TOKEN_BLOCK),
                                          sem_rows.at[st]).start()
            else:
                dest_vmem[st] = (dump_row(st) + lax.broadcasted_iota(I32, (TOP_K, TOKEN_BLOCK), 0) * TOKEN_BLOCK
                                 + lax.broadcasted_iota(I32, (TOP_K, TOKEN_BLOCK), 1))
                dest_copy(st).start()

    @pl.when(n < PROMPT_BLOCKS)
    def _():
        rows_wait(slot)
        send_rows(leaving)
        x = xp_ref[...]
        q, k, v, gu, gvn = _in_proj(x, n1g_ref, w_in_ref, sgng_ref)
        kp_ref[...] = k[TOKEN_BLOCK - WINDOW:, :]
        vp_ref[...] = v[TOKEN_BLOCK - WINDOW:, :]

        k0, k1 = _dup_heads(k)
        v0, v1 = _dup_heads(v)
        for buf, val in ((kk0, k0), (kk1, k1), (vv0, v0), (vv1, v1)):
            buf[WINDOW:, :] = val.astype(BF16)

        attn_chunks = []
        for j in range(TOKEN_BLOCK // CHUNK):
            band = slice(j * CHUNK, j * CHUNK + BAND)
            key_bias = None
            if j * CHUNK < WINDOW:
                key_pos = lax.broadcasted_iota(I32, (1, BAND), 1) + (s * TOKEN_BLOCK + j * CHUNK - WINDOW)
                key_bias = jnp.where(key_pos < 0, NEG_INF, 0.0)
            attn_chunks.append(_attend_chunk(q[j * CHUNK:(j + 1) * CHUNK], (kk0[band, :], kk1[band, :]),
                                             (vv0[band, :], vv1[band, :]), alibi_ref, sink_ref, key_bias))
        attn = jnp.concatenate(attn_chunks, axis=0)

        for buf in (kk0, kk1, vv0, vv1):
            tail = buf[TOKEN_BLOCK:TOKEN_BLOCK + WINDOW, :]
            buf[0:WINDOW, :] = jnp.where(s == SEQ_BLOCKS - 1, jnp.zeros_like(tail), tail)

        ws = _sgu_weights(sguw_ref)
        sgu = jnp.concatenate(
            [_sgu_chunk(ws, gu[c * SGU_CHUNK:(c + 1) * SGU_CHUNK], gvn[c * SGU_CHUNK:(c + 1) * SGU_CHUNK],
                        sgubt_ref) for c in range(TOKEN_BLOCK // SGU_CHUNK)], axis=0)
        _merge_route(x, attn, sgu, n * TOKEN_BLOCK, slot, *weights, *state, x1_ref, gate_ref, stage, dest_vmem)
        dest_copy(slot).start()

    @pl.when(n == PROMPT_BLOCKS)
    def _():
        rows_wait(slot)
        send_rows(leaving)
        x = xs_in_ref[...]
        q, k, v, gu, gvn = _in_proj(x, n1g_ref, w_in_ref, sgng_ref)
        gv_ref[...] = gvn
        ws = _sgu_weights(sguw_ref)
        attn_chunks, sgu_chunks = [], []
        pad = jnp.zeros((SGU_CHUNK - DEC_SEQ, SGU_WIDTH), F32)
        for b in range(DEC_BATCH):
            rows = slice(b * DEC_SEQ, (b + 1) * DEC_SEQ)
            k_all = jnp.concatenate([ck_ref[b], k[rows]], axis=0)
            v_all = jnp.concatenate([cv_ref[b], v[rows]], axis=0)
            ks_ref[b] = k_all[DEC_SEQ:]
            vs_ref[b] = v_all[DEC_SEQ:]
            k0, k1 = _dup_heads(k_all)
            v0, v1 = _dup_heads(v_all)
            attn_chunks.append(_attend_chunk(q[rows], (k0.astype(BF16), k1.astype(BF16)),
                                             (v0.astype(BF16), v1.astype(BF16)), alibi_ref, sink_ref, None))
            gated = _sgu_chunk(ws, jnp.concatenate([gu[rows], pad], axis=0),
                               jnp.concatenate([gvn[rows], pad], axis=0), sgubt_ref)
            sgu_chunks.append(gated[:DEC_SEQ])
        attn = jnp.concatenate(attn_chunks, axis=0)
        sgu = jnp.concatenate(sgu_chunks, axis=0)
        _merge_route(x, attn, sgu, n * TOKEN_BLOCK, slot, *weights, *state, x1_ref, gate_ref, stage, dest_vmem)
        dest_copy(slot).start()

        send_rows_looped((N_TOKEN_BLOCKS - 2) % STAGES)
        send_rows_looped((N_TOKEN_BLOCKS - 1) % STAGES)
        for st in range(STAGES):
            rows_wait(st)
        count = cnt_scr[...]
        left = count - jnp.floor(count * (1.0 / EXPERT_ROWS)) * EXPERT_ROWS
        lane = lax.broadcasted_iota(I32, (N_EXPERTS, LANES), 1)
        meta = jnp.where(lane == 0, cur_scr[...], jnp.where(lane == 1, left, jnp.where(lane == 2, free_scr[...], 0.0)))
        meta_t = jnp.concatenate([meta, jnp.zeros((LANES - N_EXPERTS, LANES), F32)], axis=0).T
        meta_vmem[...] = meta_t[0:SUBLANES].astype(I32)
        to_smem = pltpu.make_async_copy(meta_vmem, meta_smem, sem_dest.at[0])
        to_smem.start()
        to_smem.wait()

        def zero_row_copy(row):
            return pltpu.make_async_copy(_tile_row(zero_page, 0), _tile_row(xs_ref, row), sem_fill)

        n_fill = jnp.int32(0)
        for e in range(N_EXPERTS):
            used = meta_smem[1, e]
            first = jnp.where(used > 0, used, EXPERT_ROWS)
            base = meta_smem[0, e] * EXPERT_ROWS

            def fill(r, carry):
                zero_row_copy(base + r).start()
                return carry

            lax.fori_loop(first, EXPERT_ROWS, fill, 0)
            n_fill = n_fill + (EXPERT_ROWS - first)

        def drain(r, carry):
            zero_row_copy(0).wait()
            return carry

        lax.fori_loop(0, n_fill, drain, 0)
        in_use = meta_smem[2, 0]
        for j in range(N_PAGES - MIN_PAGES):
            @pl.when(in_use + j < N_PAGES)
            def _():
                pltpu.make_async_copy(zero_page, _tile_rows(xs_ref, (in_use + j) * EXPERT_ROWS, EXPERT_ROWS),
                                      sem_fill).start()
        for j in range(N_PAGES - MIN_PAGES):
            @pl.when(in_use + j < N_PAGES)
            def _():
                pltpu.make_async_copy(zero_page, _tile_rows(xs_ref, 0, EXPERT_ROWS), sem_fill).wait()

    cnt_ref[...] = cnt_scr[...].astype(I32)
    table_ref[...] = table_scr[...].astype(I32)


def _expert_kernel(bstart_ref, nrow_ref, page_ref, xs_ref, wgu_ref, bgu_ref, wd_ref, bd_ref, out_ref,
                   wgu_bf, wd_bf, xbuf, stage, ids_vmem, ids_smem, sem_x, sem_rows, sem_ids):
    expert = pl.program_id(0)
    first_block = bstart_ref[expert]
    end_block = bstart_ref[expert + 1]
    nv = bstart_ref[N_EXPERTS]
    block_words = EXPERT_ROWS * SUBLANES

    def x_copy(g, s):
        return pltpu.make_async_copy(_tile_rows(xs_ref, page_ref[g] * EXPERT_ROWS, EXPERT_ROWS), xbuf.at[s],
                                     sem_x.at[s])

    def dump_row(s):
        return DUMP_ROW + s * EXPERT_ROWS

    def ids_copy(s):
        return pltpu.make_async_copy(ids_vmem.at[s], ids_smem.at[s], sem_ids.at[s])

    def rows_wait(s):
        whole = pl.ds(0, block_words)
        pltpu.make_async_copy(stage.at[whole, :], out_ref.at[whole, :], sem_rows.at[s]).wait()

    def issue_rows(s):
        ids_copy(s).wait()
        for r in range(EXPERT_ROWS):
            pltpu.make_async_copy(_tile_row(stage, s * EXPERT_ROWS + r), _tile_row(out_ref, ids_smem[s, 0, r]),
                                  sem_rows.at[s]).start(priority=r % 2)

    def compute(g, xs_slot, s):
        words = _load_tile_rows(xbuf.at[xs_slot], EXPERT_ROWS)
        tag = words[:, :LANES] & TAG_MASK
        x = lax.bitcast_convert_type(words & ~TAG_MASK, F32).astype(BF16)
        gu = jnp.dot(x, wgu_bf[...], preferred_element_type=F32) + bgu_ref[...]
        x_glu = jnp.minimum(gu[:, :D_FF], SWIGLU_LIMIT)
        x_lin = jnp.clip(gu[:, D_FF:], -SWIGLU_LIMIT, SWIGLU_LIMIT)
        act = x_glu * (1.0 / (1.0 + jnp.exp(-SWIGLU_ALPHA * x_glu))) * (x_lin + 1.0)
        out = jnp.dot(act.astype(BF16), wd_bf[...], preferred_element_type=F32) + bd_ref[...]
        _store_tile_rows(stage, s * EXPERT_ROWS, out)
        k_slot = sum(j * jnp.where(tag[:, 1 + j:2 + j] == expert, 1, 0) for j in range(1, TOP_K))
        row = lax.broadcasted_iota(I32, (EXPERT_ROWS, 1), 0)
        dst = jnp.where(row < nrow_ref[g], k_slot * T_ALL + tag[:, 0:1], dump_row(s) + row)
        dst_t = jnp.broadcast_to(dst.astype(F32), (EXPERT_ROWS, LANES)).T
        ids_vmem[s] = dst_t[0:SUBLANES].astype(I32)
        ids_copy(s).start()

    @pl.when(expert == 0)
    def _():
        stage[...] = jnp.zeros_like(stage)
        for s in range(STAGES):
            if s < STAGES - 2:
                pltpu.make_async_copy(stage.at[pl.ds(s * block_words, block_words), :],
                                      out_ref.at[pl.ds(dump_row(s) * SUBLANES, block_words), :],
                                      sem_rows.at[s]).start()
            else:
                ids_vmem[s] = dump_row(s) + lax.broadcasted_iota(I32, (SUBLANES, EXPERT_ROWS), 1)
                ids_copy(s).start()
        x_copy(0, 0).start(priority=1)

    @pl.when(end_block > first_block)
    def _():
        rows = 128

        def cast_rows(c, carry):
            r = pl.multiple_of(c * rows, rows)
            wgu_bf[pl.ds(r, rows), :] = wgu_ref[pl.ds(r, rows), :].astype(BF16)
            wd_bf[pl.ds(r, rows), :] = wd_ref[pl.ds(r, rows), :].astype(BF16)
            return carry

        lax.fori_loop(0, D_MODEL // rows, cast_rows, 0)

    def block_step(g, carry):
        xs_slot = g % 2
        slot = g % STAGES
        x_copy(g, xs_slot).wait()
        x_copy(jnp.minimum(g + 1, nv - 1), 1 - xs_slot).start(priority=1)
        rows_wait(slot)
        issue_rows((g + STAGES - 2) % STAGES)
        compute(g, xs_slot, slot)
        return carry

    lax.fori_loop(first_block, end_block, block_step, 0)

    @pl.when(expert == N_EXPERTS - 1)
    def _():
        x_copy(nv - 1, nv % 2).wait()
        issue_rows((nv + STAGES - 2) % STAGES)
        issue_rows((nv + STAGES - 1) % STAGES)
        for s in range(STAGES):
            rows_wait(s)


def _combine_kernel(o0_ref, o1_ref, o2_ref, o3_ref, gate_ref, x1_ref, fg_ref, yp_ref, ys_ref):
    i = pl.program_id(0)
    gates = gate_ref[...]
    y = x1_ref[...]
    for kk, o_ref in enumerate((o0_ref, o1_ref, o2_ref, o3_ref)):
        y = y + gates[:, kk:kk + 1] * _load_tile_rows(o_ref, TOKEN_BLOCK)
    out = _rms(y, fg_ref[...])

    @pl.when(i < PROMPT_BLOCKS)
    def _():
        yp_ref[...] = out

    @pl.when(i >= PROMPT_BLOCKS)
    def _():
        ys_ref[...] = out


def _full(shape):
    return pl.BlockSpec(shape, lambda *_: (0,) * len(shape))


def kernel(x_prompt, x_sample, cache_k, cache_v, norm1_g, w_in, sgu_norm_g, attn_sinks, sgu_w, sgu_b, attn_out_g,
           sgu_out_g, w_out, norm2_g, w_router, b_router, w_gate_up, b_gate_up, w_down, b_down, final_g):
    n1g = norm1_g[0].reshape(1, D_MODEL)
    w_in_b = w_in[0].astype(BF16)
    sgng = sgu_norm_g[0].reshape(1, SGU_WIDTH)
    sink_cols = jnp.repeat(attn_sinks[0].reshape(N_KV_HEADS, Q_PER_KV, 1), CHUNK, axis=1).reshape(
        N_KV_HEADS, Q_PER_KV * CHUNK, 1)
    slopes = jnp.exp2(-8.0 * jnp.arange(1, N_Q_HEADS + 1, dtype=F32) / N_Q_HEADS).reshape(N_KV_HEADS, Q_PER_KV)
    dist = jnp.abs(jnp.arange(CHUNK)[:, None] + WINDOW - jnp.arange(BAND)[None, :]).astype(F32)
    alibi = (-slopes[:, :, None, None] * dist).reshape(N_KV_HEADS, Q_PER_KV * CHUNK, BAND)
    sguw = sgu_w[0]
    sgubt = sgu_b[0].T
    aog = attn_out_g[0].reshape(1, ATTN_WIDTH)
    sog = sgu_out_g[0].reshape(1, SGU_WIDTH)
    w_out_b = w_out[0].astype(BF16)
    n2g = norm2_g[0].reshape(1, D_MODEL)
    wr_t = w_router[0].T.astype(BF16)
    br_c = b_router[0].reshape(N_EXPERTS, 1)
    fg = final_g.reshape(1, D_MODEL)

    weight_specs = [
        _full((1, D_MODEL)), _full((D_MODEL, IN_COLS)), _full((1, SGU_WIDTH)),
        _full((N_KV_HEADS, Q_PER_KV * CHUNK, 1)), _full((N_KV_HEADS, Q_PER_KV * CHUNK, BAND)),
        _full((SGU_HEADS, SGU_CHUNK, SGU_CHUNK)), _full((SGU_CHUNK, SGU_HEADS)),
        _full((1, ATTN_WIDTH)), _full((1, SGU_WIDTH)), _full((D_MODEL, D_MODEL)), _full((1, D_MODEL)),
        _full((N_EXPERTS, D_MODEL)), _full((N_EXPERTS, 1)),
    ]
    weights = (n1g, w_in_b, sgng, sink_cols, alibi, sguw, sgubt, aog, sog, w_out_b, n2g, wr_t, br_c)
    cparams = functools.partial(pltpu.CompilerParams, vmem_limit_bytes=VMEM_LIMIT_BYTES)
    any_spec = pl.BlockSpec(memory_space=pl.ANY)

    def tok_map(n):
        return (n, 0)

    def tok_map_t(n):
        return (0, n)

    def stream_map(n):
        return (jnp.minimum(n // SEQ_BLOCKS, BATCH - 1), 0, 0)

    kv_cache = (DEC_BATCH, WINDOW, KV_WIDTH)
    bf_band = pltpu.VMEM((WINDOW + TOKEN_BLOCK, 2 * HEAD_DIM), BF16)
    state_col = pltpu.VMEM((N_EXPERTS, 1), F32)
    (x1_all, gate_all, cnt_all, table, kp_new, vp_new, ks_new, vs_new, gv_s, xs) = pl.pallas_call(
        _mixer_kernel,
        grid=(N_TOKEN_BLOCKS,),
        in_specs=[pl.BlockSpec((TOKEN_BLOCK, D_MODEL), lambda n: (jnp.minimum(n, PROMPT_BLOCKS - 1), 0)),
                  _full((T_SAMPLE, D_MODEL)), _full(kv_cache), _full(kv_cache)] + weight_specs,
        out_specs=[
            pl.BlockSpec((TOKEN_BLOCK, D_MODEL), tok_map),
            pl.BlockSpec((TOP_K, TOKEN_BLOCK), tok_map_t),
            _full((N_EXPERTS, 1)), _full((N_EXPERTS, LANES)),
            pl.BlockSpec((None, WINDOW, KV_WIDTH), stream_map), pl.BlockSpec((None, WINDOW, KV_WIDTH), stream_map),
            _full(kv_cache), _full(kv_cache), _full((T_SAMPLE, SGU_WIDTH)),
            any_spec,
        ],
        out_shape=(
            jax.ShapeDtypeStruct((T_ALL, D_MODEL), F32),
            jax.ShapeDtypeStruct((TOP_K, T_ALL), F32),
            jax.ShapeDtypeStruct((N_EXPERTS, 1), I32),
            jax.ShapeDtypeStruct((N_EXPERTS, LANES), I32),
            jax.ShapeDtypeStruct((BATCH, WINDOW, KV_WIDTH), F32),
            jax.ShapeDtypeStruct((BATCH, WINDOW, KV_WIDTH), F32),
            jax.ShapeDtypeStruct(kv_cache, F32),
            jax.ShapeDtypeStruct(kv_cache, F32),
            jax.ShapeDtypeStruct((T_SAMPLE, SGU_WIDTH), F32),
            jax.ShapeDtypeStruct((XS_PAGES * EXPERT_ROWS * SUBLANES, LANES), I32),
        ),
        scratch_shapes=[
            bf_band, bf_band, bf_band, bf_band,
            state_col, state_col, pltpu.VMEM((1, 1), F32), pltpu.VMEM((N_EXPERTS, LANES), F32),
            pltpu.VMEM((STAGES * TOKEN_BLOCK * SUBLANES, LANES), I32),
            pltpu.VMEM((EXPERT_ROWS * SUBLANES, LANES), I32),
            pltpu.VMEM((STAGES, TOP_K, TOKEN_BLOCK), I32), pltpu.SMEM((STAGES, TOP_K, TOKEN_BLOCK), I32),
            pltpu.VMEM((SUBLANES, LANES), I32), pltpu.SMEM((SUBLANES, LANES), I32),
            pltpu.SemaphoreType.DMA((STAGES,)), pltpu.SemaphoreType.DMA((STAGES,)), pltpu.SemaphoreType.DMA,
        ],
        compiler_params=cparams(dimension_semantics=("arbitrary",)),
        name="mixer",
    )(x_prompt.reshape(T_PROMPT, D_MODEL), x_sample.reshape(T_SAMPLE, D_MODEL),
      cache_k[0].reshape(kv_cache), cache_v[0].reshape(kv_cache), *weights)

    counts = cnt_all[:, 0]
    blocks_per = (counts + EXPERT_ROWS - 1) // EXPERT_ROWS
    block_end = jnp.cumsum(blocks_per)
    block_start = block_end - blocks_per
    n_valid = block_end[-1:].astype(I32)
    step = jnp.minimum(jnp.arange(N_PAGES, dtype=I32), n_valid - 1)
    expert_of_step = step[:, None] >= block_end[None, :]
    blk_e = jnp.minimum(jnp.sum(expert_of_step, axis=1), N_EXPERTS - 1).astype(I32)
    own = (jnp.arange(N_EXPERTS, dtype=I32)[None, :] == blk_e[:, None]).astype(I32)
    blk_in_expert = step - jnp.sum(own * block_start[None, :], axis=1)
    rows_left = jnp.sum(own * counts[None, :], axis=1) - blk_in_expert * EXPERT_ROWS
    blk_rows = jnp.clip(rows_left, 0, EXPERT_ROWS).astype(I32)
    blk_page = table.reshape(-1)[blk_e * LANES + blk_in_expert].astype(I32)
    block_bounds = jnp.concatenate([jnp.zeros((1,), I32), block_end.astype(I32)])

    def expert_map(e, bounds, nr, pg):
        return (e, 0, 0)

    out_kt = pl.pallas_call(
        _expert_kernel,
        grid_spec=pltpu.PrefetchScalarGridSpec(
            num_scalar_prefetch=3,
            grid=(N_EXPERTS,),
            in_specs=[
                any_spec,
                pl.BlockSpec((None, D_MODEL, 2 * D_FF), expert_map),
                pl.BlockSpec((None, 1, 2 * D_FF), expert_map),
                pl.BlockSpec((None, D_FF, D_MODEL), expert_map),
                pl.BlockSpec((None, 1, D_MODEL), expert_map),
            ],
            out_specs=any_spec,
            scratch_shapes=[
                pltpu.VMEM((D_MODEL, 2 * D_FF), BF16), pltpu.VMEM((D_FF, D_MODEL), BF16),
                pltpu.VMEM((2, EXPERT_ROWS * SUBLANES, LANES), I32),
                pltpu.VMEM((STAGES * EXPERT_ROWS * SUBLANES, LANES), F32),
                pltpu.VMEM((STAGES, SUBLANES, EXPERT_ROWS), I32),
                pltpu.SMEM((STAGES, SUBLANES, EXPERT_ROWS), I32),
                pltpu.SemaphoreType.DMA((2,)), pltpu.SemaphoreType.DMA((STAGES,)),
                pltpu.SemaphoreType.DMA((STAGES,)),
            ],
        ),
        out_shape=jax.ShapeDtypeStruct((OUT_ROWS * SUBLANES, LANES), F32),
        compiler_params=cparams(dimension_semantics=("arbitrary",)),
        name="experts",
    )(block_bounds, blk_rows, blk_page, xs, w_gate_up[0], b_gate_up[0].reshape(N_EXPERTS, 1, 2 * D_FF), w_down[0],
      b_down[0].reshape(N_EXPERTS, 1, D_MODEL))

    def slot_spec(kk):
        return pl.BlockSpec((TOKEN_BLOCK * SUBLANES, LANES), lambda i: (kk * N_TOKEN_BLOCKS + i, 0))

    y_p, y_s = pl.pallas_call(
        _combine_kernel,
        grid=(N_TOKEN_BLOCKS,),
        in_specs=[slot_spec(kk) for kk in range(TOP_K)] + [
            pl.BlockSpec((TOKEN_BLOCK, TOP_K), lambda i: (i, 0)),
            pl.BlockSpec((TOKEN_BLOCK, D_MODEL), lambda i: (i, 0)),
            _full((1, D_MODEL)),
        ],
        out_specs=[
            pl.BlockSpec((TOKEN_BLOCK, D_MODEL), lambda i: (jnp.minimum(i, PROMPT_BLOCKS - 1), 0)),
            pl.BlockSpec((TOKEN_BLOCK, D_MODEL), lambda i: (jnp.maximum(i - PROMPT_BLOCKS, 0), 0)),
        ],
        out_shape=(jax.ShapeDtypeStruct((T_PROMPT, D_MODEL), F32), jax.ShapeDtypeStruct((T_SAMPLE, D_MODEL), F32)),
        compiler_params=cparams(dimension_semantics=("arbitrary",)),
        name="combine",
    )(out_kt, out_kt, out_kt, out_kt, gate_all.T, x1_all, fg)

    kv5 = (1, -1, WINDOW, N_KV_HEADS, HEAD_DIM)
    return (y_p.reshape(BATCH, SEQ, D_MODEL), y_s.reshape(DEC_BATCH, DEC_SEQ, D_MODEL),
            kp_new.reshape(kv5), vp_new.reshape(kv5), ks_new.reshape(kv5), vs_new.reshape(kv5),
            gv_s.reshape(1, DEC_BATCH, DEC_SEQ, SGU_HEADS, SGU_HEAD_DIM))
```

```python
import functools

import jax
import jax.numpy as jnp
from jax import lax
from jax.experimental import pallas as pl
from jax.experimental.pallas import tpu as pltpu

D_MODEL = 1024
BATCH = 2
SEQ = 8192
DEC_BATCH = 8
DEC_SEQ = 64
CHUNK = 64
WINDOW = 128
BAND = WINDOW + CHUNK
PAIR_BAND = BAND + CHUNK
HEAD_DIM = 64
N_Q_HEADS = 8
N_KV_HEADS = 2
Q_PER_KV = N_Q_HEADS // N_KV_HEADS
ATTN_WIDTH = N_Q_HEADS * HEAD_DIM
KV_WIDTH = N_KV_HEADS * HEAD_DIM
QKV_COLS = ATTN_WIDTH + 2 * KV_WIDTH
SGU_CHUNK = 128
SGU_HEADS = 4
SGU_HEAD_DIM = 128
SGU_WIDTH = SGU_HEADS * SGU_HEAD_DIM
IN_COLS = QKV_COLS + 2 * SGU_WIDTH
N_EXPERTS = 32
TOP_K = 4
D_FF = 1024
SWIGLU_LIMIT = 7.0
SWIGLU_ALPHA = 1.702
RMS_EPS = 1e-5
NEG_INF = -1e30

LANES = 128
SUBLANES = 8
assert D_MODEL == LANES * SUBLANES

T_PROMPT = BATCH * SEQ
T_SAMPLE = DEC_BATCH * DEC_SEQ
T_ALL = T_PROMPT + T_SAMPLE
TOKEN_BLOCK = 512
SEQ_BLOCKS = SEQ // TOKEN_BLOCK
N_TOKEN_BLOCKS = T_ALL // TOKEN_BLOCK
PROMPT_BLOCKS = T_PROMPT // TOKEN_BLOCK
EXPERT_ROWS = 256
N_SLOTS = T_ALL * TOP_K
MIN_PAGES = N_SLOTS // EXPERT_ROWS
N_PAGES = MIN_PAGES + N_EXPERTS
STAGES = 4
BLOCK_COPIES = TOP_K * TOKEN_BLOCK
DUMP_PAGE = N_PAGES
XS_PAGES = N_PAGES + STAGES * BLOCK_COPIES // EXPERT_ROWS
DUMP_ROW = N_SLOTS
OUT_ROWS = DUMP_ROW + STAGES * EXPERT_ROWS
TAG_MASK = 0xFFFF
VMEM_LIMIT_BYTES = 56 * 1024 * 1024
ISSUE_UNROLL = 8

assert T_SAMPLE == TOKEN_BLOCK and SEQ % TOKEN_BLOCK == 0 and N_SLOTS % EXPERT_ROWS == 0
assert T_ALL <= TAG_MASK and N_PAGES <= LANES * N_EXPERTS and T_ALL // EXPERT_ROWS + 1 <= LANES

F32 = jnp.float32
BF16 = jnp.bfloat16
I32 = jnp.int32


def _rms(x, g):
    return x * lax.rsqrt(jnp.mean(x * x, axis=-1, keepdims=True) + RMS_EPS) * g


def _gelu(x):
    return 0.5 * x * (1.0 + lax.erf(x * 0.7071067811865476))


def _store_tile_rows(ref, first_row, val):
    n = val.shape[0]
    for c in range(SUBLANES):
        ref[pl.ds(first_row * SUBLANES + c, n, stride=SUBLANES), :] = val[:, c * LANES:(c + 1) * LANES]


def _load_tile_rows(ref, n):
    return jnp.concatenate([ref[pl.ds(c, n, stride=SUBLANES), :] for c in range(SUBLANES)], axis=1)


def _tile_row(ref, row):
    return ref.at[pl.ds(pl.multiple_of(row * SUBLANES, SUBLANES), SUBLANES), :]


def _tile_rows(ref, first_row, n):
    return ref.at[pl.ds(pl.multiple_of(first_row * SUBLANES, SUBLANES), n * SUBLANES), :]


def _in_proj(x, n1g_ref, w_in_ref, sgng_ref):
    h = _rms(x, n1g_ref[...])
    cols = jnp.dot(h.astype(BF16), w_in_ref[...], preferred_element_type=F32)
    q = cols[:, :ATTN_WIDTH] * (HEAD_DIM ** -0.5)
    k = cols[:, ATTN_WIDTH:ATTN_WIDTH + KV_WIDTH]
    v = cols[:, ATTN_WIDTH + KV_WIDTH:QKV_COLS]
    gu = _gelu(cols[:, QKV_COLS:QKV_COLS + SGU_WIDTH])
    gv = _gelu(cols[:, QKV_COLS + SGU_WIDTH:])
    sg = sgng_ref[...]
    gvn = jnp.concatenate(
        [_rms(gv[:, i * SGU_HEAD_DIM:(i + 1) * SGU_HEAD_DIM], sg[:, i * SGU_HEAD_DIM:(i + 1) * SGU_HEAD_DIM])
         for i in range(SGU_HEADS)], axis=1)
    return q, k, v, gu, gvn


def _dup_heads(t):
    lane = lax.broadcasted_iota(I32, t.shape, 1)
    r = pltpu.roll(t, HEAD_DIM, axis=1)
    return jnp.where(lane < HEAD_DIM, t, r), jnp.where(lane < HEAD_DIM, r, t)


def _attend_chunks(q_chunks, kk, vv, alibi_ref, sink_ref, key_bias):
    lane = lax.broadcasted_iota(I32, (CHUNK, 2 * HEAD_DIM), 1)
    lo = lane < HEAD_DIM
    rows_per_chunk = Q_PER_KV * CHUNK
    pairs = [[] for _ in q_chunks]
    for g in range(N_KV_HEADS):
        rows = []
        for q64 in q_chunks:
            for p in range(2):
                qp = q64[:, (2 * g + p) * 128:(2 * g + p + 1) * 128]
                rows.append(jnp.where(lo, qp, 0.0))
                rows.append(jnp.where(lo, 0.0, qp))
        q4 = jnp.concatenate(rows, axis=0).astype(BF16)
        sc = lax.dot_general(q4, kk[g], (((1,), (1,)), ((), ())), preferred_element_type=F32)
        sc = sc + alibi_ref[g]
        if key_bias is not None:
            sc = sc + key_bias
        sink = sink_ref[g]
        m = jnp.maximum(jnp.max(sc, axis=1, keepdims=True), sink)
        p_un = jnp.exp(sc - m)
        den = jnp.sum(p_un, axis=1, keepdims=True) + jnp.exp(sink - m)
        o = jnp.dot(p_un.astype(BF16), vv[g], preferred_element_type=F32) / den
        for c in range(len(q_chunks)):
            for p in range(2):
                first = c * rows_per_chunk + 2 * p * CHUNK
                pairs[c].append(jnp.where(lo, o[first:first + CHUNK], o[first + CHUNK:first + 2 * CHUNK]))
    return [jnp.concatenate(chunk_pairs, axis=1) for chunk_pairs in pairs]


def _sgu_weights(sguw_ref):
    sub_i = lax.broadcasted_iota(I32, (SGU_CHUNK, SGU_CHUNK), 0) // CHUNK
    sub_j = lax.broadcasted_iota(I32, (SGU_CHUNK, SGU_CHUNK), 1) // CHUNK
    keep = sub_j <= sub_i
    return [jnp.where(keep, sguw_ref[i], 0.0).astype(BF16) for i in range(SGU_HEADS)]


def _sgu_chunk(ws, gu_c, gvn_c, sgubt_ref):
    outs = []
    for i in range(SGU_HEADS):
        sl = slice(i * SGU_HEAD_DIM, (i + 1) * SGU_HEAD_DIM)
        sp = jnp.dot(ws[i], gvn_c[:, sl].astype(BF16), preferred_element_type=F32) + sgubt_ref[:, i:i + 1]
        outs.append(gu_c[:, sl] * sp)
    return jnp.concatenate(outs, axis=1)


def _merge_route(x, attn, sgu, tok_base, slot, aog_ref, sog_ref, w_out_ref, n2g_ref, wr_ref, br_ref, earlier_ref,
                 cnt_scr, cur_scr, free_scr, table_scr, x1_ref, gate_ref, stage, dest_vmem):
    n = x.shape[0]
    a_n = _rms(attn, aog_ref[...]).astype(BF16)
    s_n = _rms(sgu, sog_ref[...]).astype(BF16)
    x1 = (x + jnp.dot(a_n, w_out_ref[:ATTN_WIDTH, :], preferred_element_type=F32)
          + jnp.dot(s_n, w_out_ref[ATTN_WIDTH:, :], preferred_element_type=F32))
    x1_ref[...] = x1
    h2 = _rms(x1, n2g_ref[...]).astype(BF16)
    logits = lax.dot_general(wr_ref[...], h2, (((1,), (1,)), ((), ())), preferred_element_type=F32) + br_ref[...]
    e_iota = lax.broadcasted_iota(I32, (N_EXPERTS, n), 0).astype(F32)
    cur = logits
    vals, idxs, sels = [], [], []
    for _ in range(TOP_K):
        m = jnp.max(cur, axis=0, keepdims=True)
        idx = jnp.min(jnp.where(cur == m, e_iota, float(N_EXPERTS)), axis=0, keepdims=True)
        sel = e_iota == idx
        vals.append(m)
        idxs.append(idx)
        sels.append(sel)
        cur = jnp.where(sel, -jnp.inf, cur)
    exps = [jnp.exp(v - vals[0]) for v in vals]
    den = exps[0] + exps[1] + exps[2] + exps[3]
    gate_ref[...] = jnp.concatenate([e / den for e in exps], axis=0)
    onehot = jnp.where(sels[0] | sels[1] | sels[2] | sels[3], 1.0, 0.0)
    count = cnt_scr[...]
    rank = jnp.dot(onehot.astype(BF16), earlier_ref[...], preferred_element_type=F32) + count
    added = jnp.sum(onehot, axis=1, keepdims=True)
    inv_page = 1.0 / EXPERT_ROWS
    had_pages = jnp.floor((count + (EXPERT_ROWS - 1)) * inv_page)
    need_pages = jnp.floor((count + added + (EXPERT_ROWS - 1)) * inv_page)
    new_pages = need_pages - had_pages
    below = (lax.broadcasted_iota(I32, (N_EXPERTS, N_EXPERTS), 1)
             < lax.broadcasted_iota(I32, (N_EXPERTS, N_EXPERTS), 0))
    first_new = free_scr[...] + jnp.dot(
        jnp.where(below, 1.0, 0.0).astype(BF16), jnp.broadcast_to(new_pages, (N_EXPERTS, LANES)).astype(BF16),
        preferred_element_type=F32)[:, 0:1]
    page_idx = jnp.floor(rank * inv_page)
    page = jnp.where(page_idx < had_pages, cur_scr[...], first_new + (page_idx - had_pages))
    place = page * EXPERT_ROWS + (rank - page_idx * EXPERT_ROWS)
    dest_vmem[slot] = jnp.concatenate(
        [jnp.sum(jnp.where(s, place, 0.0), axis=0, keepdims=True) for s in sels], axis=0).astype(I32)
    lane = lax.broadcasted_iota(I32, (N_EXPERTS, LANES), 1).astype(F32)
    table_scr[...] = jnp.where((lane >= had_pages) & (lane < need_pages), first_new + (lane - had_pages),
                               table_scr[...])
    cur_scr[...] = jnp.where(new_pages > 0, first_new + new_pages - 1.0, cur_scr[...])
    cnt_scr[...] = count + added
    free_scr[...] = free_scr[...] + jnp.sum(new_pages, axis=0, keepdims=True)

    words = lax.bitcast_convert_type(h2.astype(F32), I32)
    tok = (lax.broadcasted_iota(I32, (1, n), 1) + tok_base).astype(F32)
    tag_t = jnp.concatenate([tok] + idxs + [jnp.zeros((LANES - 1 - TOP_K, n), F32)], axis=0)
    tagged = words[:, :LANES] | tag_t.T.astype(I32)
    _store_tile_rows(stage, slot * TOKEN_BLOCK, jnp.concatenate([tagged, words[:, LANES:]], axis=1))


def _mixer_kernel(xp_ref, xs_in_ref, ck_ref, cv_ref, n1g_ref, w_in_ref, sgng_ref, sink_ref, alibi_ref, sink2_ref,
                  alibi2_ref, sguw_ref, sgubt_ref, aog_ref, sog_ref, w_out_ref, n2g_ref, wr_ref, br_ref, earlier_ref,
                  x1_ref, gate_ref, cnt_ref, table_ref, kp_ref, vp_ref, ks_ref, vs_ref, gv_ref, xs_ref,
                  kk0, kk1, vv0, vv1, cnt_scr, cur_scr, free_scr, table_scr, stage, zero_page,
                  dest_vmem, dest_smem, meta_vmem, meta_smem, sem_rows, sem_dest, sem_fill):
    n = pl.program_id(0)
    s = n % SEQ_BLOCKS
    slot = n % STAGES
    leaving = (n + STAGES - 2) % STAGES
    weights = (aog_ref, sog_ref, w_out_ref, n2g_ref, wr_ref, br_ref, earlier_ref)
    state = (cnt_scr, cur_scr, free_scr, table_scr)

    def dump_row(st):
        return DUMP_PAGE * EXPERT_ROWS + st * BLOCK_COPIES

    def dest_copy(st):
        return pltpu.make_async_copy(dest_vmem.at[st], dest_smem.at[st], sem_dest.at[st])

    def rows_wait(st):
        for _ in range(TOP_K):
            pltpu.make_async_copy(_tile_rows(stage, 0, TOKEN_BLOCK), _tile_rows(xs_ref, 0, TOKEN_BLOCK),
                                  sem_rows.at[st]).wait()

    def row_copy(st, kk, t):
        return pltpu.make_async_copy(_tile_row(stage, st * TOKEN_BLOCK + t), _tile_row(xs_ref, dest_smem[st, kk, t]),
                                     sem_rows.at[st])

    def send_rows(st):
        dest_copy(st).wait()
        for t in range(TOKEN_BLOCK):
            for kk in range(TOP_K):
                row_copy(st, kk, t).start(priority=kk % 2)

    def send_rows_looped(st):
        dest_copy(st).wait()

        def issue(t, carry):
            for kk in range(TOP_K):
                row_copy(st, kk, t).start(priority=kk % 2)
            return carry

        lax.fori_loop(0, TOKEN_BLOCK, issue, 0, unroll=ISSUE_UNROLL)

    @pl.when(n == 0)
    def _():
        for ref in state:
            ref[...] = jnp.zeros_like(ref)
        stage[...] = jnp.zeros_like(stage)
        zero_page[...] = jnp.zeros_like(zero_page)
        for buf in (kk0, kk1, vv0, vv1):
            buf[0:WINDOW, :] = jnp.zeros((WINDOW, 2 * HEAD_DIM), BF16)
        for st in range(STAGES):
            if st < STAGES - 2:
                for kk in range(TOP_K):
                    pltpu.make_async_copy(_tile_rows(stage, st * TOKEN_BLOCK, TOKEN_BLOCK),
                                          _tile_rows(xs_ref, dump_row(st) + kk * TOKEN_BLOCK, TOKEN_BLOCK),
                                          sem_rows.at[st]).start()
            else:
                dest_vmem[st] = (dump_row(st) + lax.broadcasted_iota(I32, (TOP_K, TOKEN_BLOCK), 0) * TOKEN_BLOCK
                                 + lax.broadcasted_iota(I32, (TOP_K, TOKEN_BLOCK), 1))
                dest_copy(st).start()

    @pl.when(n < PROMPT_BLOCKS)
    def _():
        rows_wait(slot)
        send_rows(leaving)
        x = xp_ref[...]
        q, k, v, gu, gvn = _in_proj(x, n1g_ref, w_in_ref, sgng_ref)
        kp_ref[...] = k[TOKEN_BLOCK - WINDOW:, :]
        vp_ref[...] = v[TOKEN_BLOCK - WINDOW:, :]

        k0, k1 = _dup_heads(k)
        v0, v1 = _dup_heads(v)
        for buf, val in ((kk0, k0), (kk1, k1), (vv0, v0), (vv1, v1)):
            buf[WINDOW:, :] = val.astype(BF16)

        attn_chunks = []
        for j in range(0, TOKEN_BLOCK // CHUNK, 2):
            band = slice(j * CHUNK, j * CHUNK + PAIR_BAND)
            key_bias = None
            if j * CHUNK < WINDOW:
                key_pos = lax.broadcasted_iota(I32, (1, PAIR_BAND), 1) + (s * TOKEN_BLOCK + j * CHUNK - WINDOW)
                key_bias = jnp.where(key_pos < 0, NEG_INF, 0.0)
            attn_chunks += _attend_chunks([q[j * CHUNK:(j + 1) * CHUNK], q[(j + 1) * CHUNK:(j + 2) * CHUNK]],
                                          (kk0[band, :], kk1[band, :]), (vv0[band, :], vv1[band, :]),
                                          alibi2_ref, sink2_ref, key_bias)
        attn = jnp.concatenate(attn_chunks, axis=0)

        for buf in (kk0, kk1, vv0, vv1):
            tail = buf[TOKEN_BLOCK:TOKEN_BLOCK + WINDOW, :]
            buf[0:WINDOW, :] = jnp.where(s == SEQ_BLOCKS - 1, jnp.zeros_like(tail), tail)

        ws = _sgu_weights(sguw_ref)
        sgu = jnp.concatenate(
            [_sgu_chunk(ws, gu[c * SGU_CHUNK:(c + 1) * SGU_CHUNK], gvn[c * SGU_CHUNK:(c + 1) * SGU_CHUNK],
                        sgubt_ref) for c in range(TOKEN_BLOCK // SGU_CHUNK)], axis=0)
        _merge_route(x, attn, sgu, n * TOKEN_BLOCK, slot, *weights, *state, x1_ref, gate_ref, stage, dest_vmem)
        dest_copy(slot).start()

    @pl.when(n == PROMPT_BLOCKS)
    def _():
        rows_wait(slot)
        send_rows(leaving)
        x = xs_in_ref[...]
        q, k, v, gu, gvn = _in_proj(x, n1g_ref, w_in_ref, sgng_ref)
        gv_ref[...] = gvn
        ws = _sgu_weights(sguw_ref)
        attn_chunks, sgu_chunks = [], []
        pad = jnp.zeros((SGU_CHUNK - DEC_SEQ, SGU_WIDTH), F32)
        for b in range(DEC_BATCH):
            rows = slice(b * DEC_SEQ, (b + 1) * DEC_SEQ)
            k_all = jnp.concatenate([ck_ref[b], k[rows]], axis=0)
            v_all = jnp.concatenate([cv_ref[b], v[rows]], axis=0)
            ks_ref[b] = k_all[DEC_SEQ:]
            vs_ref[b] = v_all[DEC_SEQ:]
            k0, k1 = _dup_heads(k_all)
            v0, v1 = _dup_heads(v_all)
            attn_chunks += _attend_chunks([q[rows]], (k0.astype(BF16), k1.astype(BF16)),
                                          (v0.astype(BF16), v1.astype(BF16)), alibi_ref, sink_ref, None)
            gated = _sgu_chunk(ws, jnp.concatenate([gu[rows], pad], axis=0),
                               jnp.concatenate([gvn[rows], pad], axis=0), sgubt_ref)
            sgu_chunks.append(gated[:DEC_SEQ])
        attn = jnp.concatenate(attn_chunks, axis=0)
        sgu = jnp.concatenate(sgu_chunks, axis=0)
        _merge_route(x, attn, sgu, n * TOKEN_BLOCK, slot, *weights, *state, x1_ref, gate_ref, stage, dest_vmem)
        dest_copy(slot).start()

        send_rows_looped((N_TOKEN_BLOCKS - 2) % STAGES)
        send_rows_looped((N_TOKEN_BLOCKS - 1) % STAGES)
        for st in range(STAGES):
            rows_wait(st)
        count = cnt_scr[...]
        left = count - jnp.floor(count * (1.0 / EXPERT_ROWS)) * EXPERT_ROWS
        lane = lax.broadcasted_iota(I32, (N_EXPERTS, LANES), 1)
        meta = jnp.where(lane == 0, cur_scr[...], jnp.where(lane == 1, left, jnp.where(lane == 2, free_scr[...], 0.0)))
        meta_t = jnp.concatenate([meta, jnp.zeros((LANES - N_EXPERTS, LANES), F32)], axis=0).T
        meta_vmem[...] = meta_t[0:SUBLANES].astype(I32)
        to_smem = pltpu.make_async_copy(meta_vmem, meta_smem, sem_dest.at[0])
        to_smem.start()
        to_smem.wait()

        def zero_row_copy(row):
            return pltpu.make_async_copy(_tile_row(zero_page, 0), _tile_row(xs_ref, row), sem_fill)

        n_fill = jnp.int32(0)
        for e in range(N_EXPERTS):
            used = meta_smem[1, e]
            first = jnp.where(used > 0, used, EXPERT_ROWS)
            base = meta_smem[0, e] * EXPERT_ROWS

            def fill(r, carry):
                zero_row_copy(base + r).start()
                return carry

            lax.fori_loop(first, EXPERT_ROWS, fill, 0)
            n_fill = n_fill + (EXPERT_ROWS - first)

        def drain(r, carry):
            zero_row_copy(0).wait()
            return carry

        lax.fori_loop(0, n_fill, drain, 0)
        in_use = meta_smem[2, 0]
        for j in range(N_PAGES - MIN_PAGES):
            @pl.when(in_use + j < N_PAGES)
            def _():
                pltpu.make_async_copy(zero_page, _tile_rows(xs_ref, (in_use + j) * EXPERT_ROWS, EXPERT_ROWS),
                                      sem_fill).start()
        for j in range(N_PAGES - MIN_PAGES):
            @pl.when(in_use + j < N_PAGES)
            def _():
                pltpu.make_async_copy(zero_page, _tile_rows(xs_ref, 0, EXPERT_ROWS), sem_fill).wait()

    cnt_ref[...] = cnt_scr[...].astype(I32)
    table_ref[...] = table_scr[...].astype(I32)


def _expert_kernel(bstart_ref, nrow_ref, page_ref, xs_ref, wgu_ref, bgu_ref, wd_ref, bd_ref, out_ref,
                   wgu_bf, wd_bf, xbuf, stage, ids_vmem, ids_smem, sem_x, sem_rows, sem_ids):
    expert = pl.program_id(0)
    first_block = bstart_ref[expert]
    end_block = bstart_ref[expert + 1]
    nv = bstart_ref[N_EXPERTS]
    block_words = EXPERT_ROWS * SUBLANES

    def x_copy(g, s):
        return pltpu.make_async_copy(_tile_rows(xs_ref, page_ref[g] * EXPERT_ROWS, EXPERT_ROWS), xbuf.at[s],
                                     sem_x.at[s])

    def dump_row(s):
        return DUMP_ROW + s * EXPERT_ROWS

    def ids_copy(s):
        return pltpu.make_async_copy(ids_vmem.at[s], ids_smem.at[s], sem_ids.at[s])

    def rows_wait(s):
        whole = pl.ds(0, block_words)
        pltpu.make_async_copy(stage.at[whole, :], out_ref.at[whole, :], sem_rows.at[s]).wait()

    def issue_rows(s):
        ids_copy(s).wait()
        for r in range(EXPERT_ROWS):
            pltpu.make_async_copy(_tile_row(stage, s * EXPERT_ROWS + r), _tile_row(out_ref, ids_smem[s, 0, r]),
                                  sem_rows.at[s]).start(priority=r % 2)

    def compute(g, xs_slot, s):
        words = _load_tile_rows(xbuf.at[xs_slot], EXPERT_ROWS)
        tag = words[:, :LANES] & TAG_MASK
        x = lax.bitcast_convert_type(words & ~TAG_MASK, F32).astype(BF16)
        gu = jnp.dot(x, wgu_bf[...], preferred_element_type=F32) + bgu_ref[...]
        x_glu = jnp.minimum(gu[:, :D_FF], SWIGLU_LIMIT)
        x_lin = jnp.clip(gu[:, D_FF:], -SWIGLU_LIMIT, SWIGLU_LIMIT)
        act = x_glu * (1.0 / (1.0 + jnp.exp(-SWIGLU_ALPHA * x_glu))) * (x_lin + 1.0)
        out = jnp.dot(act.astype(BF16), wd_bf[...], preferred_element_type=F32) + bd_ref[...]
        _store_tile_rows(stage, s * EXPERT_ROWS, out)
        k_slot = sum(j * jnp.where(tag[:, 1 + j:2 + j] == expert, 1, 0) for j in range(1, TOP_K))
        row = lax.broadcasted_iota(I32, (EXPERT_ROWS, 1), 0)
        dst = jnp.where(row < nrow_ref[g], k_slot * T_ALL + tag[:, 0:1], dump_row(s) + row)
        dst_t = jnp.broadcast_to(dst.astype(F32), (EXPERT_ROWS, LANES)).T
        ids_vmem[s] = dst_t[0:SUBLANES].astype(I32)
        ids_copy(s).start()

    @pl.when(expert == 0)
    def _():
        stage[...] = jnp.zeros_like(stage)
        for s in range(STAGES):
            if s < STAGES - 2:
                pltpu.make_async_copy(stage.at[pl.ds(s * block_words, block_words), :],
                                      out_ref.at[pl.ds(dump_row(s) * SUBLANES, block_words), :],
                                      sem_rows.at[s]).start()
            else:
                ids_vmem[s] = dump_row(s) + lax.broadcasted_iota(I32, (SUBLANES, EXPERT_ROWS), 1)
                ids_copy(s).start()
        x_copy(0, 0).start()

    @pl.when(end_block > first_block)
    def _():
        rows = 128

        def cast_rows(c, carry):
            r = pl.multiple_of(c * rows, rows)
            wgu_bf[pl.ds(r, rows), :] = wgu_ref[pl.ds(r, rows), :].astype(BF16)
            wd_bf[pl.ds(r, rows), :] = wd_ref[pl.ds(r, rows), :].astype(BF16)
            return carry

        lax.fori_loop(0, D_MODEL // rows, cast_rows, 0)

    def block_step(g, carry):
        xs_slot = g % 2
        slot = g % STAGES
        x_copy(g, xs_slot).wait()
        x_copy(jnp.minimum(g + 1, nv - 1), 1 - xs_slot).start()
        rows_wait(slot)
        issue_rows((g + STAGES - 2) % STAGES)
        compute(g, xs_slot, slot)
        return carry

    lax.fori_loop(first_block, end_block, block_step, 0)

    @pl.when(expert == N_EXPERTS - 1)
    def _():
        x_copy(nv - 1, nv % 2).wait()
        issue_rows((nv + STAGES - 2) % STAGES)
        issue_rows((nv + STAGES - 1) % STAGES)
        for s in range(STAGES):
            rows_wait(s)


def _combine_kernel(o0_ref, o1_ref, o2_ref, o3_ref, gate_ref, x1_ref, fg_ref, yp_ref, ys_ref):
    i = pl.program_id(0)
    gates = jnp.concatenate([gate_ref[...], jnp.zeros((LANES - TOP_K, TOKEN_BLOCK), F32)], axis=0).T
    y = x1_ref[...]
    for kk, o_ref in enumerate((o0_ref, o1_ref, o2_ref, o3_ref)):
        y = y + gates[:, kk:kk + 1] * _load_tile_rows(o_ref, TOKEN_BLOCK)
    out = _rms(y, fg_ref[...])

    @pl.when(i < PROMPT_BLOCKS)
    def _():
        yp_ref[...] = out

    @pl.when(i >= PROMPT_BLOCKS)
    def _():
        ys_ref[...] = out


def _full(shape):
    return pl.BlockSpec(shape, lambda *_: (0,) * len(shape))


def kernel(x_prompt, x_sample, cache_k, cache_v, norm1_g, w_in, sgu_norm_g, attn_sinks, sgu_w, sgu_b, attn_out_g,
           sgu_out_g, w_out, norm2_g, w_router, b_router, w_gate_up, b_gate_up, w_down, b_down, final_g):
    n1g = norm1_g[0].reshape(1, D_MODEL)
    w_in_b = w_in[0].astype(BF16)
    sgng = sgu_norm_g[0].reshape(1, SGU_WIDTH)
    sink_cols = jnp.repeat(attn_sinks[0].reshape(N_KV_HEADS, Q_PER_KV, 1), CHUNK, axis=1).reshape(
        N_KV_HEADS, Q_PER_KV * CHUNK, 1)
    slopes = jnp.exp2(-8.0 * jnp.arange(1, N_Q_HEADS + 1, dtype=F32) / N_Q_HEADS).reshape(N_KV_HEADS, Q_PER_KV)
    dist = jnp.abs(jnp.arange(CHUNK)[:, None] + WINDOW - jnp.arange(BAND)[None, :]).astype(F32)
    alibi = (-slopes[:, :, None, None] * dist).reshape(N_KV_HEADS, Q_PER_KV * CHUNK, BAND)
    sguw = sgu_w[0]
    sgubt = sgu_b[0].T
    aog = attn_out_g[0].reshape(1, ATTN_WIDTH)
    sog = sgu_out_g[0].reshape(1, SGU_WIDTH)
    w_out_b = w_out[0].astype(BF16)
    n2g = norm2_g[0].reshape(1, D_MODEL)
    wr_t = w_router[0].T.astype(BF16)
    br_c = b_router[0].reshape(N_EXPERTS, 1)
    fg = final_g.reshape(1, D_MODEL)

    weight_specs = [
        _full((1, D_MODEL)), _full((D_MODEL, IN_COLS)), _full((1, SGU_WIDTH)),
        _full((N_KV_HEADS, Q_PER_KV * CHUNK, 1)), _full((N_KV_HEADS, Q_PER_KV * CHUNK, BAND)),
        _full((N_KV_HEADS, 2 * Q_PER_KV * CHUNK, 1)), _full((N_KV_HEADS, 2 * Q_PER_KV * CHUNK, PAIR_BAND)),
        _full((SGU_HEADS, SGU_CHUNK, SGU_CHUNK)), _full((SGU_CHUNK, SGU_HEADS)),
        _full((1, ATTN_WIDTH)), _full((1, SGU_WIDTH)), _full((D_MODEL, D_MODEL)), _full((1, D_MODEL)),
        _full((N_EXPERTS, D_MODEL)), _full((N_EXPERTS, 1)), _full((TOKEN_BLOCK, TOKEN_BLOCK)),
    ]
    own_key = jnp.arange(PAIR_BAND)[None, None, :] - CHUNK * jnp.arange(2)[:, None, None]
    in_band = (own_key >= 0) & (own_key < BAND)
    dist2 = jnp.abs(jnp.arange(CHUNK)[None, :, None] + WINDOW - own_key).astype(F32)
    alibi2 = jnp.where(in_band[None, None], -slopes[:, :, None, None, None] * dist2[None, None], NEG_INF)
    alibi2 = alibi2.transpose(0, 2, 1, 3, 4).reshape(N_KV_HEADS, 2 * Q_PER_KV * CHUNK, PAIR_BAND)
    sink2 = jnp.concatenate([sink_cols, sink_cols], axis=1)
    token = jnp.arange(TOKEN_BLOCK)
    earlier = (token[:, None] < token[None, :]).astype(BF16)
    weights = (n1g, w_in_b, sgng, sink_cols, alibi, sink2, alibi2, sguw, sgubt, aog, sog, w_out_b, n2g, wr_t, br_c,
               earlier)
    cparams = functools.partial(pltpu.CompilerParams, vmem_limit_bytes=VMEM_LIMIT_BYTES)
    any_spec = pl.BlockSpec(memory_space=pl.ANY)

    def tok_map(n):
        return (n, 0)

    def tok_map_t(n):
        return (0, n)

    def stream_map(n):
        return (jnp.minimum(n // SEQ_BLOCKS, BATCH - 1), 0, 0)

    kv_cache = (DEC_BATCH, WINDOW, KV_WIDTH)
    bf_band = pltpu.VMEM((WINDOW + TOKEN_BLOCK, 2 * HEAD_DIM), BF16)
    state_col = pltpu.VMEM((N_EXPERTS, 1), F32)
    (x1_all, gate_all, cnt_all, table, kp_new, vp_new, ks_new, vs_new, gv_s, xs) = pl.pallas_call(
        _mixer_kernel,
        grid=(N_TOKEN_BLOCKS,),
        in_specs=[pl.BlockSpec((TOKEN_BLOCK, D_MODEL), lambda n: (jnp.minimum(n, PROMPT_BLOCKS - 1), 0)),
                  _full((T_SAMPLE, D_MODEL)), _full(kv_cache), _full(kv_cache)] + weight_specs,
        out_specs=[
            pl.BlockSpec((TOKEN_BLOCK, D_MODEL), tok_map),
            pl.BlockSpec((TOP_K, TOKEN_BLOCK), tok_map_t),
            _full((N_EXPERTS, 1)), _full((N_EXPERTS, LANES)),
            pl.BlockSpec((None, WINDOW, KV_WIDTH), stream_map), pl.BlockSpec((None, WINDOW, KV_WIDTH), stream_map),
            _full(kv_cache), _full(kv_cache), _full((T_SAMPLE, SGU_WIDTH)),
            any_spec,
        ],
        out_shape=(
            jax.ShapeDtypeStruct((T_ALL, D_MODEL), F32),
            jax.ShapeDtypeStruct((TOP_K, T_ALL), F32),
            jax.ShapeDtypeStruct((N_EXPERTS, 1), I32),
            jax.ShapeDtypeStruct((N_EXPERTS, LANES), I32),
            jax.ShapeDtypeStruct((BATCH, WINDOW, KV_WIDTH), F32),
            jax.ShapeDtypeStruct((BATCH, WINDOW, KV_WIDTH), F32),
            jax.ShapeDtypeStruct(kv_cache, F32),
            jax.ShapeDtypeStruct(kv_cache, F32),
            jax.ShapeDtypeStruct((T_SAMPLE, SGU_WIDTH), F32),
            jax.ShapeDtypeStruct((XS_PAGES * EXPERT_ROWS * SUBLANES, LANES), I32),
        ),
        scratch_shapes=[
            bf_band, bf_band, bf_band, bf_band,
            state_col, state_col, pltpu.VMEM((1, 1), F32), pltpu.VMEM((N_EXPERTS, LANES), F32),
            pltpu.VMEM((STAGES * TOKEN_BLOCK * SUBLANES, LANES), I32),
            pltpu.VMEM((EXPERT_ROWS * SUBLANES, LANES), I32),
            pltpu.VMEM((STAGES, TOP_K, TOKEN_BLOCK), I32), pltpu.SMEM((STAGES, TOP_K, TOKEN_BLOCK), I32),
            pltpu.VMEM((SUBLANES, LANES), I32), pltpu.SMEM((SUBLANES, LANES), I32),
            pltpu.SemaphoreType.DMA((STAGES,)), pltpu.SemaphoreType.DMA((STAGES,)), pltpu.SemaphoreType.DMA,
        ],
        compiler_params=cparams(dimension_semantics=("arbitrary",)),
        name="mixer",
    )(x_prompt.reshape(T_PROMPT, D_MODEL), x_sample.reshape(T_SAMPLE, D_MODEL),
      cache_k[0].reshape(kv_cache), cache_v[0].reshape(kv_cache), *weights)

    counts = cnt_all[:, 0]
    blocks_per = (counts + EXPERT_ROWS - 1) // EXPERT_ROWS
    block_end = jnp.cumsum(blocks_per)
    block_start = block_end - blocks_per
    n_valid = block_end[-1:].astype(I32)
    step = jnp.minimum(jnp.arange(N_PAGES, dtype=I32), n_valid - 1)
    expert_of_step = step[:, None] >= block_end[None, :]
    blk_e = jnp.minimum(jnp.sum(expert_of_step, axis=1), N_EXPERTS - 1).astype(I32)
    own = (jnp.arange(N_EXPERTS, dtype=I32)[None, :] == blk_e[:, None]).astype(I32)
    blk_in_expert = step - jnp.sum(own * block_start[None, :], axis=1)
    rows_left = jnp.sum(own * counts[None, :], axis=1) - blk_in_expert * EXPERT_ROWS
    blk_rows = jnp.clip(rows_left, 0, EXPERT_ROWS).astype(I32)
    blk_page = table.reshape(-1)[blk_e * LANES + blk_in_expert].astype(I32)
    block_bounds = jnp.concatenate([jnp.zeros((1,), I32), block_end.astype(I32)])

    def expert_map(e, bounds, nr, pg):
        return (e, 0, 0)

    out_kt = pl.pallas_call(
        _expert_kernel,
        grid_spec=pltpu.PrefetchScalarGridSpec(
            num_scalar_prefetch=3,
            grid=(N_EXPERTS,),
            in_specs=[
                any_spec,
                pl.BlockSpec((None, D_MODEL, 2 * D_FF), expert_map),
                pl.BlockSpec((None, 1, 2 * D_FF), expert_map),
                pl.BlockSpec((None, D_FF, D_MODEL), expert_map),
                pl.BlockSpec((None, 1, D_MODEL), expert_map),
            ],
            out_specs=any_spec,
            scratch_shapes=[
                pltpu.VMEM((D_MODEL, 2 * D_FF), BF16), pltpu.VMEM((D_FF, D_MODEL), BF16),
                pltpu.VMEM((2, EXPERT_ROWS * SUBLANES, LANES), I32),
                pltpu.VMEM((STAGES * EXPERT_ROWS * SUBLANES, LANES), F32),
                pltpu.VMEM((STAGES, SUBLANES, EXPERT_ROWS), I32),
                pltpu.SMEM((STAGES, SUBLANES, EXPERT_ROWS), I32),
                pltpu.SemaphoreType.DMA((2,)), pltpu.SemaphoreType.DMA((STAGES,)),
                pltpu.SemaphoreType.DMA((STAGES,)),
            ],
        ),
        out_shape=jax.ShapeDtypeStruct((OUT_ROWS * SUBLANES, LANES), F32),
        compiler_params=cparams(dimension_semantics=("arbitrary",)),
        name="experts",
    )(block_bounds, blk_rows, blk_page, xs, w_gate_up[0], b_gate_up[0].reshape(N_EXPERTS, 1, 2 * D_FF), w_down[0],
      b_down[0].reshape(N_EXPERTS, 1, D_MODEL))

    def slot_spec(kk):
        return pl.BlockSpec((TOKEN_BLOCK * SUBLANES, LANES), lambda i: (kk * N_TOKEN_BLOCKS + i, 0))

    y_p, y_s = pl.pallas_call(
        _combine_kernel,
        grid=(N_TOKEN_BLOCKS,),
        in_specs=[slot_spec(kk) for kk in range(TOP_K)] + [
            pl.BlockSpec((TOP_K, TOKEN_BLOCK), lambda i: (0, i)),
            pl.BlockSpec((TOKEN_BLOCK, D_MODEL), lambda i: (i, 0)),
            _full((1, D_MODEL)),
        ],
        out_specs=[
            pl.BlockSpec((TOKEN_BLOCK, D_MODEL), lambda i: (jnp.minimum(i, PROMPT_BLOCKS - 1), 0)),
            pl.BlockSpec((TOKEN_BLOCK, D_MODEL), lambda i: (jnp.maximum(i - PROMPT_BLOCKS, 0), 0)),
        ],
        out_shape=(jax.ShapeDtypeStruct((T_PROMPT, D_MODEL), F32), jax.ShapeDtypeStruct((T_SAMPLE, D_MODEL), F32)),
        compiler_params=cparams(dimension_semantics=("arbitrary",)),
        name="combine",
    )(out_kt, out_kt, out_kt, out_kt, gate_all, x1_all, fg)

    kv5 = (1, -1, WINDOW, N_KV_HEADS, HEAD_DIM)
    return (y_p.reshape(BATCH, SEQ, D_MODEL), y_s.reshape(DEC_BATCH, DEC_SEQ, D_MODEL),
            kp_new.reshape(kv5), vp_new.reshape(kv5), ks_new.reshape(kv5), vs_new.reshape(kv5),
            gv_s.reshape(1, DEC_BATCH, DEC_SEQ, SGU_HEADS, SGU_HEAD_DIM))
```

```python
import functools

import jax
import jax.numpy as jnp
from jax import lax
from jax.experimental import pallas as pl
from jax.experimental.pallas import tpu as pltpu

D_MODEL = 1024
BATCH = 2
SEQ = 8192
DEC_BATCH = 8
DEC_SEQ = 64
CHUNK = 64
WINDOW = 128
BAND = WINDOW + CHUNK
PAIR_BAND = BAND + CHUNK
HEAD_DIM = 64
N_Q_HEADS = 8
N_KV_HEADS = 2
Q_PER_KV = N_Q_HEADS // N_KV_HEADS
ATTN_WIDTH = N_Q_HEADS * HEAD_DIM
KV_WIDTH = N_KV_HEADS * HEAD_DIM
QKV_COLS = ATTN_WIDTH + 2 * KV_WIDTH
SGU_CHUNK = 128
SGU_HEADS = 4
SGU_HEAD_DIM = 128
SGU_WIDTH = SGU_HEADS * SGU_HEAD_DIM
IN_COLS = QKV_COLS + 2 * SGU_WIDTH
N_EXPERTS = 32
TOP_K = 4
D_FF = 1024
SWIGLU_LIMIT = 7.0
SWIGLU_ALPHA = 1.702
RMS_EPS = 1e-5
NEG_INF = -1e30

LANES = 128
SUBLANES = 8
assert D_MODEL == LANES * SUBLANES

T_PROMPT = BATCH * SEQ
T_SAMPLE = DEC_BATCH * DEC_SEQ
T_ALL = T_PROMPT + T_SAMPLE
TOKEN_BLOCK = 512
SEQ_BLOCKS = SEQ // TOKEN_BLOCK
N_TOKEN_BLOCKS = T_ALL // TOKEN_BLOCK
PROMPT_BLOCKS = T_PROMPT // TOKEN_BLOCK
EXPERT_ROWS = 256
N_SLOTS = T_ALL * TOP_K
MIN_PAGES = N_SLOTS // EXPERT_ROWS
N_PAGES = MIN_PAGES + N_EXPERTS
STAGES = 4
BLOCK_COPIES = TOP_K * TOKEN_BLOCK
DUMP_PAGE = N_PAGES
XS_PAGES = N_PAGES + STAGES * BLOCK_COPIES // EXPERT_ROWS
DUMP_ROW = N_SLOTS
OUT_ROWS = DUMP_ROW + STAGES * EXPERT_ROWS
TAG_MASK = 0xFFFF
VMEM_LIMIT_BYTES = 56 * 1024 * 1024
ISSUE_UNROLL = 8

assert T_SAMPLE == TOKEN_BLOCK and SEQ % TOKEN_BLOCK == 0 and N_SLOTS % EXPERT_ROWS == 0
assert T_ALL <= TAG_MASK and N_PAGES <= LANES * N_EXPERTS and T_ALL // EXPERT_ROWS + 1 <= LANES

F32 = jnp.float32
BF16 = jnp.bfloat16
I32 = jnp.int32


def _rms(x, g):
    return x * lax.rsqrt(jnp.mean(x * x, axis=-1, keepdims=True) + RMS_EPS) * g


def _gelu(x):
    return 0.5 * x * (1.0 + lax.erf(x * 0.7071067811865476))


def _store_tile_rows(ref, first_row, val):
    n = val.shape[0]
    for c in range(SUBLANES):
        ref[pl.ds(first_row * SUBLANES + c, n, stride=SUBLANES), :] = val[:, c * LANES:(c + 1) * LANES]


def _load_tile_rows(ref, n):
    return jnp.concatenate([ref[pl.ds(c, n, stride=SUBLANES), :] for c in range(SUBLANES)], axis=1)


def _tile_row(ref, row):
    return ref.at[pl.ds(pl.multiple_of(row * SUBLANES, SUBLANES), SUBLANES), :]


def _tile_rows(ref, first_row, n):
    return ref.at[pl.ds(pl.multiple_of(first_row * SUBLANES, SUBLANES), n * SUBLANES), :]


def _in_proj(x, n1g_ref, w_in_ref, sgng_ref):
    h = _rms(x, n1g_ref[...])
    cols = jnp.dot(h.astype(BF16), w_in_ref[...], preferred_element_type=F32)
    q = cols[:, :ATTN_WIDTH] * (HEAD_DIM ** -0.5)
    k = cols[:, ATTN_WIDTH:ATTN_WIDTH + KV_WIDTH]
    v = cols[:, ATTN_WIDTH + KV_WIDTH:QKV_COLS]
    gu = _gelu(cols[:, QKV_COLS:QKV_COLS + SGU_WIDTH])
    gv = _gelu(cols[:, QKV_COLS + SGU_WIDTH:])
    sg = sgng_ref[...]
    gvn = jnp.concatenate(
        [_rms(gv[:, i * SGU_HEAD_DIM:(i + 1) * SGU_HEAD_DIM], sg[:, i * SGU_HEAD_DIM:(i + 1) * SGU_HEAD_DIM])
         for i in range(SGU_HEADS)], axis=1)
    return q, k, v, gu, gvn


def _dup_heads(t):
    lane = lax.broadcasted_iota(I32, t.shape, 1)
    r = pltpu.roll(t, HEAD_DIM, axis=1)
    return jnp.where(lane < HEAD_DIM, t, r), jnp.where(lane < HEAD_DIM, r, t)


def _attend_chunks(q_chunks, kk, vv, alibi_ref, sink_ref, key_bias):
    lane = lax.broadcasted_iota(I32, (CHUNK, 2 * HEAD_DIM), 1)
    lo = lane < HEAD_DIM
    rows_per_chunk = Q_PER_KV * CHUNK
    pairs = [[] for _ in q_chunks]
    for g in range(N_KV_HEADS):
        rows = []
        for q64 in q_chunks:
            for p in range(2):
                qp = q64[:, (2 * g + p) * 128:(2 * g + p + 1) * 128]
                rows.append(jnp.where(lo, qp, 0.0))
                rows.append(jnp.where(lo, 0.0, qp))
        q4 = jnp.concatenate(rows, axis=0).astype(BF16)
        sc = lax.dot_general(q4, kk[g], (((1,), (1,)), ((), ())), preferred_element_type=F32)
        sc = sc + alibi_ref[g]
        if key_bias is not None:
            sc = sc + key_bias
        sink = sink_ref[g]
        m = jnp.maximum(jnp.max(sc, axis=1, keepdims=True), sink)
        p_un = jnp.exp(sc - m)
        den = jnp.sum(p_un, axis=1, keepdims=True) + jnp.exp(sink - m)
        o = jnp.dot(p_un.astype(BF16), vv[g], preferred_element_type=F32) / den
        for c in range(len(q_chunks)):
            for p in range(2):
                first = c * rows_per_chunk + 2 * p * CHUNK
                pairs[c].append(jnp.where(lo, o[first:first + CHUNK], o[first + CHUNK:first + 2 * CHUNK]))
    return [jnp.concatenate(chunk_pairs, axis=1) for chunk_pairs in pairs]


def _sgu_weights(sguw_ref):
    sub_i = lax.broadcasted_iota(I32, (SGU_CHUNK, SGU_CHUNK), 0) // CHUNK
    sub_j = lax.broadcasted_iota(I32, (SGU_CHUNK, SGU_CHUNK), 1) // CHUNK
    keep = sub_j <= sub_i
    return [jnp.where(keep, sguw_ref[i], 0.0).astype(BF16) for i in range(SGU_HEADS)]


def _sgu_chunk(ws, gu_c, gvn_c, sgubt_ref):
    outs = []
    for i in range(SGU_HEADS):
        sl = slice(i * SGU_HEAD_DIM, (i + 1) * SGU_HEAD_DIM)
        sp = jnp.dot(ws[i], gvn_c[:, sl].astype(BF16), preferred_element_type=F32) + sgubt_ref[:, i:i + 1]
        outs.append(gu_c[:, sl] * sp)
    return jnp.concatenate(outs, axis=1)


def _merge_route(x, attn, sgu, tok_base, slot, aog_ref, sog_ref, w_out_ref, n2g_ref, wr_ref, br_ref, earlier_ref,
                 cnt_scr, cur_scr, free_scr, table_scr, x1_ref, gate_ref, stage, dest_vmem):
    n = x.shape[0]
    a_n = _rms(attn, aog_ref[...]).astype(BF16)
    s_n = _rms(sgu, sog_ref[...]).astype(BF16)
    x1 = (x + jnp.dot(a_n, w_out_ref[:ATTN_WIDTH, :], preferred_element_type=F32)
          + jnp.dot(s_n, w_out_ref[ATTN_WIDTH:, :], preferred_element_type=F32))
    x1_ref[...] = x1
    h2 = _rms(x1, n2g_ref[...]).astype(BF16)
    logits = lax.dot_general(wr_ref[...], h2, (((1,), (1,)), ((), ())), preferred_element_type=F32) + br_ref[...]
    e_iota = lax.broadcasted_iota(I32, (N_EXPERTS, n), 0).astype(F32)
    cur = logits
    vals, idxs, sels = [], [], []
    for _ in range(TOP_K):
        m = jnp.max(cur, axis=0, keepdims=True)
        idx = jnp.min(jnp.where(cur == m, e_iota, float(N_EXPERTS)), axis=0, keepdims=True)
        sel = e_iota == idx
        vals.append(m)
        idxs.append(idx)
        sels.append(sel)
        cur = jnp.where(sel, -jnp.inf, cur)
    exps = [jnp.exp(v - vals[0]) for v in vals]
    den = exps[0] + exps[1] + exps[2] + exps[3]
    gate_ref[...] = jnp.concatenate([e / den for e in exps], axis=0)
    onehot = jnp.where(sels[0] | sels[1] | sels[2] | sels[3], 1.0, 0.0)
    count = cnt_scr[...]
    rank = jnp.dot(onehot.astype(BF16), earlier_ref[...], preferred_element_type=F32) + count
    added = jnp.sum(onehot, axis=1, keepdims=True)
    inv_page = 1.0 / EXPERT_ROWS
    had_pages = jnp.floor((count + (EXPERT_ROWS - 1)) * inv_page)
    need_pages = jnp.floor((count + added + (EXPERT_ROWS - 1)) * inv_page)
    new_pages = need_pages - had_pages
    below = (lax.broadcasted_iota(I32, (N_EXPERTS, N_EXPERTS), 1)
             < lax.broadcasted_iota(I32, (N_EXPERTS, N_EXPERTS), 0))
    first_new = free_scr[...] + jnp.dot(
        jnp.where(below, 1.0, 0.0).astype(BF16), jnp.broadcast_to(new_pages, (N_EXPERTS, LANES)).astype(BF16),
        preferred_element_type=F32)[:, 0:1]
    page_idx = jnp.floor(rank * inv_page)
    page = jnp.where(page_idx < had_pages, cur_scr[...], first_new + (page_idx - had_pages))
    place = page * EXPERT_ROWS + (rank - page_idx * EXPERT_ROWS)
    dest_vmem[slot] = jnp.concatenate(
        [jnp.sum(jnp.where(s, place, 0.0), axis=0, keepdims=True) for s in sels], axis=0).astype(I32)
    lane = lax.broadcasted_iota(I32, (N_EXPERTS, LANES), 1).astype(F32)
    table_scr[...] = jnp.where((lane >= had_pages) & (lane < need_pages), first_new + (lane - had_pages),
                               table_scr[...])
    cur_scr[...] = jnp.where(new_pages > 0, first_new + new_pages - 1.0, cur_scr[...])
    cnt_scr[...] = count + added
    free_scr[...] = free_scr[...] + jnp.sum(new_pages, axis=0, keepdims=True)

    words = lax.bitcast_convert_type(h2.astype(F32), I32)
    tok = (lax.broadcasted_iota(I32, (1, n), 1) + tok_base).astype(F32)
    tag_t = jnp.concatenate([tok] + idxs + [jnp.zeros((LANES - 1 - TOP_K, n), F32)], axis=0)
    tagged = words[:, :LANES] | tag_t.T.astype(I32)
    _store_tile_rows(stage, slot * TOKEN_BLOCK, jnp.concatenate([tagged, words[:, LANES:]], axis=1))


def _mixer_kernel(xp_ref, xs_in_ref, ck_ref, cv_ref, n1g_ref, w_in_ref, sgng_ref, sink_ref, alibi_ref, sink2_ref,
                  alibi2_ref, sguw_ref, sgubt_ref, aog_ref, sog_ref, w_out_ref, n2g_ref, wr_ref, br_ref, earlier_ref,
                  x1_ref, gate_ref, cnt_ref, table_ref, kp_ref, vp_ref, ks_ref, vs_ref, gv_ref, xs_ref,
                  kk0, kk1, vv0, vv1, cnt_scr, cur_scr, free_scr, table_scr, stage, zero_page,
                  dest_vmem, dest_smem, meta_vmem, meta_smem, sem_rows, sem_dest, sem_fill):
    n = pl.program_id(0)
    s = n % SEQ_BLOCKS
    slot = n % STAGES
    leaving = (n + STAGES - 2) % STAGES
    weights = (aog_ref, sog_ref, w_out_ref, n2g_ref, wr_ref, br_ref, earlier_ref)
    state = (cnt_scr, cur_scr, free_scr, table_scr)

    def dump_row(st):
        return DUMP_PAGE * EXPERT_ROWS + st * BLOCK_COPIES

    def dest_copy(st):
        return pltpu.make_async_copy(dest_vmem.at[st], dest_smem.at[st], sem_dest.at[st])

    def rows_wait(st):
        for _ in range(TOP_K):
            pltpu.make_async_copy(_tile_rows(stage, 0, TOKEN_BLOCK), _tile_rows(xs_ref, 0, TOKEN_BLOCK),
                                  sem_rows.at[st]).wait()

    def row_copy(st, kk, t):
        return pltpu.make_async_copy(_tile_row(stage, st * TOKEN_BLOCK + t), _tile_row(xs_ref, dest_smem[st, kk, t]),
                                     sem_rows.at[st])

    def send_rows(st):
        dest_copy(st).wait()
        for t in range(TOKEN_BLOCK):
            for kk in range(TOP_K):
                row_copy(st, kk, t).start(priority=kk % 2)

    def send_rows_looped(st):
        dest_copy(st).wait()

        def issue(t, carry):
            for kk in range(TOP_K):
                row_copy(st, kk, t).start(priority=kk % 2)
            return carry

        lax.fori_loop(0, TOKEN_BLOCK, issue, 0, unroll=ISSUE_UNROLL)

    @pl.when(n == 0)
    def _():
        for ref in state:
            ref[...] = jnp.zeros_like(ref)
        stage[...] = jnp.zeros_like(stage)
        zero_page[...] = jnp.zeros_like(zero_page)
        for buf in (kk0, kk1, vv0, vv1):
            buf[0:WINDOW, :] = jnp.zeros((WINDOW, 2 * HEAD_DIM), BF16)
        for st in range(STAGES):
            if st < STAGES - 2:
                for kk in range(TOP_K):
                    pltpu.make_async_copy(_tile_rows(stage, st * TOKEN_BLOCK, TOKEN_BLOCK),
                                          _tile_rows(xs_ref, dump_row(st) + kk * TOKEN_BLOCK, TOKEN_BLOCK),
                                          sem_rows.at[st]).start()
            else:
                dest_vmem[st] = (dump_row(st) + lax.broadcasted_iota(I32, (TOP_K, TOKEN_BLOCK), 0) * TOKEN_BLOCK
                                 + lax.broadcasted_iota(I32, (TOP_K, TOKEN_BLOCK), 1))
                dest_copy(st).start()

    @pl.when(n < PROMPT_BLOCKS)
    def _():
        rows_wait(slot)
        send_rows(leaving)
        x = xp_ref[...]
        q, k, v, gu, gvn = _in_proj(x, n1g_ref, w_in_ref, sgng_ref)
        kp_ref[...] = k[TOKEN_BLOCK - WINDOW:, :]
        vp_ref[...] = v[TOKEN_BLOCK - WINDOW:, :]

        k0, k1 = _dup_heads(k)
        v0, v1 = _dup_heads(v)
        for buf, val in ((kk0, k0), (kk1, k1), (vv0, v0), (vv1, v1)):
            buf[WINDOW:, :] = val.astype(BF16)

        attn_chunks = []
        for j in range(0, TOKEN_BLOCK // CHUNK, 2):
            band = slice(j * CHUNK, j * CHUNK + PAIR_BAND)
            key_bias = None
            if j * CHUNK < WINDOW:
                key_pos = lax.broadcasted_iota(I32, (1, PAIR_BAND), 1) + (s * TOKEN_BLOCK + j * CHUNK - WINDOW)
                key_bias = jnp.where(key_pos < 0, NEG_INF, 0.0)
            attn_chunks += _attend_chunks([q[j * CHUNK:(j + 1) * CHUNK], q[(j + 1) * CHUNK:(j + 2) * CHUNK]],
                                          (kk0[band, :], kk1[band, :]), (vv0[band, :], vv1[band, :]),
                                          alibi2_ref, sink2_ref, key_bias)
        attn = jnp.concatenate(attn_chunks, axis=0)

        for buf in (kk0, kk1, vv0, vv1):
            tail = buf[TOKEN_BLOCK:TOKEN_BLOCK + WINDOW, :]
            buf[0:WINDOW, :] = jnp.where(s == SEQ_BLOCKS - 1, jnp.zeros_like(tail), tail)

        ws = _sgu_weights(sguw_ref)
        sgu = jnp.concatenate(
            [_sgu_chunk(ws, gu[c * SGU_CHUNK:(c + 1) * SGU_CHUNK], gvn[c * SGU_CHUNK:(c + 1) * SGU_CHUNK],
                        sgubt_ref) for c in range(TOKEN_BLOCK // SGU_CHUNK)], axis=0)
        _merge_route(x, attn, sgu, n * TOKEN_BLOCK, slot, *weights, *state, x1_ref, gate_ref, stage, dest_vmem)
        dest_copy(slot).start()

    @pl.when(n == PROMPT_BLOCKS)
    def _():
        rows_wait(slot)
        send_rows(leaving)
        x = xs_in_ref[...]
        q, k, v, gu, gvn = _in_proj(x, n1g_ref, w_in_ref, sgng_ref)
        gv_ref[...] = gvn
        ws = _sgu_weights(sguw_ref)
        attn_chunks, sgu_chunks = [], []
        pad = jnp.zeros((SGU_CHUNK - DEC_SEQ, SGU_WIDTH), F32)
        for b in range(DEC_BATCH):
            rows = slice(b * DEC_SEQ, (b + 1) * DEC_SEQ)
            k_all = jnp.concatenate([ck_ref[b], k[rows]], axis=0)
            v_all = jnp.concatenate([cv_ref[b], v[rows]], axis=0)
            ks_ref[b] = k_all[DEC_SEQ:]
            vs_ref[b] = v_all[DEC_SEQ:]
            k0, k1 = _dup_heads(k_all)
            v0, v1 = _dup_heads(v_all)
            attn_chunks += _attend_chunks([q[rows]], (k0.astype(BF16), k1.astype(BF16)),
                                          (v0.astype(BF16), v1.astype(BF16)), alibi_ref, sink_ref, None)
            gated = _sgu_chunk(ws, jnp.concatenate([gu[rows], pad], axis=0),
                               jnp.concatenate([gvn[rows], pad], axis=0), sgubt_ref)
            sgu_chunks.append(gated[:DEC_SEQ])
        attn = jnp.concatenate(attn_chunks, axis=0)
        sgu = jnp.concatenate(sgu_chunks, axis=0)
        _merge_route(x, attn, sgu, n * TOKEN_BLOCK, slot, *weights, *state, x1_ref, gate_ref, stage, dest_vmem)
        dest_copy(slot).start()

        send_rows_looped((N_TOKEN_BLOCKS - 2) % STAGES)
        send_rows_looped((N_TOKEN_BLOCKS - 1) % STAGES)
        for st in range(STAGES):
            rows_wait(st)
        count = cnt_scr[...]
        left = count - jnp.floor(count * (1.0 / EXPERT_ROWS)) * EXPERT_ROWS
        lane = lax.broadcasted_iota(I32, (N_EXPERTS, LANES), 1)
        meta = jnp.where(lane == 0, cur_scr[...], jnp.where(lane == 1, left, jnp.where(lane == 2, free_scr[...], 0.0)))
        meta_t = jnp.concatenate([meta, jnp.zeros((LANES - N_EXPERTS, LANES), F32)], axis=0).T
        meta_vmem[...] = meta_t[0:SUBLANES].astype(I32)
        to_smem = pltpu.make_async_copy(meta_vmem, meta_smem, sem_dest.at[0])
        to_smem.start()
        to_smem.wait()

        def zero_fill(first_row, rows):
            return pltpu.make_async_copy(_tile_rows(zero_page, 0, rows), _tile_rows(xs_ref, first_row, rows), sem_fill)

        bits = [1 << b for b in reversed(range(EXPERT_ROWS.bit_length() - 1))]
        for wait in (False, True):
            for e in range(N_EXPERTS):
                used = meta_smem[1, e]
                left = jnp.where(used > 0, EXPERT_ROWS - used, 0)
                row = meta_smem[0, e] * EXPERT_ROWS + used
                for bit in bits:
                    take = (left & bit) != 0

                    @pl.when(take)
                    def _():
                        if wait:
                            zero_fill(0, bit).wait()
                        else:
                            zero_fill(row, bit).start()

                    row = row + jnp.where(take, bit, 0)
        in_use = meta_smem[2, 0]
        for j in range(N_PAGES - MIN_PAGES):
            @pl.when(in_use + j < N_PAGES)
            def _():
                pltpu.make_async_copy(zero_page, _tile_rows(xs_ref, (in_use + j) * EXPERT_ROWS, EXPERT_ROWS),
                                      sem_fill).start()
        for j in range(N_PAGES - MIN_PAGES):
            @pl.when(in_use + j < N_PAGES)
            def _():
                pltpu.make_async_copy(zero_page, _tile_rows(xs_ref, 0, EXPERT_ROWS), sem_fill).wait()

    cnt_ref[...] = cnt_scr[...].astype(I32)
    table_ref[...] = table_scr[...].astype(I32)


def _expert_kernel(bstart_ref, nrow_ref, page_ref, xs_ref, wgu_ref, bgu_ref, wd_ref, bd_ref, out_ref,
                   wgu_bf, wd_bf, xbuf, stage, ids_vmem, ids_smem, sem_x, sem_rows, sem_ids):
    expert = pl.program_id(0)
    first_block = bstart_ref[expert]
    end_block = bstart_ref[expert + 1]
    nv = bstart_ref[N_EXPERTS]
    block_words = EXPERT_ROWS * SUBLANES

    def x_copy(g, s):
        return pltpu.make_async_copy(_tile_rows(xs_ref, page_ref[g] * EXPERT_ROWS, EXPERT_ROWS), xbuf.at[s],
                                     sem_x.at[s])

    def dump_row(s):
        return DUMP_ROW + s * EXPERT_ROWS

    def ids_copy(s):
        return pltpu.make_async_copy(ids_vmem.at[s], ids_smem.at[s], sem_ids.at[s])

    def rows_wait(s):
        whole = pl.ds(0, block_words)
        pltpu.make_async_copy(stage.at[whole, :], out_ref.at[whole, :], sem_rows.at[s]).wait()

    def issue_rows(s):
        ids_copy(s).wait()
        for r in range(EXPERT_ROWS):
            pltpu.make_async_copy(_tile_row(stage, s * EXPERT_ROWS + r), _tile_row(out_ref, ids_smem[s, 0, r]),
                                  sem_rows.at[s]).start(priority=r % 2)

    def compute(g, xs_slot, s):
        words = _load_tile_rows(xbuf.at[xs_slot], EXPERT_ROWS)
        tag = words[:, :LANES] & TAG_MASK
        x = lax.bitcast_convert_type(words & ~TAG_MASK, F32).astype(BF16)
        gu = jnp.dot(x, wgu_bf[...], preferred_element_type=F32) + bgu_ref[...]
        x_glu = jnp.minimum(gu[:, :D_FF], SWIGLU_LIMIT)
        x_lin = jnp.clip(gu[:, D_FF:], -SWIGLU_LIMIT, SWIGLU_LIMIT)
        act = x_glu * (1.0 / (1.0 + jnp.exp(-SWIGLU_ALPHA * x_glu))) * (x_lin + 1.0)
        out = jnp.dot(act.astype(BF16), wd_bf[...], preferred_element_type=F32) + bd_ref[...]
        _store_tile_rows(stage, s * EXPERT_ROWS, out)
        k_slot = sum(j * jnp.where(tag[:, 1 + j:2 + j] == expert, 1, 0) for j in range(1, TOP_K))
        row = lax.broadcasted_iota(I32, (EXPERT_ROWS, 1), 0)
        dst = jnp.where(row < nrow_ref[g], k_slot * T_ALL + tag[:, 0:1], dump_row(s) + row)
        dst_t = jnp.broadcast_to(dst.astype(F32), (EXPERT_ROWS, LANES)).T
        ids_vmem[s] = dst_t[0:SUBLANES].astype(I32)
        ids_copy(s).start()

    @pl.when(expert == 0)
    def _():
        stage[...] = jnp.zeros_like(stage)
        for s in range(STAGES):
            if s < STAGES - 2:
                pltpu.make_async_copy(stage.at[pl.ds(s * block_words, block_words), :],
                                      out_ref.at[pl.ds(dump_row(s) * SUBLANES, block_words), :],
                                      sem_rows.at[s]).start()
            else:
                ids_vmem[s] = dump_row(s) + lax.broadcasted_iota(I32, (SUBLANES, EXPERT_ROWS), 1)
                ids_copy(s).start()
        x_copy(0, 0).start()

    @pl.when(end_block > first_block)
    def _():
        rows = 128

        def cast_rows(c, carry):
            r = pl.multiple_of(c * rows, rows)
            wgu_bf[pl.ds(r, rows), :] = wgu_ref[pl.ds(r, rows), :].astype(BF16)
            wd_bf[pl.ds(r, rows), :] = wd_ref[pl.ds(r, rows), :].astype(BF16)
            return carry

        lax.fori_loop(0, D_MODEL // rows, cast_rows, 0)

    def block_step(g, carry):
        xs_slot = g % 2
        slot = g % STAGES
        x_copy(g, xs_slot).wait()
        x_copy(jnp.minimum(g + 1, nv - 1), 1 - xs_slot).start()
        rows_wait(slot)
        issue_rows((g + STAGES - 2) % STAGES)
        compute(g, xs_slot, slot)
        return carry

    lax.fori_loop(first_block, end_block, block_step, 0)

    @pl.when(expert == N_EXPERTS - 1)
    def _():
        x_copy(nv - 1, nv % 2).wait()
        issue_rows((nv + STAGES - 2) % STAGES)
        issue_rows((nv + STAGES - 1) % STAGES)
        for s in range(STAGES):
            rows_wait(s)


def _combine_kernel(o0_ref, o1_ref, o2_ref, o3_ref, gate_ref, x1_ref, fg_ref, yp_ref, ys_ref):
    i = pl.program_id(0)
    gates = jnp.concatenate([gate_ref[...], jnp.zeros((LANES - TOP_K, TOKEN_BLOCK), F32)], axis=0).T
    y = x1_ref[...]
    for kk, o_ref in enumerate((o0_ref, o1_ref, o2_ref, o3_ref)):
        y = y + gates[:, kk:kk + 1] * _load_tile_rows(o_ref, TOKEN_BLOCK)
    out = _rms(y, fg_ref[...])

    @pl.when(i < PROMPT_BLOCKS)
    def _():
        yp_ref[...] = out

    @pl.when(i >= PROMPT_BLOCKS)
    def _():
        ys_ref[...] = out


def _full(shape):
    return pl.BlockSpec(shape, lambda *_: (0,) * len(shape))


def kernel(x_prompt, x_sample, cache_k, cache_v, norm1_g, w_in, sgu_norm_g, attn_sinks, sgu_w, sgu_b, attn_out_g,
           sgu_out_g, w_out, norm2_g, w_router, b_router, w_gate_up, b_gate_up, w_down, b_down, final_g):
    n1g = norm1_g[0].reshape(1, D_MODEL)
    w_in_b = w_in[0].astype(BF16)
    sgng = sgu_norm_g[0].reshape(1, SGU_WIDTH)
    sink_cols = jnp.repeat(attn_sinks[0].reshape(N_KV_HEADS, Q_PER_KV, 1), CHUNK, axis=1).reshape(
        N_KV_HEADS, Q_PER_KV * CHUNK, 1)
    slopes = jnp.exp2(-8.0 * jnp.arange(1, N_Q_HEADS + 1, dtype=F32) / N_Q_HEADS).reshape(N_KV_HEADS, Q_PER_KV)
    dist = jnp.abs(jnp.arange(CHUNK)[:, None] + WINDOW - jnp.arange(BAND)[None, :]).astype(F32)
    alibi = (-slopes[:, :, None, None] * dist).reshape(N_KV_HEADS, Q_PER_KV * CHUNK, BAND)
    sguw = sgu_w[0]
    sgubt = sgu_b[0].T
    aog = attn_out_g[0].reshape(1, ATTN_WIDTH)
    sog = sgu_out_g[0].reshape(1, SGU_WIDTH)
    w_out_b = w_out[0].astype(BF16)
    n2g = norm2_g[0].reshape(1, D_MODEL)
    wr_t = w_router[0].T.astype(BF16)
    br_c = b_router[0].reshape(N_EXPERTS, 1)
    fg = final_g.reshape(1, D_MODEL)

    weight_specs = [
        _full((1, D_MODEL)), _full((D_MODEL, IN_COLS)), _full((1, SGU_WIDTH)),
        _full((N_KV_HEADS, Q_PER_KV * CHUNK, 1)), _full((N_KV_HEADS, Q_PER_KV * CHUNK, BAND)),
        _full((N_KV_HEADS, 2 * Q_PER_KV * CHUNK, 1)), _full((N_KV_HEADS, 2 * Q_PER_KV * CHUNK, PAIR_BAND)),
        _full((SGU_HEADS, SGU_CHUNK, SGU_CHUNK)), _full((SGU_CHUNK, SGU_HEADS)),
        _full((1, ATTN_WIDTH)), _full((1, SGU_WIDTH)), _full((D_MODEL, D_MODEL)), _full((1, D_MODEL)),
        _full((N_EXPERTS, D_MODEL)), _full((N_EXPERTS, 1)), _full((TOKEN_BLOCK, TOKEN_BLOCK)),
    ]
    own_key = jnp.arange(PAIR_BAND)[None, None, :] - CHUNK * jnp.arange(2)[:, None, None]
    in_band = (own_key >= 0) & (own_key < BAND)
    dist2 = jnp.abs(jnp.arange(CHUNK)[None, :, None] + WINDOW - own_key).astype(F32)
    alibi2 = jnp.where(in_band[None, None], -slopes[:, :, None, None, None] * dist2[None, None], NEG_INF)
    alibi2 = alibi2.transpose(0, 2, 1, 3, 4).reshape(N_KV_HEADS, 2 * Q_PER_KV * CHUNK, PAIR_BAND)
    sink2 = jnp.concatenate([sink_cols, sink_cols], axis=1)
    token = jnp.arange(TOKEN_BLOCK)
    earlier = (token[:, None] < token[None, :]).astype(BF16)
    weights = (n1g, w_in_b, sgng, sink_cols, alibi, sink2, alibi2, sguw, sgubt, aog, sog, w_out_b, n2g, wr_t, br_c,
               earlier)
    cparams = functools.partial(pltpu.CompilerParams, vmem_limit_bytes=VMEM_LIMIT_BYTES)
    any_spec = pl.BlockSpec(memory_space=pl.ANY)

    def tok_map(n):
        return (n, 0)

    def tok_map_t(n):
        return (0, n)

    def stream_map(n):
        return (jnp.minimum(n // SEQ_BLOCKS, BATCH - 1), 0, 0)

    kv_cache = (DEC_BATCH, WINDOW, KV_WIDTH)
    bf_band = pltpu.VMEM((WINDOW + TOKEN_BLOCK, 2 * HEAD_DIM), BF16)
    state_col = pltpu.VMEM((N_EXPERTS, 1), F32)
    (x1_all, gate_all, cnt_all, table, kp_new, vp_new, ks_new, vs_new, gv_s, xs) = pl.pallas_call(
        _mixer_kernel,
        grid=(N_TOKEN_BLOCKS,),
        in_specs=[pl.BlockSpec((TOKEN_BLOCK, D_MODEL), lambda n: (jnp.minimum(n, PROMPT_BLOCKS - 1), 0)),
                  _full((T_SAMPLE, D_MODEL)), _full(kv_cache), _full(kv_cache)] + weight_specs,
        out_specs=[
            pl.BlockSpec((TOKEN_BLOCK, D_MODEL), tok_map),
            pl.BlockSpec((TOP_K, TOKEN_BLOCK), tok_map_t),
            _full((N_EXPERTS, 1)), _full((N_EXPERTS, LANES)),
            pl.BlockSpec((None, WINDOW, KV_WIDTH), stream_map), pl.BlockSpec((None, WINDOW, KV_WIDTH), stream_map),
            _full(kv_cache), _full(kv_cache), _full((T_SAMPLE, SGU_WIDTH)),
            any_spec,
        ],
        out_shape=(
            jax.ShapeDtypeStruct((T_ALL, D_MODEL), F32),
            jax.ShapeDtypeStruct((TOP_K, T_ALL), F32),
            jax.ShapeDtypeStruct((N_EXPERTS, 1), I32),
            jax.ShapeDtypeStruct((N_EXPERTS, LANES), I32),
            jax.ShapeDtypeStruct((BATCH, WINDOW, KV_WIDTH), F32),
            jax.ShapeDtypeStruct((BATCH, WINDOW, KV_WIDTH), F32),
            jax.ShapeDtypeStruct(kv_cache, F32),
            jax.ShapeDtypeStruct(kv_cache, F32),
            jax.ShapeDtypeStruct((T_SAMPLE, SGU_WIDTH), F32),
            jax.ShapeDtypeStruct((XS_PAGES * EXPERT_ROWS * SUBLANES, LANES), I32),
        ),
        scratch_shapes=[
            bf_band, bf_band, bf_band, bf_band,
            state_col, state_col, pltpu.VMEM((1, 1), F32), pltpu.VMEM((N_EXPERTS, LANES), F32),
            pltpu.VMEM((STAGES * TOKEN_BLOCK * SUBLANES, LANES), I32),
            pltpu.VMEM((EXPERT_ROWS * SUBLANES, LANES), I32),
            pltpu.VMEM((STAGES, TOP_K, TOKEN_BLOCK), I32), pltpu.SMEM((STAGES, TOP_K, TOKEN_BLOCK), I32),
            pltpu.VMEM((SUBLANES, LANES), I32), pltpu.SMEM((SUBLANES, LANES), I32),
            pltpu.SemaphoreType.DMA((STAGES,)), pltpu.SemaphoreType.DMA((STAGES,)), pltpu.SemaphoreType.DMA,
        ],
        compiler_params=cparams(dimension_semantics=("arbitrary",)),
        name="mixer",
    )(x_prompt.reshape(T_PROMPT, D_MODEL), x_sample.reshape(T_SAMPLE, D_MODEL),
      cache_k[0].reshape(kv_cache), cache_v[0].reshape(kv_cache), *weights)

    counts = cnt_all[:, 0]
    blocks_per = (counts + EXPERT_ROWS - 1) // EXPERT_ROWS
    block_end = jnp.cumsum(blocks_per)
    block_start = block_end - blocks_per
    n_valid = block_end[-1:].astype(I32)
    step = jnp.minimum(jnp.arange(N_PAGES, dtype=I32), n_valid - 1)
    expert_of_step = step[:, None] >= block_end[None, :]
    blk_e = jnp.minimum(jnp.sum(expert_of_step, axis=1), N_EXPERTS - 1).astype(I32)
    own = (jnp.arange(N_EXPERTS, dtype=I32)[None, :] == blk_e[:, None]).astype(I32)
    blk_in_expert = step - jnp.sum(own * block_start[None, :], axis=1)
    rows_left = jnp.sum(own * counts[None, :], axis=1) - blk_in_expert * EXPERT_ROWS
    blk_rows = jnp.clip(rows_left, 0, EXPERT_ROWS).astype(I32)
    blk_page = table.reshape(-1)[blk_e * LANES + blk_in_expert].astype(I32)
    block_bounds = jnp.concatenate([jnp.zeros((1,), I32), block_end.astype(I32)])

    def expert_map(e, bounds, nr, pg):
        return (e, 0, 0)

    out_kt = pl.pallas_call(
        _expert_kernel,
        grid_spec=pltpu.PrefetchScalarGridSpec(
            num_scalar_prefetch=3,
            grid=(N_EXPERTS,),
            in_specs=[
                any_spec,
                pl.BlockSpec((None, D_MODEL, 2 * D_FF), expert_map),
                pl.BlockSpec((None, 1, 2 * D_FF), expert_map),
                pl.BlockSpec((None, D_FF, D_MODEL), expert_map),
                pl.BlockSpec((None, 1, D_MODEL), expert_map),
            ],
            out_specs=any_spec,
            scratch_shapes=[
                pltpu.VMEM((D_MODEL, 2 * D_FF), BF16), pltpu.VMEM((D_FF, D_MODEL), BF16),
                pltpu.VMEM((2, EXPERT_ROWS * SUBLANES, LANES), I32),
                pltpu.VMEM((STAGES * EXPERT_ROWS * SUBLANES, LANES), F32),
                pltpu.VMEM((STAGES, SUBLANES, EXPERT_ROWS), I32),
                pltpu.SMEM((STAGES, SUBLANES, EXPERT_ROWS), I32),
                pltpu.SemaphoreType.DMA((2,)), pltpu.SemaphoreType.DMA((STAGES,)),
                pltpu.SemaphoreType.DMA((STAGES,)),
            ],
        ),
        out_shape=jax.ShapeDtypeStruct((OUT_ROWS * SUBLANES, LANES), F32),
        compiler_params=cparams(dimension_semantics=("arbitrary",)),
        name="experts",
    )(block_bounds, blk_rows, blk_page, xs, w_gate_up[0], b_gate_up[0].reshape(N_EXPERTS, 1, 2 * D_FF), w_down[0],
      b_down[0].reshape(N_EXPERTS, 1, D_MODEL))

    def slot_spec(kk):
        return pl.BlockSpec((TOKEN_BLOCK * SUBLANES, LANES), lambda i: (kk * N_TOKEN_BLOCKS + i, 0))

    y_p, y_s = pl.pallas_call(
        _combine_kernel,
        grid=(N_TOKEN_BLOCKS,),
        in_specs=[slot_spec(kk) for kk in range(TOP_K)] + [
            pl.BlockSpec((TOP_K, TOKEN_BLOCK), lambda i: (0, i)),
            pl.BlockSpec((TOKEN_BLOCK, D_MODEL), lambda i: (i, 0)),
            _full((1, D_MODEL)),
        ],
        out_specs=[
            pl.BlockSpec((TOKEN_BLOCK, D_MODEL), lambda i: (jnp.minimum(i, PROMPT_BLOCKS - 1), 0)),
            pl.BlockSpec((TOKEN_BLOCK, D_MODEL), lambda i: (jnp.maximum(i - PROMPT_BLOCKS, 0), 0)),
        ],
        out_shape=(jax.ShapeDtypeStruct((T_PROMPT, D_MODEL), F32), jax.ShapeDtypeStruct((T_SAMPLE, D_MODEL), F32)),
        compiler_params=cparams(dimension_semantics=("arbitrary",)),
        name="combine",
    )(out_kt, out_kt, out_kt, out_kt, gate_all, x1_all, fg)

    kv5 = (1, -1, WINDOW, N_KV_HEADS, HEAD_DIM)
    return (y_p.reshape(BATCH, SEQ, D_MODEL), y_s.reshape(DEC_BATCH, DEC_SEQ, D_MODEL),
            kp_new.reshape(kv5), vp_new.reshape(kv5), ks_new.reshape(kv5), vs_new.reshape(kv5),
            gv_s.reshape(1, DEC_BATCH, DEC_SEQ, SGU_HEADS, SGU_HEAD_DIM))
```

```python
import functools

import jax
import jax.numpy as jnp
from jax import lax
from jax.experimental import pallas as pl
from jax.experimental.pallas import tpu as pltpu

D_MODEL = 1024
BATCH = 2
SEQ = 8192
DEC_BATCH = 8
DEC_SEQ = 64
CHUNK = 64
WINDOW = 128
BAND = WINDOW + CHUNK
PAIR_BAND = BAND + CHUNK
HEAD_DIM = 64
N_Q_HEADS = 8
N_KV_HEADS = 2
Q_PER_KV = N_Q_HEADS // N_KV_HEADS
ATTN_WIDTH = N_Q_HEADS * HEAD_DIM
KV_WIDTH = N_KV_HEADS * HEAD_DIM
QKV_COLS = ATTN_WIDTH + 2 * KV_WIDTH
SGU_CHUNK = 128
SGU_HEADS = 4
SGU_HEAD_DIM = 128
SGU_WIDTH = SGU_HEADS * SGU_HEAD_DIM
IN_COLS = QKV_COLS + 2 * SGU_WIDTH
N_EXPERTS = 32
TOP_K = 4
D_FF = 1024
SWIGLU_LIMIT = 7.0
SWIGLU_ALPHA = 1.702
RMS_EPS = 1e-5
NEG_INF = -1e30

LANES = 128
SUBLANES = 8
assert D_MODEL == LANES * SUBLANES

T_PROMPT = BATCH * SEQ
T_SAMPLE = DEC_BATCH * DEC_SEQ
T_ALL = T_PROMPT + T_SAMPLE
TOKEN_BLOCK = 512
SEQ_BLOCKS = SEQ // TOKEN_BLOCK
N_TOKEN_BLOCKS = T_ALL // TOKEN_BLOCK
PROMPT_BLOCKS = T_PROMPT // TOKEN_BLOCK
EXPERT_ROWS = 256
N_SLOTS = T_ALL * TOP_K
MIN_PAGES = N_SLOTS // EXPERT_ROWS
N_PAGES = MIN_PAGES + N_EXPERTS
STAGES = 4
BLOCK_COPIES = TOP_K * TOKEN_BLOCK
DUMP_PAGE = N_PAGES
XS_PAGES = N_PAGES + STAGES * BLOCK_COPIES // EXPERT_ROWS
DUMP_ROW = N_SLOTS
OUT_ROWS = DUMP_ROW + STAGES * EXPERT_ROWS
TAG_MASK = 0xFFFF
VMEM_LIMIT_BYTES = 56 * 1024 * 1024
ISSUE_UNROLL = 8

assert T_SAMPLE == TOKEN_BLOCK and SEQ % TOKEN_BLOCK == 0 and N_SLOTS % EXPERT_ROWS == 0
assert T_ALL <= TAG_MASK and N_PAGES <= LANES * N_EXPERTS and T_ALL // EXPERT_ROWS + 1 <= LANES

F32 = jnp.float32
BF16 = jnp.bfloat16
I32 = jnp.int32


def _rms(x, g):
    return x * lax.rsqrt(jnp.mean(x * x, axis=-1, keepdims=True) + RMS_EPS) * g


def _gelu(x):
    return 0.5 * x * (1.0 + lax.erf(x * 0.7071067811865476))


def _store_tile_rows(ref, first_row, val):
    n = val.shape[0]
    for c in range(SUBLANES):
        ref[pl.ds(first_row * SUBLANES + c, n, stride=SUBLANES), :] = val[:, c * LANES:(c + 1) * LANES]


def _load_tile_rows(ref, n):
    return jnp.concatenate([ref[pl.ds(c, n, stride=SUBLANES), :] for c in range(SUBLANES)], axis=1)


def _tile_row(ref, row):
    return ref.at[pl.ds(pl.multiple_of(row * SUBLANES, SUBLANES), SUBLANES), :]


def _tile_rows(ref, first_row, n):
    return ref.at[pl.ds(pl.multiple_of(first_row * SUBLANES, SUBLANES), n * SUBLANES), :]


def _in_proj(x, n1g_ref, w_in_ref, sgng_ref):
    h = _rms(x, n1g_ref[...])
    cols = jnp.dot(h.astype(BF16), w_in_ref[...], preferred_element_type=F32)
    q = cols[:, :ATTN_WIDTH] * (HEAD_DIM ** -0.5)
    k = cols[:, ATTN_WIDTH:ATTN_WIDTH + KV_WIDTH]
    v = cols[:, ATTN_WIDTH + KV_WIDTH:QKV_COLS]
    gu = _gelu(cols[:, QKV_COLS:QKV_COLS + SGU_WIDTH])
    gv = _gelu(cols[:, QKV_COLS + SGU_WIDTH:])
    sg = sgng_ref[...]
    gvn = jnp.concatenate(
        [_rms(gv[:, i * SGU_HEAD_DIM:(i + 1) * SGU_HEAD_DIM], sg[:, i * SGU_HEAD_DIM:(i + 1) * SGU_HEAD_DIM])
         for i in range(SGU_HEADS)], axis=1)
    return q, k, v, gu, gvn


def _dup_heads(t):
    lane = lax.broadcasted_iota(I32, t.shape, 1)
    r = pltpu.roll(t, HEAD_DIM, axis=1)
    return jnp.where(lane < HEAD_DIM, t, r), jnp.where(lane < HEAD_DIM, r, t)


def _attend_chunks(q_chunks, kk, vv, alibi_ref, sink_ref, key_bias):
    lane = lax.broadcasted_iota(I32, (CHUNK, 2 * HEAD_DIM), 1)
    lo = lane < HEAD_DIM
    rows_per_chunk = Q_PER_KV * CHUNK
    pairs = [[] for _ in q_chunks]
    for g in range(N_KV_HEADS):
        rows = []
        for q64 in q_chunks:
            for p in range(2):
                qp = q64[:, (2 * g + p) * 128:(2 * g + p + 1) * 128]
                rows.append(jnp.where(lo, qp, 0.0))
                rows.append(jnp.where(lo, 0.0, qp))
        q4 = jnp.concatenate(rows, axis=0).astype(BF16)
        sc = lax.dot_general(q4, kk[g], (((1,), (1,)), ((), ())), preferred_element_type=F32)
        sc = sc + alibi_ref[g]
        if key_bias is not None:
            sc = sc + key_bias
        sink = sink_ref[g]
        m = jnp.maximum(jnp.max(sc, axis=1, keepdims=True), sink)
        p_un = jnp.exp(sc - m)
        den = jnp.sum(p_un, axis=1, keepdims=True) + jnp.exp(sink - m)
        o = jnp.dot(p_un.astype(BF16), vv[g], preferred_element_type=F32) / den
        for c in range(len(q_chunks)):
            for p in range(2):
                first = c * rows_per_chunk + 2 * p * CHUNK
                pairs[c].append(jnp.where(lo, o[first:first + CHUNK], o[first + CHUNK:first + 2 * CHUNK]))
    return [jnp.concatenate(chunk_pairs, axis=1) for chunk_pairs in pairs]


def _sgu_weights(sguw_ref):
    sub_i = lax.broadcasted_iota(I32, (SGU_CHUNK, SGU_CHUNK), 0) // CHUNK
    sub_j = lax.broadcasted_iota(I32, (SGU_CHUNK, SGU_CHUNK), 1) // CHUNK
    keep = sub_j <= sub_i
    return [jnp.where(keep, sguw_ref[i], 0.0).astype(BF16) for i in range(SGU_HEADS)]


def _sgu_chunk(ws, gu_c, gvn_c, sgubt_ref):
    outs = []
    for i in range(SGU_HEADS):
        sl = slice(i * SGU_HEAD_DIM, (i + 1) * SGU_HEAD_DIM)
        sp = jnp.dot(ws[i], gvn_c[:, sl].astype(BF16), preferred_element_type=F32) + sgubt_ref[:, i:i + 1]
        outs.append(gu_c[:, sl] * sp)
    return jnp.concatenate(outs, axis=1)


def _merge_route(x, attn, sgu, tok_base, slot, aog_ref, sog_ref, w_out_ref, n2g_ref, wr_ref, br_ref, earlier_ref,
                 cnt_scr, cur_scr, free_scr, table_scr, x1_ref, gate_ref, stage, dest_vmem):
    n = x.shape[0]
    a_n = _rms(attn, aog_ref[...]).astype(BF16)
    s_n = _rms(sgu, sog_ref[...]).astype(BF16)
    x1 = (x + jnp.dot(a_n, w_out_ref[:ATTN_WIDTH, :], preferred_element_type=F32)
          + jnp.dot(s_n, w_out_ref[ATTN_WIDTH:, :], preferred_element_type=F32))
    x1_ref[...] = x1
    h2 = _rms(x1, n2g_ref[...]).astype(BF16)
    logits = lax.dot_general(wr_ref[...], h2, (((1,), (1,)), ((), ())), preferred_element_type=F32) + br_ref[...]
    e_iota = lax.broadcasted_iota(I32, (N_EXPERTS, n), 0).astype(F32)
    cur = logits
    vals, idxs, sels = [], [], []
    for _ in range(TOP_K):
        m = jnp.max(cur, axis=0, keepdims=True)
        idx = jnp.min(jnp.where(cur == m, e_iota, float(N_EXPERTS)), axis=0, keepdims=True)
        sel = e_iota == idx
        vals.append(m)
        idxs.append(idx)
        sels.append(sel)
        cur = jnp.where(sel, -jnp.inf, cur)
    exps = [jnp.exp(v - vals[0]) for v in vals]
    den = exps[0] + exps[1] + exps[2] + exps[3]
    gate_ref[...] = jnp.concatenate([e / den for e in exps], axis=0)
    onehot = jnp.where(sels[0] | sels[1] | sels[2] | sels[3], 1.0, 0.0)
    count = cnt_scr[...]
    rank = jnp.dot(onehot.astype(BF16), earlier_ref[...], preferred_element_type=F32) + count
    added = jnp.sum(onehot, axis=1, keepdims=True)
    inv_page = 1.0 / EXPERT_ROWS
    had_pages = jnp.floor((count + (EXPERT_ROWS - 1)) * inv_page)
    need_pages = jnp.floor((count + added + (EXPERT_ROWS - 1)) * inv_page)
    new_pages = need_pages - had_pages
    below = (lax.broadcasted_iota(I32, (N_EXPERTS, N_EXPERTS), 1)
             < lax.broadcasted_iota(I32, (N_EXPERTS, N_EXPERTS), 0))
    first_new = free_scr[...] + jnp.dot(
        jnp.where(below, 1.0, 0.0).astype(BF16), jnp.broadcast_to(new_pages, (N_EXPERTS, LANES)).astype(BF16),
        preferred_element_type=F32)[:, 0:1]
    page_idx = jnp.floor(rank * inv_page)
    page = jnp.where(page_idx < had_pages, cur_scr[...], first_new + (page_idx - had_pages))
    place = page * EXPERT_ROWS + (rank - page_idx * EXPERT_ROWS)
    dest_vmem[slot] = jnp.concatenate(
        [jnp.sum(jnp.where(s, place, 0.0), axis=0, keepdims=True) for s in sels], axis=0).astype(I32)
    lane = lax.broadcasted_iota(I32, (N_EXPERTS, LANES), 1).astype(F32)
    table_scr[...] = jnp.where((lane >= had_pages) & (lane < need_pages), first_new + (lane - had_pages),
                               table_scr[...])
    cur_scr[...] = jnp.where(new_pages > 0, first_new + new_pages - 1.0, cur_scr[...])
    cnt_scr[...] = count + added
    free_scr[...] = free_scr[...] + jnp.sum(new_pages, axis=0, keepdims=True)

    words = lax.bitcast_convert_type(h2.astype(F32), I32)
    tok = (lax.broadcasted_iota(I32, (1, n), 1) + tok_base).astype(F32)
    tag_t = jnp.concatenate([tok] + idxs + [jnp.zeros((LANES - 1 - TOP_K, n), F32)], axis=0)
    tagged = words[:, :LANES] | tag_t.T.astype(I32)
    _store_tile_rows(stage, slot * TOKEN_BLOCK, jnp.concatenate([tagged, words[:, LANES:]], axis=1))


def _mixer_kernel(xp_ref, xs_in_ref, ck_ref, cv_ref, n1g_ref, w_in_ref, sgng_ref, sink_ref, alibi_ref, sink2_ref,
                  alibi2_ref, sguw_ref, sgubt_ref, aog_ref, sog_ref, w_out_ref, n2g_ref, wr_ref, br_ref, earlier_ref,
                  x1_ref, gate_ref, cnt_ref, table_ref, kp_ref, vp_ref, ks_ref, vs_ref, gv_ref, xs_ref,
                  kk0, kk1, vv0, vv1, cnt_scr, cur_scr, free_scr, table_scr, stage, zero_page,
                  dest_vmem, dest_smem, meta_vmem, meta_smem, sem_rows, sem_dest, sem_fill):
    n = pl.program_id(0)
    s = n % SEQ_BLOCKS
    slot = n % STAGES
    leaving = (n + STAGES - 2) % STAGES
    weights = (aog_ref, sog_ref, w_out_ref, n2g_ref, wr_ref, br_ref, earlier_ref)
    state = (cnt_scr, cur_scr, free_scr, table_scr)

    def dump_row(st):
        return DUMP_PAGE * EXPERT_ROWS + st * BLOCK_COPIES

    def dest_copy(st):
        return pltpu.make_async_copy(dest_vmem.at[st], dest_smem.at[st], sem_dest.at[st])

    def rows_wait(st):
        for _ in range(TOP_K):
            pltpu.make_async_copy(_tile_rows(stage, 0, TOKEN_BLOCK), _tile_rows(xs_ref, 0, TOKEN_BLOCK),
                                  sem_rows.at[st]).wait()

    def row_copy(st, kk, t):
        return pltpu.make_async_copy(_tile_row(stage, st * TOKEN_BLOCK + t), _tile_row(xs_ref, dest_smem[st, kk, t]),
                                     sem_rows.at[st])

    def send_rows(st):
        dest_copy(st).wait()
        for t in range(TOKEN_BLOCK):
            for kk in range(TOP_K):
                row_copy(st, kk, t).start(priority=kk % 2)

    def send_rows_looped(st):
        dest_copy(st).wait()

        def issue(t, carry):
            for kk in range(TOP_K):
                row_copy(st, kk, t).start(priority=kk % 2)
            return carry

        lax.fori_loop(0, TOKEN_BLOCK, issue, 0, unroll=ISSUE_UNROLL)

    @pl.when(n == 0)
    def _():
        for ref in state:
            ref[...] = jnp.zeros_like(ref)
        zero_page[...] = jnp.zeros_like(zero_page)
        for buf in (kk0, kk1, vv0, vv1):
            buf[0:WINDOW, :] = jnp.zeros((WINDOW, 2 * HEAD_DIM), BF16)
        sent_rows = 2 * TOKEN_BLOCK * SUBLANES
        stage[pl.ds((STAGES - 2) * TOKEN_BLOCK * SUBLANES, sent_rows), :] = jnp.zeros((sent_rows, LANES), I32)
        for st in range(STAGES):
            if st < STAGES - 2:
                for first in range(0, BLOCK_COPIES, EXPERT_ROWS):
                    pltpu.make_async_copy(zero_page, _tile_rows(xs_ref, dump_row(st) + first, EXPERT_ROWS),
                                          sem_rows.at[st]).start()
            else:
                dest_vmem[st] = (dump_row(st) + lax.broadcasted_iota(I32, (TOP_K, TOKEN_BLOCK), 0) * TOKEN_BLOCK
                                 + lax.broadcasted_iota(I32, (TOP_K, TOKEN_BLOCK), 1))
                dest_copy(st).start()

    @pl.when(n < PROMPT_BLOCKS)
    def _():
        rows_wait(slot)
        send_rows(leaving)
        x = xp_ref[...]
        q, k, v, gu, gvn = _in_proj(x, n1g_ref, w_in_ref, sgng_ref)
        kp_ref[...] = k[TOKEN_BLOCK - WINDOW:, :]
        vp_ref[...] = v[TOKEN_BLOCK - WINDOW:, :]

        k0, k1 = _dup_heads(k)
        v0, v1 = _dup_heads(v)
        for buf, val in ((kk0, k0), (kk1, k1), (vv0, v0), (vv1, v1)):
            buf[WINDOW:, :] = val.astype(BF16)

        attn_chunks = []
        for j in range(0, TOKEN_BLOCK // CHUNK, 2):
            band = slice(j * CHUNK, j * CHUNK + PAIR_BAND)
            key_bias = None
            if j * CHUNK < WINDOW:
                key_pos = lax.broadcasted_iota(I32, (1, PAIR_BAND), 1) + (s * TOKEN_BLOCK + j * CHUNK - WINDOW)
                key_bias = jnp.where(key_pos < 0, NEG_INF, 0.0)
            attn_chunks += _attend_chunks([q[j * CHUNK:(j + 1) * CHUNK], q[(j + 1) * CHUNK:(j + 2) * CHUNK]],
                                          (kk0[band, :], kk1[band, :]), (vv0[band, :], vv1[band, :]),
                                          alibi2_ref, sink2_ref, key_bias)
        attn = jnp.concatenate(attn_chunks, axis=0)

        for buf in (kk0, kk1, vv0, vv1):
            tail = buf[TOKEN_BLOCK:TOKEN_BLOCK + WINDOW, :]
            buf[0:WINDOW, :] = jnp.where(s == SEQ_BLOCKS - 1, jnp.zeros_like(tail), tail)

        ws = _sgu_weights(sguw_ref)
        sgu = jnp.concatenate(
            [_sgu_chunk(ws, gu[c * SGU_CHUNK:(c + 1) * SGU_CHUNK], gvn[c * SGU_CHUNK:(c + 1) * SGU_CHUNK],
                        sgubt_ref) for c in range(TOKEN_BLOCK // SGU_CHUNK)], axis=0)
        _merge_route(x, attn, sgu, n * TOKEN_BLOCK, slot, *weights, *state, x1_ref, gate_ref, stage, dest_vmem)
        dest_copy(slot).start()

    @pl.when(n == PROMPT_BLOCKS)
    def _():
        rows_wait(slot)
        send_rows(leaving)
        send_rows((N_TOKEN_BLOCKS - 2) % STAGES)
        x = xs_in_ref[...]
        q, k, v, gu, gvn = _in_proj(x, n1g_ref, w_in_ref, sgng_ref)
        gv_ref[...] = gvn
        ws = _sgu_weights(sguw_ref)
        attn_chunks, sgu_chunks = [], []
        pad = jnp.zeros((SGU_CHUNK - DEC_SEQ, SGU_WIDTH), F32)
        for b in range(DEC_BATCH):
            rows = slice(b * DEC_SEQ, (b + 1) * DEC_SEQ)
            k_all = jnp.concatenate([ck_ref[b], k[rows]], axis=0)
            v_all = jnp.concatenate([cv_ref[b], v[rows]], axis=0)
            ks_ref[b] = k_all[DEC_SEQ:]
            vs_ref[b] = v_all[DEC_SEQ:]
            k0, k1 = _dup_heads(k_all)
            v0, v1 = _dup_heads(v_all)
            attn_chunks += _attend_chunks([q[rows]], (k0.astype(BF16), k1.astype(BF16)),
                                          (v0.astype(BF16), v1.astype(BF16)), alibi_ref, sink_ref, None)
            gated = _sgu_chunk(ws, jnp.concatenate([gu[rows], pad], axis=0),
                               jnp.concatenate([gvn[rows], pad], axis=0), sgubt_ref)
            sgu_chunks.append(gated[:DEC_SEQ])
        attn = jnp.concatenate(attn_chunks, axis=0)
        sgu = jnp.concatenate(sgu_chunks, axis=0)
        _merge_route(x, attn, sgu, n * TOKEN_BLOCK, slot, *weights, *state, x1_ref, gate_ref, stage, dest_vmem)
        dest_copy(slot).start()

        send_rows_looped((N_TOKEN_BLOCKS - 1) % STAGES)
        for st in range(STAGES):
            rows_wait(st)
        count = cnt_scr[...]
        left = count - jnp.floor(count * (1.0 / EXPERT_ROWS)) * EXPERT_ROWS
        lane = lax.broadcasted_iota(I32, (N_EXPERTS, LANES), 1)
        meta = jnp.where(lane == 0, cur_scr[...], jnp.where(lane == 1, left, jnp.where(lane == 2, free_scr[...], 0.0)))
        meta_t = jnp.concatenate([meta, jnp.zeros((LANES - N_EXPERTS, LANES), F32)], axis=0).T
        meta_vmem[...] = meta_t[0:SUBLANES].astype(I32)
        to_smem = pltpu.make_async_copy(meta_vmem, meta_smem, sem_dest.at[0])
        to_smem.start()
        to_smem.wait()

        def zero_fill(first_row, rows):
            return pltpu.make_async_copy(_tile_rows(zero_page, 0, rows), _tile_rows(xs_ref, first_row, rows), sem_fill)

        bits = [1 << b for b in reversed(range(EXPERT_ROWS.bit_length() - 1))]
        for wait in (False, True):
            for e in range(N_EXPERTS):
                used = meta_smem[1, e]
                left = jnp.where(used > 0, EXPERT_ROWS - used, 0)
                row = meta_smem[0, e] * EXPERT_ROWS + used
                for bit in bits:
                    take = (left & bit) != 0

                    @pl.when(take)
                    def _():
                        if wait:
                            zero_fill(0, bit).wait()
                        else:
                            zero_fill(row, bit).start()

                    row = row + jnp.where(take, bit, 0)
        in_use = meta_smem[2, 0]
        for j in range(N_PAGES - MIN_PAGES):
            @pl.when(in_use + j < N_PAGES)
            def _():
                pltpu.make_async_copy(zero_page, _tile_rows(xs_ref, (in_use + j) * EXPERT_ROWS, EXPERT_ROWS),
                                      sem_fill).start()
        for j in range(N_PAGES - MIN_PAGES):
            @pl.when(in_use + j < N_PAGES)
            def _():
                pltpu.make_async_copy(zero_page, _tile_rows(xs_ref, 0, EXPERT_ROWS), sem_fill).wait()

    cnt_ref[...] = cnt_scr[...].astype(I32)
    table_ref[...] = table_scr[...].astype(I32)


def _expert_kernel(bstart_ref, nrow_ref, page_ref, xs_ref, wgu_ref, bgu_ref, wd_ref, bd_ref, out_ref,
                   wgu_bf, wd_bf, xbuf, stage, ids_vmem, ids_smem, sem_x, sem_rows, sem_ids):
    expert = pl.program_id(0)
    first_block = bstart_ref[expert]
    end_block = bstart_ref[expert + 1]
    nv = bstart_ref[N_EXPERTS]
    block_words = EXPERT_ROWS * SUBLANES

    def x_copy(g, s):
        return pltpu.make_async_copy(_tile_rows(xs_ref, page_ref[g] * EXPERT_ROWS, EXPERT_ROWS), xbuf.at[s],
                                     sem_x.at[s])

    def dump_row(s):
        return DUMP_ROW + s * EXPERT_ROWS

    def ids_copy(s):
        return pltpu.make_async_copy(ids_vmem.at[s], ids_smem.at[s], sem_ids.at[s])

    def rows_wait(s):
        whole = pl.ds(0, block_words)
        pltpu.make_async_copy(stage.at[whole, :], out_ref.at[whole, :], sem_rows.at[s]).wait()

    def issue_rows(s):
        ids_copy(s).wait()
        for r in range(EXPERT_ROWS):
            pltpu.make_async_copy(_tile_row(stage, s * EXPERT_ROWS + r), _tile_row(out_ref, ids_smem[s, 0, r]),
                                  sem_rows.at[s]).start(priority=r % 2)

    def compute(g, xs_slot, s):
        words = _load_tile_rows(xbuf.at[xs_slot], EXPERT_ROWS)
        tag = words[:, :LANES] & TAG_MASK
        x = lax.bitcast_convert_type(words & ~TAG_MASK, F32).astype(BF16)
        gu = jnp.dot(x, wgu_bf[...], preferred_element_type=F32) + bgu_ref[...]
        x_glu = jnp.minimum(gu[:, :D_FF], SWIGLU_LIMIT)
        x_lin = jnp.clip(gu[:, D_FF:], -SWIGLU_LIMIT, SWIGLU_LIMIT)
        act = x_glu * (1.0 / (1.0 + jnp.exp(-SWIGLU_ALPHA * x_glu))) * (x_lin + 1.0)
        out = jnp.dot(act.astype(BF16), wd_bf[...], preferred_element_type=F32) + bd_ref[...]
        _store_tile_rows(stage, s * EXPERT_ROWS, out)
        k_slot = sum(j * jnp.where(tag[:, 1 + j:2 + j] == expert, 1, 0) for j in range(1, TOP_K))
        row = lax.broadcasted_iota(I32, (EXPERT_ROWS, 1), 0)
        dst = jnp.where(row < nrow_ref[g], k_slot * T_ALL + tag[:, 0:1], dump_row(s) + row)
        dst_t = jnp.broadcast_to(dst.astype(F32), (EXPERT_ROWS, LANES)).T
        ids_vmem[s] = dst_t[0:SUBLANES].astype(I32)
        ids_copy(s).start()

    @pl.when(expert == 0)
    def _():
        stage[...] = jnp.zeros_like(stage)
        for s in range(STAGES):
            if s < STAGES - 2:
                pltpu.make_async_copy(stage.at[pl.ds(s * block_words, block_words), :],
                                      out_ref.at[pl.ds(dump_row(s) * SUBLANES, block_words), :],
                                      sem_rows.at[s]).start()
            else:
                ids_vmem[s] = dump_row(s) + lax.broadcasted_iota(I32, (SUBLANES, EXPERT_ROWS), 1)
                ids_copy(s).start()
        x_copy(0, 0).start()

    @pl.when(end_block > first_block)
    def _():
        rows = 128

        def cast_rows(c, carry):
            r = pl.multiple_of(c * rows, rows)
            wgu_bf[pl.ds(r, rows), :] = wgu_ref[pl.ds(r, rows), :].astype(BF16)
            wd_bf[pl.ds(r, rows), :] = wd_ref[pl.ds(r, rows), :].astype(BF16)
            return carry

        lax.fori_loop(0, D_MODEL // rows, cast_rows, 0)

    def block_step(g, carry):
        xs_slot = g % 2
        slot = g % STAGES
        x_copy(g, xs_slot).wait()
        x_copy(jnp.minimum(g + 1, nv - 1), 1 - xs_slot).start()
        rows_wait(slot)
        issue_rows((g + STAGES - 2) % STAGES)
        compute(g, xs_slot, slot)
        return carry

    lax.fori_loop(first_block, end_block, block_step, 0)

    @pl.when(expert == N_EXPERTS - 1)
    def _():
        x_copy(nv - 1, nv % 2).wait()
        issue_rows((nv + STAGES - 2) % STAGES)
        issue_rows((nv + STAGES - 1) % STAGES)
        for s in range(STAGES):
            rows_wait(s)


def _combine_kernel(o0_ref, o1_ref, o2_ref, o3_ref, gate_ref, x1_ref, fg_ref, yp_ref, ys_ref):
    i = pl.program_id(0)
    gates = jnp.concatenate([gate_ref[...], jnp.zeros((LANES - TOP_K, TOKEN_BLOCK), F32)], axis=0).T
    y = x1_ref[...]
    for kk, o_ref in enumerate((o0_ref, o1_ref, o2_ref, o3_ref)):
        y = y + gates[:, kk:kk + 1] * _load_tile_rows(o_ref, TOKEN_BLOCK)
    out = _rms(y, fg_ref[...])

    @pl.when(i < PROMPT_BLOCKS)
    def _():
        yp_ref[...] = out

    @pl.when(i >= PROMPT_BLOCKS)
    def _():
        ys_ref[...] = out


def _full(shape):
    return pl.BlockSpec(shape, lambda *_: (0,) * len(shape))


def kernel(x_prompt, x_sample, cache_k, cache_v, norm1_g, w_in, sgu_norm_g, attn_sinks, sgu_w, sgu_b, attn_out_g,
           sgu_out_g, w_out, norm2_g, w_router, b_router, w_gate_up, b_gate_up, w_down, b_down, final_g):
    n1g = norm1_g[0].reshape(1, D_MODEL)
    w_in_b = w_in[0].astype(BF16)
    sgng = sgu_norm_g[0].reshape(1, SGU_WIDTH)
    sink_cols = jnp.repeat(attn_sinks[0].reshape(N_KV_HEADS, Q_PER_KV, 1), CHUNK, axis=1).reshape(
        N_KV_HEADS, Q_PER_KV * CHUNK, 1)
    slopes = jnp.exp2(-8.0 * jnp.arange(1, N_Q_HEADS + 1, dtype=F32) / N_Q_HEADS).reshape(N_KV_HEADS, Q_PER_KV)
    dist = jnp.abs(jnp.arange(CHUNK)[:, None] + WINDOW - jnp.arange(BAND)[None, :]).astype(F32)
    alibi = (-slopes[:, :, None, None] * dist).reshape(N_KV_HEADS, Q_PER_KV * CHUNK, BAND)
    sguw = sgu_w[0]
    sgubt = sgu_b[0].T
    aog = attn_out_g[0].reshape(1, ATTN_WIDTH)
    sog = sgu_out_g[0].reshape(1, SGU_WIDTH)
    w_out_b = w_out[0].astype(BF16)
    n2g = norm2_g[0].reshape(1, D_MODEL)
    wr_t = w_router[0].T.astype(BF16)
    br_c = b_router[0].reshape(N_EXPERTS, 1)
    fg = final_g.reshape(1, D_MODEL)

    weight_specs = [
        _full((1, D_MODEL)), _full((D_MODEL, IN_COLS)), _full((1, SGU_WIDTH)),
        _full((N_KV_HEADS, Q_PER_KV * CHUNK, 1)), _full((N_KV_HEADS, Q_PER_KV * CHUNK, BAND)),
        _full((N_KV_HEADS, 2 * Q_PER_KV * CHUNK, 1)), _full((N_KV_HEADS, 2 * Q_PER_KV * CHUNK, PAIR_BAND)),
        _full((SGU_HEADS, SGU_CHUNK, SGU_CHUNK)), _full((SGU_CHUNK, SGU_HEADS)),
        _full((1, ATTN_WIDTH)), _full((1, SGU_WIDTH)), _full((D_MODEL, D_MODEL)), _full((1, D_MODEL)),
        _full((N_EXPERTS, D_MODEL)), _full((N_EXPERTS, 1)), _full((TOKEN_BLOCK, TOKEN_BLOCK)),
    ]
    own_key = jnp.arange(PAIR_BAND)[None, None, :] - CHUNK * jnp.arange(2)[:, None, None]
    in_band = (own_key >= 0) & (own_key < BAND)
    dist2 = jnp.abs(jnp.arange(CHUNK)[None, :, None] + WINDOW - own_key).astype(F32)
    alibi2 = jnp.where(in_band[None, None], -slopes[:, :, None, None, None] * dist2[None, None], NEG_INF)
    alibi2 = alibi2.transpose(0, 2, 1, 3, 4).reshape(N_KV_HEADS, 2 * Q_PER_KV * CHUNK, PAIR_BAND)
    sink2 = jnp.concatenate([sink_cols, sink_cols], axis=1)
    token = jnp.arange(TOKEN_BLOCK)
    earlier = (token[:, None] < token[None, :]).astype(BF16)
    weights = (n1g, w_in_b, sgng, sink_cols, alibi, sink2, alibi2, sguw, sgubt, aog, sog, w_out_b, n2g, wr_t, br_c,
               earlier)
    cparams = functools.partial(pltpu.CompilerParams, vmem_limit_bytes=VMEM_LIMIT_BYTES)
    any_spec = pl.BlockSpec(memory_space=pl.ANY)

    def tok_map(n):
        return (n, 0)

    def tok_map_t(n):
        return (0, n)

    def stream_map(n):
        return (jnp.minimum(n // SEQ_BLOCKS, BATCH - 1), 0, 0)

    kv_cache = (DEC_BATCH, WINDOW, KV_WIDTH)
    bf_band = pltpu.VMEM((WINDOW + TOKEN_BLOCK, 2 * HEAD_DIM), BF16)
    state_col = pltpu.VMEM((N_EXPERTS, 1), F32)
    (x1_all, gate_all, cnt_all, table, kp_new, vp_new, ks_new, vs_new, gv_s, xs) = pl.pallas_call(
        _mixer_kernel,
        grid=(N_TOKEN_BLOCKS,),
        in_specs=[pl.BlockSpec((TOKEN_BLOCK, D_MODEL), lambda n: (jnp.minimum(n, PROMPT_BLOCKS - 1), 0)),
                  _full((T_SAMPLE, D_MODEL)), _full(kv_cache), _full(kv_cache)] + weight_specs,
        out_specs=[
            pl.BlockSpec((TOKEN_BLOCK, D_MODEL), tok_map),
            pl.BlockSpec((TOP_K, TOKEN_BLOCK), tok_map_t),
            _full((N_EXPERTS, 1)), _full((N_EXPERTS, LANES)),
            pl.BlockSpec((None, WINDOW, KV_WIDTH), stream_map), pl.BlockSpec((None, WINDOW, KV_WIDTH), stream_map),
            _full(kv_cache), _full(kv_cache), _full((T_SAMPLE, SGU_WIDTH)),
            any_spec,
        ],
        out_shape=(
            jax.ShapeDtypeStruct((T_ALL, D_MODEL), F32),
            jax.ShapeDtypeStruct((TOP_K, T_ALL), F32),
            jax.ShapeDtypeStruct((N_EXPERTS, 1), I32),
            jax.ShapeDtypeStruct((N_EXPERTS, LANES), I32),
            jax.ShapeDtypeStruct((BATCH, WINDOW, KV_WIDTH), F32),
            jax.ShapeDtypeStruct((BATCH, WINDOW, KV_WIDTH), F32),
            jax.ShapeDtypeStruct(kv_cache, F32),
            jax.ShapeDtypeStruct(kv_cache, F32),
            jax.ShapeDtypeStruct((T_SAMPLE, SGU_WIDTH), F32),
            jax.ShapeDtypeStruct((XS_PAGES * EXPERT_ROWS * SUBLANES, LANES), I32),
        ),
        scratch_shapes=[
            bf_band, bf_band, bf_band, bf_band,
            state_col, state_col, pltpu.VMEM((1, 1), F32), pltpu.VMEM((N_EXPERTS, LANES), F32),
            pltpu.VMEM((STAGES * TOKEN_BLOCK * SUBLANES, LANES), I32),
            pltpu.VMEM((EXPERT_ROWS * SUBLANES, LANES), I32),
            pltpu.VMEM((STAGES, TOP_K, TOKEN_BLOCK), I32), pltpu.SMEM((STAGES, TOP_K, TOKEN_BLOCK), I32),
            pltpu.VMEM((SUBLANES, LANES), I32), pltpu.SMEM((SUBLANES, LANES), I32),
            pltpu.SemaphoreType.DMA((STAGES,)), pltpu.SemaphoreType.DMA((STAGES,)), pltpu.SemaphoreType.DMA,
        ],
        compiler_params=cparams(dimension_semantics=("arbitrary",)),
        name="mixer",
    )(x_prompt.reshape(T_PROMPT, D_MODEL), x_sample.reshape(T_SAMPLE, D_MODEL),
      cache_k[0].reshape(kv_cache), cache_v[0].reshape(kv_cache), *weights)

    counts = cnt_all[:, 0]
    blocks_per = (counts + EXPERT_ROWS - 1) // EXPERT_ROWS
    block_end = jnp.cumsum(blocks_per)
    block_start = block_end - blocks_per
    n_valid = block_end[-1:].astype(I32)
    step = jnp.minimum(jnp.arange(N_PAGES, dtype=I32), n_valid - 1)
    expert_of_step = step[:, None] >= block_end[None, :]
    blk_e = jnp.minimum(jnp.sum(expert_of_step, axis=1), N_EXPERTS - 1).astype(I32)
    own = (jnp.arange(N_EXPERTS, dtype=I32)[None, :] == blk_e[:, None]).astype(I32)
    blk_in_expert = step - jnp.sum(own * block_start[None, :], axis=1)
    rows_left = jnp.sum(own * counts[None, :], axis=1) - blk_in_expert * EXPERT_ROWS
    blk_rows = jnp.clip(rows_left, 0, EXPERT_ROWS).astype(I32)
    blk_page = table.reshape(-1)[blk_e * LANES + blk_in_expert].astype(I32)
    block_bounds = jnp.concatenate([jnp.zeros((1,), I32), block_end.astype(I32)])

    def expert_map(e, bounds, nr, pg):
        return (e, 0, 0)

    out_kt = pl.pallas_call(
        _expert_kernel,
        grid_spec=pltpu.PrefetchScalarGridSpec(
            num_scalar_prefetch=3,
            grid=(N_EXPERTS,),
            in_specs=[
                any_spec,
                pl.BlockSpec((None, D_MODEL, 2 * D_FF), expert_map),
                pl.BlockSpec((None, 1, 2 * D_FF), expert_map),
                pl.BlockSpec((None, D_FF, D_MODEL), expert_map),
                pl.BlockSpec((None, 1, D_MODEL), expert_map),
            ],
            out_specs=any_spec,
            scratch_shapes=[
                pltpu.VMEM((D_MODEL, 2 * D_FF), BF16), pltpu.VMEM((D_FF, D_MODEL), BF16),
                pltpu.VMEM((2, EXPERT_ROWS * SUBLANES, LANES), I32),
                pltpu.VMEM((STAGES * EXPERT_ROWS * SUBLANES, LANES), F32),
                pltpu.VMEM((STAGES, SUBLANES, EXPERT_ROWS), I32),
                pltpu.SMEM((STAGES, SUBLANES, EXPERT_ROWS), I32),
                pltpu.SemaphoreType.DMA((2,)), pltpu.SemaphoreType.DMA((STAGES,)),
                pltpu.SemaphoreType.DMA((STAGES,)),
            ],
        ),
        out_shape=jax.ShapeDtypeStruct((OUT_ROWS * SUBLANES, LANES), F32),
        compiler_params=cparams(dimension_semantics=("arbitrary",)),
        name="experts",
    )(block_bounds, blk_rows, blk_page, xs, w_gate_up[0], b_gate_up[0].reshape(N_EXPERTS, 1, 2 * D_FF), w_down[0],
      b_down[0].reshape(N_EXPERTS, 1, D_MODEL))

    def slot_spec(kk):
        return pl.BlockSpec((TOKEN_BLOCK * SUBLANES, LANES), lambda i: (kk * N_TOKEN_BLOCKS + i, 0))

    y_p, y_s = pl.pallas_call(
        _combine_kernel,
        grid=(N_TOKEN_BLOCKS,),
        in_specs=[slot_spec(kk) for kk in range(TOP_K)] + [
            pl.BlockSpec((TOP_K, TOKEN_BLOCK), lambda i: (0, i)),
            pl.BlockSpec((TOKEN_BLOCK, D_MODEL), lambda i: (i, 0)),
            _full((1, D_MODEL)),
        ],
        out_specs=[
            pl.BlockSpec((TOKEN_BLOCK, D_MODEL), lambda i: (jnp.minimum(i, PROMPT_BLOCKS - 1), 0)),
            pl.BlockSpec((TOKEN_BLOCK, D_MODEL), lambda i: (jnp.maximum(i - PROMPT_BLOCKS, 0), 0)),
        ],
        out_shape=(jax.ShapeDtypeStruct((T_PROMPT, D_MODEL), F32), jax.ShapeDtypeStruct((T_SAMPLE, D_MODEL), F32)),
        compiler_params=cparams(dimension_semantics=("arbitrary",)),
        name="combine",
    )(out_kt, out_kt, out_kt, out_kt, gate_all, x1_all, fg)

    kv5 = (1, -1, WINDOW, N_KV_HEADS, HEAD_DIM)
    return (y_p.reshape(BATCH, SEQ, D_MODEL), y_s.reshape(DEC_BATCH, DEC_SEQ, D_MODEL),
            kp_new.reshape(kv5), vp_new.reshape(kv5), ks_new.reshape(kv5), vs_new.reshape(kv5),
            gv_s.reshape(1, DEC_BATCH, DEC_SEQ, SGU_HEADS, SGU_HEAD_DIM))
```

```python
import functools

import jax
import jax.numpy as jnp
from jax import lax
from jax.experimental import pallas as pl
from jax.experimental.pallas import tpu as pltpu

D_MODEL = 1024
BATCH = 2
SEQ = 8192
DEC_BATCH = 8
DEC_SEQ = 64
CHUNK = 64
WINDOW = 128
BAND = WINDOW + CHUNK
PAIR_BAND = BAND + CHUNK
HEAD_DIM = 64
N_Q_HEADS = 8
N_KV_HEADS = 2
Q_PER_KV = N_Q_HEADS // N_KV_HEADS
ATTN_WIDTH = N_Q_HEADS * HEAD_DIM
KV_WIDTH = N_KV_HEADS * HEAD_DIM
QKV_COLS = ATTN_WIDTH + 2 * KV_WIDTH
SGU_CHUNK = 128
SGU_HEADS = 4
SGU_HEAD_DIM = 128
SGU_WIDTH = SGU_HEADS * SGU_HEAD_DIM
IN_COLS = QKV_COLS + 2 * SGU_WIDTH
N_EXPERTS = 32
TOP_K = 4
D_FF = 1024
SWIGLU_LIMIT = 7.0
SWIGLU_ALPHA = 1.702
RMS_EPS = 1e-5
NEG_INF = -1e30

LANES = 128
SUBLANES = 8
assert D_MODEL == LANES * SUBLANES

T_PROMPT = BATCH * SEQ
T_SAMPLE = DEC_BATCH * DEC_SEQ
T_ALL = T_PROMPT + T_SAMPLE
TOKEN_BLOCK = 512
SEQ_BLOCKS = SEQ // TOKEN_BLOCK
N_TOKEN_BLOCKS = T_ALL // TOKEN_BLOCK
PROMPT_BLOCKS = T_PROMPT // TOKEN_BLOCK
EXPERT_ROWS = 256
N_SLOTS = T_ALL * TOP_K
MIN_PAGES = N_SLOTS // EXPERT_ROWS
N_PAGES = MIN_PAGES + N_EXPERTS
STAGES = 4
BLOCK_COPIES = TOP_K * TOKEN_BLOCK
DUMP_PAGE = N_PAGES
XS_PAGES = N_PAGES + STAGES * BLOCK_COPIES // EXPERT_ROWS
DUMP_ROW = N_SLOTS
OUT_ROWS = DUMP_ROW + STAGES * EXPERT_ROWS
TAG_MASK = 0xFFFF
VMEM_LIMIT_BYTES = 56 * 1024 * 1024
ISSUE_UNROLL = 8

assert T_SAMPLE == TOKEN_BLOCK and SEQ % TOKEN_BLOCK == 0 and N_SLOTS % EXPERT_ROWS == 0
assert T_ALL <= TAG_MASK and N_PAGES <= LANES * N_EXPERTS and T_ALL // EXPERT_ROWS + 1 <= LANES

F32 = jnp.float32
BF16 = jnp.bfloat16
I32 = jnp.int32


def _rms(x, g):
    return x * lax.rsqrt(jnp.mean(x * x, axis=-1, keepdims=True) + RMS_EPS) * g


def _gelu(x):
    return 0.5 * x * (1.0 + lax.erf(x * 0.7071067811865476))


def _store_tile_rows(ref, first_row, val):
    n = val.shape[0]
    for c in range(SUBLANES):
        ref[pl.ds(first_row * SUBLANES + c, n, stride=SUBLANES), :] = val[:, c * LANES:(c + 1) * LANES]


def _load_tile_rows(ref, n):
    return jnp.concatenate([ref[pl.ds(c, n, stride=SUBLANES), :] for c in range(SUBLANES)], axis=1)


def _tile_row(ref, row):
    return ref.at[pl.ds(pl.multiple_of(row * SUBLANES, SUBLANES), SUBLANES), :]


def _tile_rows(ref, first_row, n):
    return ref.at[pl.ds(pl.multiple_of(first_row * SUBLANES, SUBLANES), n * SUBLANES), :]


def _in_proj(x, n1g_ref, w_in_ref, sgng_ref):
    h = _rms(x, n1g_ref[...])
    cols = jnp.dot(h.astype(BF16), w_in_ref[...], preferred_element_type=F32)
    q = cols[:, :ATTN_WIDTH] * (HEAD_DIM ** -0.5)
    k = cols[:, ATTN_WIDTH:ATTN_WIDTH + KV_WIDTH]
    v = cols[:, ATTN_WIDTH + KV_WIDTH:QKV_COLS]
    gu = _gelu(cols[:, QKV_COLS:QKV_COLS + SGU_WIDTH])
    gv = _gelu(cols[:, QKV_COLS + SGU_WIDTH:])
    sg = sgng_ref[...]
    gvn = jnp.concatenate(
        [_rms(gv[:, i * SGU_HEAD_DIM:(i + 1) * SGU_HEAD_DIM], sg[:, i * SGU_HEAD_DIM:(i + 1) * SGU_HEAD_DIM])
         for i in range(SGU_HEADS)], axis=1)
    return q, k, v, gu, gvn


def _dup_heads(t):
    lane = lax.broadcasted_iota(I32, t.shape, 1)
    r = pltpu.roll(t, HEAD_DIM, axis=1)
    return jnp.where(lane < HEAD_DIM, t, r), jnp.where(lane < HEAD_DIM, r, t)


def _attend_chunks(q_chunks, kk, vv, alibi_ref, sink_ref, key_bias):
    lane = lax.broadcasted_iota(I32, (CHUNK, 2 * HEAD_DIM), 1)
    lo = lane < HEAD_DIM
    rows_per_chunk = Q_PER_KV * CHUNK
    pairs = [[] for _ in q_chunks]
    for g in range(N_KV_HEADS):
        rows = []
        for q64 in q_chunks:
            for p in range(2):
                qp = q64[:, (2 * g + p) * 128:(2 * g + p + 1) * 128]
                rows.append(jnp.where(lo, qp, 0.0))
                rows.append(jnp.where(lo, 0.0, qp))
        q4 = jnp.concatenate(rows, axis=0).astype(BF16)
        sc = lax.dot_general(q4, kk[g], (((1,), (1,)), ((), ())), preferred_element_type=F32)
        sc = sc + alibi_ref[g]
        if key_bias is not None:
            sc = sc + key_bias
        sink = sink_ref[g]
        m = jnp.maximum(jnp.max(sc, axis=1, keepdims=True), sink)
        p_un = jnp.exp(sc - m)
        den = jnp.sum(p_un, axis=1, keepdims=True) + jnp.exp(sink - m)
        o = jnp.dot(p_un.astype(BF16), vv[g], preferred_element_type=F32) / den
        for c in range(len(q_chunks)):
            for p in range(2):
                first = c * rows_per_chunk + 2 * p * CHUNK
                pairs[c].append(jnp.where(lo, o[first:first + CHUNK], o[first + CHUNK:first + 2 * CHUNK]))
    return [jnp.concatenate(chunk_pairs, axis=1) for chunk_pairs in pairs]


def _sgu_weights(sguw_ref):
    sub_i = lax.broadcasted_iota(I32, (SGU_CHUNK, SGU_CHUNK), 0) // CHUNK
    sub_j = lax.broadcasted_iota(I32, (SGU_CHUNK, SGU_CHUNK), 1) // CHUNK
    keep = sub_j <= sub_i
    return [jnp.where(keep, sguw_ref[i], 0.0).astype(BF16) for i in range(SGU_HEADS)]


def _sgu_chunk(ws, gu_c, gvn_c, sgubt_ref):
    outs = []
    for i in range(SGU_HEADS):
        sl = slice(i * SGU_HEAD_DIM, (i + 1) * SGU_HEAD_DIM)
        sp = jnp.dot(ws[i], gvn_c[:, sl].astype(BF16), preferred_element_type=F32) + sgubt_ref[:, i:i + 1]
        outs.append(gu_c[:, sl] * sp)
    return jnp.concatenate(outs, axis=1)


def _merge_route(x, attn, sgu, tok_base, slot, aog_ref, sog_ref, w_out_ref, n2g_ref, wr_ref, br_ref, earlier_ref,
                 cnt_scr, cur_scr, free_scr, table_scr, x1_ref, gate_ref, stage, dest_vmem):
    n = x.shape[0]
    a_n = _rms(attn, aog_ref[...]).astype(BF16)
    s_n = _rms(sgu, sog_ref[...]).astype(BF16)
    x1 = (x + jnp.dot(a_n, w_out_ref[:ATTN_WIDTH, :], preferred_element_type=F32)
          + jnp.dot(s_n, w_out_ref[ATTN_WIDTH:, :], preferred_element_type=F32))
    x1_ref[...] = x1
    h2 = _rms(x1, n2g_ref[...]).astype(BF16)
    logits = lax.dot_general(wr_ref[...], h2, (((1,), (1,)), ((), ())), preferred_element_type=F32) + br_ref[...]
    e_iota = lax.broadcasted_iota(I32, (N_EXPERTS, n), 0).astype(F32)
    cur = logits
    vals, idxs, sels = [], [], []
    for _ in range(TOP_K):
        m = jnp.max(cur, axis=0, keepdims=True)
        idx = jnp.min(jnp.where(cur == m, e_iota, float(N_EXPERTS)), axis=0, keepdims=True)
        sel = e_iota == idx
        vals.append(m)
        idxs.append(idx)
        sels.append(sel)
        cur = jnp.where(sel, -jnp.inf, cur)
    exps = [jnp.exp(v - vals[0]) for v in vals]
    den = exps[0] + exps[1] + exps[2] + exps[3]
    gate_ref[...] = jnp.concatenate([e / den for e in exps], axis=0)
    onehot = jnp.where(sels[0] | sels[1] | sels[2] | sels[3], 1.0, 0.0)
    count = cnt_scr[...]
    rank = jnp.dot(onehot.astype(BF16), earlier_ref[...], preferred_element_type=F32) + count
    added = jnp.sum(onehot, axis=1, keepdims=True)
    inv_page = 1.0 / EXPERT_ROWS
    had_pages = jnp.floor((count + (EXPERT_ROWS - 1)) * inv_page)
    need_pages = jnp.floor((count + added + (EXPERT_ROWS - 1)) * inv_page)
    new_pages = need_pages - had_pages
    below = (lax.broadcasted_iota(I32, (N_EXPERTS, N_EXPERTS), 1)
             < lax.broadcasted_iota(I32, (N_EXPERTS, N_EXPERTS), 0))
    first_new = free_scr[...] + jnp.dot(
        jnp.where(below, 1.0, 0.0).astype(BF16), jnp.broadcast_to(new_pages, (N_EXPERTS, LANES)).astype(BF16),
        preferred_element_type=F32)[:, 0:1]
    page_idx = jnp.floor(rank * inv_page)
    page = jnp.where(page_idx < had_pages, cur_scr[...], first_new + (page_idx - had_pages))
    place = page * EXPERT_ROWS + (rank - page_idx * EXPERT_ROWS)
    dest_vmem[slot] = jnp.concatenate(
        [jnp.sum(jnp.where(s, place, 0.0), axis=0, keepdims=True) for s in sels], axis=0).astype(I32)
    lane = lax.broadcasted_iota(I32, (N_EXPERTS, LANES), 1).astype(F32)
    table_scr[...] = jnp.where((lane >= had_pages) & (lane < need_pages), first_new + (lane - had_pages),
                               table_scr[...])
    cur_scr[...] = jnp.where(new_pages > 0, first_new + new_pages - 1.0, cur_scr[...])
    cnt_scr[...] = count + added
    free_scr[...] = free_scr[...] + jnp.sum(new_pages, axis=0, keepdims=True)

    words = lax.bitcast_convert_type(h2.astype(F32), I32)
    tok = (lax.broadcasted_iota(I32, (1, n), 1) + tok_base).astype(F32)
    tag_t = jnp.concatenate([tok] + idxs + [jnp.zeros((LANES - 1 - TOP_K, n), F32)], axis=0)
    tagged = words[:, :LANES] | tag_t.T.astype(I32)
    _store_tile_rows(stage, slot * TOKEN_BLOCK, jnp.concatenate([tagged, words[:, LANES:]], axis=1))


def _mixer_kernel(xp_ref, xs_in_ref, ck_ref, cv_ref, n1g_ref, w_in_ref, sgng_ref, sink_ref, alibi_ref, sink2_ref,
                  alibi2_ref, sguw_ref, sgubt_ref, aog_ref, sog_ref, w_out_ref, n2g_ref, wr_ref, br_ref, earlier_ref,
                  x1_ref, gate_ref, cnt_ref, table_ref, kp_ref, vp_ref, ks_ref, vs_ref, gv_ref, xs_ref,
                  kk0, kk1, vv0, vv1, cnt_scr, cur_scr, free_scr, table_scr, stage, zero_page,
                  dest_vmem, dest_smem, meta_vmem, meta_smem, sem_rows, sem_dest, sem_fill):
    n = pl.program_id(0)
    s = n % SEQ_BLOCKS
    slot = n % STAGES
    leaving = (n + STAGES - 2) % STAGES
    weights = (aog_ref, sog_ref, w_out_ref, n2g_ref, wr_ref, br_ref, earlier_ref)
    state = (cnt_scr, cur_scr, free_scr, table_scr)

    def dump_row(st):
        return DUMP_PAGE * EXPERT_ROWS + st * BLOCK_COPIES

    def dest_copy(st):
        return pltpu.make_async_copy(dest_vmem.at[st], dest_smem.at[st], sem_dest.at[st])

    def rows_wait(st):
        for _ in range(TOP_K):
            pltpu.make_async_copy(_tile_rows(stage, 0, TOKEN_BLOCK), _tile_rows(xs_ref, 0, TOKEN_BLOCK),
                                  sem_rows.at[st]).wait()

    def row_copy(st, kk, t):
        return pltpu.make_async_copy(_tile_row(stage, st * TOKEN_BLOCK + t), _tile_row(xs_ref, dest_smem[st, kk, t]),
                                     sem_rows.at[st])

    def send_rows(st):
        dest_copy(st).wait()
        for t in range(TOKEN_BLOCK):
            for kk in range(TOP_K):
                row_copy(st, kk, t).start(priority=kk % 2)

    def send_rows_looped(st):
        dest_copy(st).wait()

        def issue(t, carry):
            for kk in range(TOP_K):
                row_copy(st, kk, t).start(priority=kk % 2)
            return carry

        lax.fori_loop(0, TOKEN_BLOCK, issue, 0, unroll=ISSUE_UNROLL)

    @pl.when(n == 0)
    def _():
        for ref in state:
            ref[...] = jnp.zeros_like(ref)
        zero_page[...] = jnp.zeros_like(zero_page)
        for buf in (kk0, kk1, vv0, vv1):
            buf[0:WINDOW, :] = jnp.zeros((WINDOW, 2 * HEAD_DIM), BF16)
        sent_rows = 2 * TOKEN_BLOCK * SUBLANES
        stage[pl.ds((STAGES - 2) * TOKEN_BLOCK * SUBLANES, sent_rows), :] = jnp.zeros((sent_rows, LANES), I32)
        for st in range(STAGES):
            if st < STAGES - 2:
                for first in range(0, BLOCK_COPIES, EXPERT_ROWS):
                    pltpu.make_async_copy(zero_page, _tile_rows(xs_ref, dump_row(st) + first, EXPERT_ROWS),
                                          sem_rows.at[st]).start()
            else:
                dest_vmem[st] = (dump_row(st) + lax.broadcasted_iota(I32, (TOP_K, TOKEN_BLOCK), 0) * TOKEN_BLOCK
                                 + lax.broadcasted_iota(I32, (TOP_K, TOKEN_BLOCK), 1))
                dest_copy(st).start()

    @pl.when(n < PROMPT_BLOCKS)
    def _():
        rows_wait(slot)
        send_rows(leaving)
        x = xp_ref[...]
        q, k, v, gu, gvn = _in_proj(x, n1g_ref, w_in_ref, sgng_ref)
        kp_ref[...] = k[TOKEN_BLOCK - WINDOW:, :]
        vp_ref[...] = v[TOKEN_BLOCK - WINDOW:, :]

        k0, k1 = _dup_heads(k)
        v0, v1 = _dup_heads(v)
        for buf, val in ((kk0, k0), (kk1, k1), (vv0, v0), (vv1, v1)):
            buf[WINDOW:, :] = val.astype(BF16)

        attn_chunks = []
        for j in range(0, TOKEN_BLOCK // CHUNK, 2):
            band = slice(j * CHUNK, j * CHUNK + PAIR_BAND)
            key_bias = None
            if j * CHUNK < WINDOW:
                key_pos = lax.broadcasted_iota(I32, (1, PAIR_BAND), 1) + (s * TOKEN_BLOCK + j * CHUNK - WINDOW)
                key_bias = jnp.where(key_pos < 0, NEG_INF, 0.0)
            attn_chunks += _attend_chunks([q[j * CHUNK:(j + 1) * CHUNK], q[(j + 1) * CHUNK:(j + 2) * CHUNK]],
                                          (kk0[band, :], kk1[band, :]), (vv0[band, :], vv1[band, :]),
                                          alibi2_ref, sink2_ref, key_bias)
        attn = jnp.concatenate(attn_chunks, axis=0)

        for buf in (kk0, kk1, vv0, vv1):
            tail = buf[TOKEN_BLOCK:TOKEN_BLOCK + WINDOW, :]
            buf[0:WINDOW, :] = jnp.where(s == SEQ_BLOCKS - 1, jnp.zeros_like(tail), tail)

        ws = _sgu_weights(sguw_ref)
        sgu = jnp.concatenate(
            [_sgu_chunk(ws, gu[c * SGU_CHUNK:(c + 1) * SGU_CHUNK], gvn[c * SGU_CHUNK:(c + 1) * SGU_CHUNK],
                        sgubt_ref) for c in range(TOKEN_BLOCK // SGU_CHUNK)], axis=0)
        _merge_route(x, attn, sgu, n * TOKEN_BLOCK, slot, *weights, *state, x1_ref, gate_ref, stage, dest_vmem)
        dest_copy(slot).start()

    @pl.when(n == PROMPT_BLOCKS)
    def _():
        rows_wait(slot)
        send_rows(leaving)
        send_rows((N_TOKEN_BLOCKS - 2) % STAGES)
        x = xs_in_ref[...]
        q, k, v, gu, gvn = _in_proj(x, n1g_ref, w_in_ref, sgng_ref)
        gv_ref[...] = gvn
        ws = _sgu_weights(sguw_ref)
        attn_chunks, sgu_chunks = [], []
        pad = jnp.zeros((SGU_CHUNK - DEC_SEQ, SGU_WIDTH), F32)
        for b in range(DEC_BATCH):
            rows = slice(b * DEC_SEQ, (b + 1) * DEC_SEQ)
            k_all = jnp.concatenate([ck_ref[b], k[rows]], axis=0)
            v_all = jnp.concatenate([cv_ref[b], v[rows]], axis=0)
            ks_ref[b] = k_all[DEC_SEQ:]
            vs_ref[b] = v_all[DEC_SEQ:]
            k0, k1 = _dup_heads(k_all)
            v0, v1 = _dup_heads(v_all)
            attn_chunks += _attend_chunks([q[rows]], (k0.astype(BF16), k1.astype(BF16)),
                                          (v0.astype(BF16), v1.astype(BF16)), alibi_ref, sink_ref, None)
            gated = _sgu_chunk(ws, jnp.concatenate([gu[rows], pad], axis=0),
                               jnp.concatenate([gvn[rows], pad], axis=0), sgubt_ref)
            sgu_chunks.append(gated[:DEC_SEQ])
        attn = jnp.concatenate(attn_chunks, axis=0)
        sgu = jnp.concatenate(sgu_chunks, axis=0)
        _merge_route(x, attn, sgu, n * TOKEN_BLOCK, slot, *weights, *state, x1_ref, gate_ref, stage, dest_vmem)
        dest_copy(slot).start()

        send_rows_looped((N_TOKEN_BLOCKS - 1) % STAGES)
        for st in range(STAGES):
            rows_wait(st)
        count = cnt_scr[...]
        left = count - jnp.floor(count * (1.0 / EXPERT_ROWS)) * EXPERT_ROWS
        lane = lax.broadcasted_iota(I32, (N_EXPERTS, LANES), 1)
        meta = jnp.where(lane == 0, cur_scr[...], jnp.where(lane == 1, left, jnp.where(lane == 2, free_scr[...], 0.0)))
        meta_t = jnp.concatenate([meta, jnp.zeros((LANES - N_EXPERTS, LANES), F32)], axis=0).T
        meta_vmem[...] = meta_t[0:SUBLANES].astype(I32)
        to_smem = pltpu.make_async_copy(meta_vmem, meta_smem, sem_dest.at[0])
        to_smem.start()
        to_smem.wait()

        def zero_fill(first_row, rows):
            return pltpu.make_async_copy(_tile_rows(zero_page, 0, rows), _tile_rows(xs_ref, first_row, rows), sem_fill)

        bits = [1 << b for b in reversed(range(EXPERT_ROWS.bit_length() - 1))]
        for wait in (False, True):
            for e in range(N_EXPERTS):
                used = meta_smem[1, e]
                left = jnp.where(used > 0, EXPERT_ROWS - used, 0)
                row = meta_smem[0, e] * EXPERT_ROWS + used
                for bit in bits:
                    take = (left & bit) != 0

                    @pl.when(take)
                    def _():
                        if wait:
                            zero_fill(0, bit).wait()
                        else:
                            zero_fill(row, bit).start()

                    row = row + jnp.where(take, bit, 0)
        in_use = meta_smem[2, 0]
        for j in range(N_PAGES - MIN_PAGES):
            @pl.when(in_use + j < N_PAGES)
            def _():
                pltpu.make_async_copy(zero_page, _tile_rows(xs_ref, (in_use + j) * EXPERT_ROWS, EXPERT_ROWS),
                                      sem_fill).start()
        for j in range(N_PAGES - MIN_PAGES):
            @pl.when(in_use + j < N_PAGES)
            def _():
                pltpu.make_async_copy(zero_page, _tile_rows(xs_ref, 0, EXPERT_ROWS), sem_fill).wait()

    cnt_ref[...] = cnt_scr[...].astype(I32)
    table_ref[...] = table_scr[...].astype(I32)


def _expert_kernel(bstart_ref, nrow_ref, page_ref, xs_ref, wgu_ref, bgu_ref, wd_ref, bd_ref, out_ref,
                   wgu_bf, wd_bf, xbuf, stage, ids_vmem, ids_smem, sem_x, sem_rows, sem_ids):
    expert = pl.program_id(0)
    first_block = bstart_ref[expert]
    end_block = bstart_ref[expert + 1]
    nv = bstart_ref[N_EXPERTS]
    block_words = EXPERT_ROWS * SUBLANES

    def x_copy(g, s):
        return pltpu.make_async_copy(_tile_rows(xs_ref, page_ref[g] * EXPERT_ROWS, EXPERT_ROWS), xbuf.at[s],
                                     sem_x.at[s])

    def dump_row(s):
        return DUMP_ROW + s * EXPERT_ROWS

    def ids_copy(s):
        return pltpu.make_async_copy(ids_vmem.at[s], ids_smem.at[s], sem_ids.at[s])

    def rows_wait(s):
        whole = pl.ds(0, block_words)
        pltpu.make_async_copy(stage.at[whole, :], out_ref.at[whole, :], sem_rows.at[s]).wait()

    def issue_rows(s):
        ids_copy(s).wait()
        for r in range(EXPERT_ROWS):
            pltpu.make_async_copy(_tile_row(stage, s * EXPERT_ROWS + r), _tile_row(out_ref, ids_smem[s, 0, r]),
                                  sem_rows.at[s]).start(priority=r % 2)

    def compute(g, xs_slot, s):
        words = _load_tile_rows(xbuf.at[xs_slot], EXPERT_ROWS)
        tag = words[:, :LANES] & TAG_MASK
        x = lax.bitcast_convert_type(words & ~TAG_MASK, F32).astype(BF16)
        half = EXPERT_ROWS // 2
        for h in range(2):
            gu = jnp.dot(x[h * half:(h + 1) * half], wgu_bf[...], preferred_element_type=F32) + bgu_ref[...]
            x_glu = jnp.minimum(gu[:, :D_FF], SWIGLU_LIMIT)
            x_lin = jnp.clip(gu[:, D_FF:], -SWIGLU_LIMIT, SWIGLU_LIMIT)
            act = x_glu * (1.0 / (1.0 + jnp.exp(-SWIGLU_ALPHA * x_glu))) * (x_lin + 1.0)
            out = jnp.dot(act.astype(BF16), wd_bf[...], preferred_element_type=F32) + bd_ref[...]
            _store_tile_rows(stage, s * EXPERT_ROWS + h * half, out)
        k_slot = sum(j * jnp.where(tag[:, 1 + j:2 + j] == expert, 1, 0) for j in range(1, TOP_K))
        row = lax.broadcasted_iota(I32, (EXPERT_ROWS, 1), 0)
        dst = jnp.where(row < nrow_ref[g], k_slot * T_ALL + tag[:, 0:1], dump_row(s) + row)
        dst_t = jnp.broadcast_to(dst.astype(F32), (EXPERT_ROWS, LANES)).T
        ids_vmem[s] = dst_t[0:SUBLANES].astype(I32)
        ids_copy(s).start()

    @pl.when(expert == 0)
    def _():
        stage[...] = jnp.zeros_like(stage)
        for s in range(STAGES):
            if s < STAGES - 2:
                pltpu.make_async_copy(stage.at[pl.ds(s * block_words, block_words), :],
                                      out_ref.at[pl.ds(dump_row(s) * SUBLANES, block_words), :],
                                      sem_rows.at[s]).start()
            else:
                ids_vmem[s] = dump_row(s) + lax.broadcasted_iota(I32, (SUBLANES, EXPERT_ROWS), 1)
                ids_copy(s).start()
        x_copy(0, 0).start()

    @pl.when(end_block > first_block)
    def _():
        rows = 128

        def cast_rows(c, carry):
            r = pl.multiple_of(c * rows, rows)
            wgu_bf[pl.ds(r, rows), :] = wgu_ref[pl.ds(r, rows), :].astype(BF16)
            wd_bf[pl.ds(r, rows), :] = wd_ref[pl.ds(r, rows), :].astype(BF16)
            return carry

        lax.fori_loop(0, D_MODEL // rows, cast_rows, 0)

    def block_step(g, carry):
        xs_slot = g % 2
        slot = g % STAGES
        x_copy(g, xs_slot).wait()
        x_copy(jnp.minimum(g + 1, nv - 1), 1 - xs_slot).start()
        rows_wait(slot)
        issue_rows((g + STAGES - 2) % STAGES)
        compute(g, xs_slot, slot)
        return carry

    lax.fori_loop(first_block, end_block, block_step, 0)

    @pl.when(expert == N_EXPERTS - 1)
    def _():
        x_copy(nv - 1, nv % 2).wait()
        issue_rows((nv + STAGES - 2) % STAGES)
        issue_rows((nv + STAGES - 1) % STAGES)
        for s in range(STAGES):
            rows_wait(s)


def _combine_kernel(o0_ref, o1_ref, o2_ref, o3_ref, gate_ref, x1_ref, fg_ref, yp_ref, ys_ref):
    i = pl.program_id(0)
    gates = jnp.concatenate([gate_ref[...], jnp.zeros((LANES - TOP_K, TOKEN_BLOCK), F32)], axis=0).T
    y = x1_ref[...]
    for kk, o_ref in enumerate((o0_ref, o1_ref, o2_ref, o3_ref)):
        y = y + gates[:, kk:kk + 1] * _load_tile_rows(o_ref, TOKEN_BLOCK)
    out = _rms(y, fg_ref[...])

    @pl.when(i < PROMPT_BLOCKS)
    def _():
        yp_ref[...] = out

    @pl.when(i >= PROMPT_BLOCKS)
    def _():
        ys_ref[...] = out


def _full(shape):
    return pl.BlockSpec(shape, lambda *_: (0,) * len(shape))


def kernel(x_prompt, x_sample, cache_k, cache_v, norm1_g, w_in, sgu_norm_g, attn_sinks, sgu_w, sgu_b, attn_out_g,
           sgu_out_g, w_out, norm2_g, w_router, b_router, w_gate_up, b_gate_up, w_down, b_down, final_g):
    n1g = norm1_g[0].reshape(1, D_MODEL)
    w_in_b = w_in[0].astype(BF16)
    sgng = sgu_norm_g[0].reshape(1, SGU_WIDTH)
    sink_cols = jnp.repeat(attn_sinks[0].reshape(N_KV_HEADS, Q_PER_KV, 1), CHUNK, axis=1).reshape(
        N_KV_HEADS, Q_PER_KV * CHUNK, 1)
    slopes = jnp.exp2(-8.0 * jnp.arange(1, N_Q_HEADS + 1, dtype=F32) / N_Q_HEADS).reshape(N_KV_HEADS, Q_PER_KV)
    dist = jnp.abs(jnp.arange(CHUNK)[:, None] + WINDOW - jnp.arange(BAND)[None, :]).astype(F32)
    alibi = (-slopes[:, :, None, None] * dist).reshape(N_KV_HEADS, Q_PER_KV * CHUNK, BAND)
    sguw = sgu_w[0]
    sgubt = sgu_b[0].T
    aog = attn_out_g[0].reshape(1, ATTN_WIDTH)
    sog = sgu_out_g[0].reshape(1, SGU_WIDTH)
    w_out_b = w_out[0].astype(BF16)
    n2g = norm2_g[0].reshape(1, D_MODEL)
    wr_t = w_router[0].T.astype(BF16)
    br_c = b_router[0].reshape(N_EXPERTS, 1)
    fg = final_g.reshape(1, D_MODEL)

    weight_specs = [
        _full((1, D_MODEL)), _full((D_MODEL, IN_COLS)), _full((1, SGU_WIDTH)),
        _full((N_KV_HEADS, Q_PER_KV * CHUNK, 1)), _full((N_KV_HEADS, Q_PER_KV * CHUNK, BAND)),
        _full((N_KV_HEADS, 2 * Q_PER_KV * CHUNK, 1)), _full((N_KV_HEADS, 2 * Q_PER_KV * CHUNK, PAIR_BAND)),
        _full((SGU_HEADS, SGU_CHUNK, SGU_CHUNK)), _full((SGU_CHUNK, SGU_HEADS)),
        _full((1, ATTN_WIDTH)), _full((1, SGU_WIDTH)), _full((D_MODEL, D_MODEL)), _full((1, D_MODEL)),
        _full((N_EXPERTS, D_MODEL)), _full((N_EXPERTS, 1)), _full((TOKEN_BLOCK, TOKEN_BLOCK)),
    ]
    own_key = jnp.arange(PAIR_BAND)[None, None, :] - CHUNK * jnp.arange(2)[:, None, None]
    in_band = (own_key >= 0) & (own_key < BAND)
    dist2 = jnp.abs(jnp.arange(CHUNK)[None, :, None] + WINDOW - own_key).astype(F32)
    alibi2 = jnp.where(in_band[None, None], -slopes[:, :, None, None, None] * dist2[None, None], NEG_INF)
    alibi2 = alibi2.transpose(0, 2, 1, 3, 4).reshape(N_KV_HEADS, 2 * Q_PER_KV * CHUNK, PAIR_BAND)
    sink2 = jnp.concatenate([sink_cols, sink_cols], axis=1)
    token = jnp.arange(TOKEN_BLOCK)
    earlier = (token[:, None] < token[None, :]).astype(BF16)
    weights = (n1g, w_in_b, sgng, sink_cols, alibi, sink2, alibi2, sguw, sgubt, aog, sog, w_out_b, n2g, wr_t, br_c,
               earlier)
    cparams = functools.partial(pltpu.CompilerParams, vmem_limit_bytes=VMEM_LIMIT_BYTES)
    any_spec = pl.BlockSpec(memory_space=pl.ANY)

    def tok_map(n):
        return (n, 0)

    def tok_map_t(n):
        return (0, n)

    def stream_map(n):
        return (jnp.minimum(n // SEQ_BLOCKS, BATCH - 1), 0, 0)

    kv_cache = (DEC_BATCH, WINDOW, KV_WIDTH)
    bf_band = pltpu.VMEM((WINDOW + TOKEN_BLOCK, 2 * HEAD_DIM), BF16)
    state_col = pltpu.VMEM((N_EXPERTS, 1), F32)
    (x1_all, gate_all, cnt_all, table, kp_new, vp_new, ks_new, vs_new, gv_s, xs) = pl.pallas_call(
        _mixer_kernel,
        grid=(N_TOKEN_BLOCKS,),
        in_specs=[pl.BlockSpec((TOKEN_BLOCK, D_MODEL), lambda n: (jnp.minimum(n, PROMPT_BLOCKS - 1), 0)),
                  _full((T_SAMPLE, D_MODEL)), _full(kv_cache), _full(kv_cache)] + weight_specs,
        out_specs=[
            pl.BlockSpec((TOKEN_BLOCK, D_MODEL), tok_map),
            pl.BlockSpec((TOP_K, TOKEN_BLOCK), tok_map_t),
            _full((N_EXPERTS, 1)), _full((N_EXPERTS, LANES)),
            pl.BlockSpec((None, WINDOW, KV_WIDTH), stream_map), pl.BlockSpec((None, WINDOW, KV_WIDTH), stream_map),
            _full(kv_cache), _full(kv_cache), _full((T_SAMPLE, SGU_WIDTH)),
            any_spec,
        ],
        out_shape=(
            jax.ShapeDtypeStruct((T_ALL, D_MODEL), F32),
            jax.ShapeDtypeStruct((TOP_K, T_ALL), F32),
            jax.ShapeDtypeStruct((N_EXPERTS, 1), I32),
            jax.ShapeDtypeStruct((N_EXPERTS, LANES), I32),
            jax.ShapeDtypeStruct((BATCH, WINDOW, KV_WIDTH), F32),
            jax.ShapeDtypeStruct((BATCH, WINDOW, KV_WIDTH), F32),
            jax.ShapeDtypeStruct(kv_cache, F32),
            jax.ShapeDtypeStruct(kv_cache, F32),
            jax.ShapeDtypeStruct((T_SAMPLE, SGU_WIDTH), F32),
            jax.ShapeDtypeStruct((XS_PAGES * EXPERT_ROWS * SUBLANES, LANES), I32),
        ),
        scratch_shapes=[
            bf_band, bf_band, bf_band, bf_band,
            state_col, state_col, pltpu.VMEM((1, 1), F32), pltpu.VMEM((N_EXPERTS, LANES), F32),
            pltpu.VMEM((STAGES * TOKEN_BLOCK * SUBLANES, LANES), I32),
            pltpu.VMEM((EXPERT_ROWS * SUBLANES, LANES), I32),
            pltpu.VMEM((STAGES, TOP_K, TOKEN_BLOCK), I32), pltpu.SMEM((STAGES, TOP_K, TOKEN_BLOCK), I32),
            pltpu.VMEM((SUBLANES, LANES), I32), pltpu.SMEM((SUBLANES, LANES), I32),
            pltpu.SemaphoreType.DMA((STAGES,)), pltpu.SemaphoreType.DMA((STAGES,)), pltpu.SemaphoreType.DMA,
        ],
        compiler_params=cparams(dimension_semantics=("arbitrary",)),
        name="mixer",
    )(x_prompt.reshape(T_PROMPT, D_MODEL), x_sample.reshape(T_SAMPLE, D_MODEL),
      cache_k[0].reshape(kv_cache), cache_v[0].reshape(kv_cache), *weights)

    counts = cnt_all[:, 0]
    blocks_per = (counts + EXPERT_ROWS - 1) // EXPERT_ROWS
    block_end = jnp.cumsum(blocks_per)
    block_start = block_end - blocks_per
    n_valid = block_end[-1:].astype(I32)
    step = jnp.minimum(jnp.arange(N_PAGES, dtype=I32), n_valid - 1)
    expert_of_step = step[:, None] >= block_end[None, :]
    blk_e = jnp.minimum(jnp.sum(expert_of_step, axis=1), N_EXPERTS - 1).astype(I32)
    own = (jnp.arange(N_EXPERTS, dtype=I32)[None, :] == blk_e[:, None]).astype(I32)
    blk_in_expert = step - jnp.sum(own * block_start[None, :], axis=1)
    rows_left = jnp.sum(own * counts[None, :], axis=1) - blk_in_expert * EXPERT_ROWS
    blk_rows = jnp.clip(rows_left, 0, EXPERT_ROWS).astype(I32)
    blk_page = table.reshape(-1)[blk_e * LANES + blk_in_expert].astype(I32)
    block_bounds = jnp.concatenate([jnp.zeros((1,), I32), block_end.astype(I32)])

    def expert_map(e, bounds, nr, pg):
        return (e, 0, 0)

    out_kt = pl.pallas_call(
        _expert_kernel,
        grid_spec=pltpu.PrefetchScalarGridSpec(
            num_scalar_prefetch=3,
            grid=(N_EXPERTS,),
            in_specs=[
                any_spec,
                pl.BlockSpec((None, D_MODEL, 2 * D_FF), expert_map),
                pl.BlockSpec((None, 1, 2 * D_FF), expert_map),
                pl.BlockSpec((None, D_FF, D_MODEL), expert_map),
                pl.BlockSpec((None, 1, D_MODEL), expert_map),
            ],
            out_specs=any_spec,
            scratch_shapes=[
                pltpu.VMEM((D_MODEL, 2 * D_FF), BF16), pltpu.VMEM((D_FF, D_MODEL), BF16),
                pltpu.VMEM((2, EXPERT_ROWS * SUBLANES, LANES), I32),
                pltpu.VMEM((STAGES * EXPERT_ROWS * SUBLANES, LANES), F32),
                pltpu.VMEM((STAGES, SUBLANES, EXPERT_ROWS), I32),
                pltpu.SMEM((STAGES, SUBLANES, EXPERT_ROWS), I32),
                pltpu.SemaphoreType.DMA((2,)), pltpu.SemaphoreType.DMA((STAGES,)),
                pltpu.SemaphoreType.DMA((STAGES,)),
            ],
        ),
        out_shape=jax.ShapeDtypeStruct((OUT_ROWS * SUBLANES, LANES), F32),
        compiler_params=cparams(dimension_semantics=("arbitrary",)),
        name="experts",
    )(block_bounds, blk_rows, blk_page, xs, w_gate_up[0], b_gate_up[0].reshape(N_EXPERTS, 1, 2 * D_FF), w_down[0],
      b_down[0].reshape(N_EXPERTS, 1, D_MODEL))

    def slot_spec(kk):
        return pl.BlockSpec((TOKEN_BLOCK * SUBLANES, LANES), lambda i: (kk * N_TOKEN_BLOCKS + i, 0))

    y_p, y_s = pl.pallas_call(
        _combine_kernel,
        grid=(N_TOKEN_BLOCKS,),
        in_specs=[slot_spec(kk) for kk in range(TOP_K)] + [
            pl.BlockSpec((TOP_K, TOKEN_BLOCK), lambda i: (0, i)),
            pl.BlockSpec((TOKEN_BLOCK, D_MODEL), lambda i: (i, 0)),
            _full((1, D_MODEL)),
        ],
        out_specs=[
            pl.BlockSpec((TOKEN_BLOCK, D_MODEL), lambda i: (jnp.minimum(i, PROMPT_BLOCKS - 1), 0)),
            pl.BlockSpec((TOKEN_BLOCK, D_MODEL), lambda i: (jnp.maximum(i - PROMPT_BLOCKS, 0), 0)),
        ],
        out_shape=(jax.ShapeDtypeStruct((T_PROMPT, D_MODEL), F32), jax.ShapeDtypeStruct((T_SAMPLE, D_MODEL), F32)),
        compiler_params=cparams(dimension_semantics=("arbitrary",)),
        name="combine",
    )(out_kt, out_kt, out_kt, out_kt, gate_all, x1_all, fg)

    kv5 = (1, -1, WINDOW, N_KV_HEADS, HEAD_DIM)
    return (y_p.reshape(BATCH, SEQ, D_MODEL), y_s.reshape(DEC_BATCH, DEC_SEQ, D_MODEL),
            kp_new.reshape(kv5), vp_new.reshape(kv5), ks_new.reshape(kv5), vs_new.reshape(kv5),
            gv_s.reshape(1, DEC_BATCH, DEC_SEQ, SGU_HEADS, SGU_HEAD_DIM))
```
